```python
import math
import jax, jax.numpy as jnp
from jax import lax
import numpy as np

D_MODEL = 1024
BATCH = 2
SEQ = 8192
DEPTH = 2

HEAD_DIM = 64
N_HEADS = D_MODEL // HEAD_DIM
N_FOX = N_HEADS // 2
N_SB = N_HEADS - N_FOX
FOX_W = N_FOX * HEAD_DIM
SB_W = N_SB * HEAD_DIM
EVEN_IN = 3 * FOX_W + 3 * SB_W + N_FOX
N_Q = N_HEADS
N_KV = 4
GROUP = N_Q // N_KV
ODD_IN = N_Q * HEAD_DIM + 2 * N_KV * HEAD_DIM
WINDOW = 128
BLOCK_Q = 128
ROPE_THETA = 10000.0
D_FF = ((8 * D_MODEL // 3 + 255) // 256) * 256
PLE_DIM = 256
N_EVEN = (DEPTH + 1) // 2
N_ODD = DEPTH // 2
EPS = 1e-6
NEG_INF = -1e30

kernel_name = "hybrid_fox_stickbreak_swa_sink_block"


def _rmsnorm(x, g):
    xf = x.astype(jnp.float32)
    y = xf * lax.rsqrt(jnp.mean(xf * xf, axis=-1, keepdims=True) + EPS)
    return (y * g.astype(jnp.float32)).astype(x.dtype)


def _rope(x, pos):
    half = x.shape[-1] // 2
    inv = ROPE_THETA ** (-jnp.arange(half, dtype=jnp.float32) / half)
    ang = pos.astype(jnp.float32)[..., None] * inv
    cos = jnp.cos(ang)[:, :, None, :]
    sin = jnp.sin(ang)[:, :, None, :]
    xf = x.astype(jnp.float32)
    x1, x2 = xf[..., :half], xf[..., half:]
    out = jnp.concatenate([x1 * cos - x2 * sin, x2 * cos + x1 * sin], axis=-1)
    return out.astype(x.dtype)


def _forgetting_attention(q, k, v, log_f):
    S, d = q.shape[1], q.shape[-1]
    scale = d ** -0.5
    cum = jnp.cumsum(log_f, axis=1).transpose(0, 2, 1)
    outs = []
    for i in range(S // BLOCK_Q):
        q0, q1 = i * BLOCK_Q, (i + 1) * BLOCK_Q
        s = jnp.einsum('bqhd,bkhd->bhqk', q[:, q0:q1], k[:, :q1],
                       preferred_element_type=jnp.float32) * scale
        s = s + cum[:, :, q0:q1, None] - cum[:, :, None, :q1]
        causal = jnp.arange(q1)[None, :] <= jnp.arange(q0, q1)[:, None]
        s = jnp.where(causal, s, NEG_INF)
        w = jax.nn.softmax(s, axis=-1)
        outs.append(jnp.einsum('bhqk,bkhd->bqhd', w.astype(v.dtype), v[:, :q1]))
    return jnp.concatenate(outs, axis=1)


def _stick_breaking_attention(q, k, v):
    S, d = q.shape[1], q.shape[-1]
    scale = d ** -0.5
    outs = []
    for i in range(S // BLOCK_Q):
        q0, q1 = i * BLOCK_Q, (i + 1) * BLOCK_Q
        z = jnp.einsum('bqhd,bkhd->bhqk', q[:, q0:q1], k[:, :q1],
                       preferred_element_type=jnp.float32) * scale
        strict = jnp.arange(q1)[None, :] < jnp.arange(q0, q1)[:, None]
        log_1mb = jnp.where(strict, jax.nn.log_sigmoid(-z), 0.0)
        suffix = lax.cumsum(log_1mb, axis=3, reverse=True) - log_1mb
        a = jnp.where(strict, jnp.exp(jax.nn.log_sigmoid(z) + suffix), 0.0)
        outs.append(jnp.einsum('bhqk,bkhd->bqhd', a.astype(v.dtype), v[:, :q1]))
    return jnp.concatenate(outs, axis=1)


def _sliding_window_sink_attention(q, k, v, sinks):
    B, S, _, d = q.shape
    nb = S // WINDOW
    scale = d ** -0.5
    qb = q.reshape(B, nb, WINDOW, N_KV, GROUP, d)

    def band(x):
        xb = x.reshape(B, nb, WINDOW, N_KV, d)
        prev = jnp.concatenate([jnp.zeros_like(xb[:, :1]), xb[:, :-1]], axis=1)
        return jnp.concatenate([prev, xb], axis=2)

    kb, vb = band(k), band(v)
    s = jnp.einsum('bnqhgd,bnkhd->bnhgqk', qb, kb,
                   preferred_element_type=jnp.float32) * scale
    qi = jnp.arange(WINDOW)[:, None]
    kj = jnp.arange(2 * WINDOW)[None, :]
    rel = qi + WINDOW - kj
    valid = (rel >= 0) & (rel < WINDOW)
    first = (jnp.arange(nb)[:, None, None] == 0) & (kj[None] < WINDOW)
    mask = valid[None] & ~first
    s = jnp.where(mask[None, :, None, None], s, NEG_INF)
    sink = jnp.broadcast_to(
        sinks.astype(jnp.float32).reshape(N_KV, GROUP)[None, None, :, :, None, None],
        s.shape[:-1] + (1,))
    w = jax.nn.softmax(jnp.concatenate([s, sink], axis=-1), axis=-1)[..., :-1]
    o = jnp.einsum('bnhgqk,bnkhd->bnqhgd', w.astype(v.dtype), vb)
    return o.reshape(B, S, N_Q, d)


def _swiglu(x, w_gate, w_up, w_down):
    return (jax.nn.silu(x @ w_gate) * (x @ w_up)) @ w_down


def setup_inputs(seed: int = 0) -> dict:
    key = jax.random.key(seed)
    ks = jax.random.split(key, 20)
    f32 = jnp.float32

    def w(k, shape, fan_in):
        return jax.random.normal(k, shape, f32) * fan_in ** -0.5

    def gain(k, shape):
        return 1.0 + 0.02 * jax.random.normal(k, shape, f32)

    return {
        "x": jax.random.normal(ks[0], (BATCH, SEQ, D_MODEL), f32),
        "p": jax.random.normal(ks[1], (DEPTH, BATCH, SEQ, PLE_DIM), f32),
        "positions": jnp.broadcast_to(jnp.arange(SEQ, dtype=jnp.int32), (BATCH, SEQ)),
        "norm_mix": gain(ks[2], (DEPTH, D_MODEL)),
        "norm_ffn": gain(ks[3], (DEPTH, D_MODEL)),
        "norm_ple": gain(ks[4], (DEPTH, D_MODEL)),
        "norm_final": gain(ks[5], (D_MODEL,)),
        "ev_w_in": w(ks[6], (N_EVEN, D_MODEL, EVEN_IN), D_MODEL),
        "ev_b_f": 0.1 * jax.random.normal(ks[7], (N_EVEN, N_FOX), f32),
        "ev_w_out": w(ks[8], (N_EVEN, FOX_W + SB_W, D_MODEL), FOX_W + SB_W),
        "od_w_in": w(ks[9], (N_ODD, D_MODEL, ODD_IN), D_MODEL),
        "od_sinks": 0.5 * jax.random.normal(ks[10], (N_ODD, N_Q), f32),
        "od_w_out": w(ks[11], (N_ODD, N_Q * HEAD_DIM, D_MODEL), N_Q * HEAD_DIM),
        "ffn_w_gate": w(ks[12], (DEPTH, D_MODEL, D_FF), D_MODEL),
        "ffn_w_up": w(ks[13], (DEPTH, D_MODEL, D_FF), D_MODEL),
        "ffn_w_down": w(ks[14], (DEPTH, D_FF, D_MODEL), D_FF),
        "ple_w_proj": w(ks[15], (DEPTH, PLE_DIM, D_MODEL), PLE_DIM),
        "ple_w_gate": w(ks[16], (DEPTH, D_MODEL, D_MODEL), D_MODEL),
    }


def reference(x, p, positions, norm_mix, norm_ffn, norm_ple, norm_final,
              ev_w_in, ev_b_f, ev_w_out, od_w_in, od_sinks, od_w_out,
              ffn_w_gate, ffn_w_up, ffn_w_down, ple_w_proj, ple_w_gate):
    B, S, _ = x.shape
    h = x
    for i in range(DEPTH):
        hn = _rmsnorm(h, norm_mix[i])
        if i % 2 == 0:
            j = i // 2
            proj = hn @ ev_w_in[j]
            c = [0, FOX_W, 2 * FOX_W, 3 * FOX_W,
                 3 * FOX_W + SB_W, 3 * FOX_W + 2 * SB_W, 3 * FOX_W + 3 * SB_W]
            qa = proj[..., c[0]:c[1]].reshape(B, S, N_FOX, HEAD_DIM)
            ka = proj[..., c[1]:c[2]].reshape(B, S, N_FOX, HEAD_DIM)
            va = proj[..., c[2]:c[3]].reshape(B, S, N_FOX, HEAD_DIM)
            qs = proj[..., c[3]:c[4]].reshape(B, S, N_SB, HEAD_DIM)
            ks_ = proj[..., c[4]:c[5]].reshape(B, S, N_SB, HEAD_DIM)
            vs = proj[..., c[5]:c[6]].reshape(B, S, N_SB, HEAD_DIM)
            log_f = jax.nn.log_sigmoid(
                (proj[..., c[6]:] + ev_b_f[j]).astype(jnp.float32))
            o_fox = _forgetting_attention(qa, ka, va, log_f)
            o_sb = _stick_breaking_attention(qs, ks_, vs)
            mix = jnp.concatenate([o_fox.reshape(B, S, FOX_W),
                                   o_sb.reshape(B, S, SB_W)], axis=-1) @ ev_w_out[j]
        else:
            j = i // 2
            proj = hn @ od_w_in[j]
            qw, kw = N_Q * HEAD_DIM, N_KV * HEAD_DIM
            q = _rope(proj[..., :qw].reshape(B, S, N_Q, HEAD_DIM), positions)
            k = _rope(proj[..., qw:qw + kw].reshape(B, S, N_KV, HEAD_DIM), positions)
            v = proj[..., qw + kw:].reshape(B, S, N_KV, HEAD_DIM)
            o = _sliding_window_sink_attention(q, k, v, od_sinks[j])
            mix = o.reshape(B, S, qw) @ od_w_out[j]
        h = h + mix
        h = h + _swiglu(_rmsnorm(h, norm_ffn[i]), ffn_w_gate[i], ffn_w_up[i], ffn_w_down[i])
        gate = jax.nn.sigmoid(_rmsnorm(h, norm_ple[i]) @ ple_w_gate[i])
        h = h + gate * (p[i] @ ple_w_proj[i])
    return _rmsnorm(h, norm_final)
```

```python
import functools
import math

import jax
import jax.numpy as jnp
from jax import lax
from jax.experimental import pallas as pl
from jax.experimental.pallas import tpu as pltpu

F32 = jnp.float32
BF16 = jnp.bfloat16

HEAD_DIM = 64
LANES = 128
WINDOW = 128
ROPE_THETA = 10000.0
EPS = 1e-6
NEG_INF = -1e30
VMEM_LIMIT = 56 * 1024 * 1024

_NT = (((1,), (1,)), ((), ()))


def _params(sem):
    return pltpu.CompilerParams(dimension_semantics=sem, vmem_limit_bytes=VMEM_LIMIT)


def _rms(x, g):
    return x * lax.rsqrt(jnp.mean(x * x, axis=-1, keepdims=True) + EPS) * g


def _log_sigmoid(x):
    return jnp.minimum(x, 0.0) - jnp.log(1.0 + jnp.exp(-jnp.abs(x)))


def _split3(x):
    a = x.astype(BF16)
    r = x - a.astype(F32)
    b = r.astype(BF16)
    c = (r - b.astype(F32)).astype(BF16)
    return a, b, c


def _pre0_kernel(x_ref, g_ref, w_ref, wf_ref, bf_ref, qkv_ref, fcol_ref, ft_ref,
                 carry_ref, *, n_chunk):
    tm = x_ref.shape[1]

    @pl.when(pl.program_id(1) == 0)
    def _():
        carry_ref[...] = jnp.zeros_like(carry_ref)

    hb = _rms(x_ref[0], g_ref[...]).astype(BF16)
    cw = w_ref.shape[1] // n_chunk
    for c in range(n_chunk):
        qkv_ref[0, :, c * cw:(c + 1) * cw] = jnp.dot(
            hb, w_ref[:, c * cw:(c + 1) * cw], preferred_element_type=F32).astype(BF16)

    gate = jnp.dot(hb, wf_ref[...], preferred_element_type=F32) + bf_ref[...]
    lf = _log_sigmoid(gate)
    row = lax.broadcasted_iota(jnp.int32, (tm, tm), 0)
    col = lax.broadcasted_iota(jnp.int32, (tm, tm), 1)
    tri = jnp.where(row >= col, 1.0, 0.0).astype(BF16)
    cum = carry_ref[...]
    for piece in _split3(lf):
        cum = cum + jnp.dot(tri, piece, preferred_element_type=F32)
    carry_ref[...] = cum[tm - 1:tm, :]
    fcol_ref[0] = cum
    ft_ref[0] = cum.T[0:8, :]


def _pre0(x, g, w_qkv, w_f, b_f, tm):
    B, S, D = x.shape
    N = w_qkv.shape[1]
    return pl.pallas_call(
        functools.partial(_pre0_kernel, n_chunk=N // 512),
        grid=(B, S // tm),
        in_specs=[
            pl.BlockSpec((1, tm, D), lambda b, s: (b, s, 0)),
            pl.BlockSpec((1, D), lambda b, s: (0, 0)),
            pl.BlockSpec((D, N), lambda b, s: (0, 0)),
            pl.BlockSpec((D, LANES), lambda b, s: (0, 0)),
            pl.BlockSpec((1, LANES), lambda b, s: (0, 0)),
        ],
        out_specs=[
            pl.BlockSpec((1, tm, N), lambda b, s: (b, s, 0)),
            pl.BlockSpec((1, tm, LANES), lambda b, s: (b, s, 0)),
            pl.BlockSpec((1, 8, tm), lambda b, s: (b, 0, s)),
        ],
        out_shape=[
            jax.ShapeDtypeStruct((B, S, N), BF16),
            jax.ShapeDtypeStruct((B, S, LANES), F32),
            jax.ShapeDtypeStruct((B, 8, S), F32),
        ],
        scratch_shapes=[pltpu.VMEM((1, LANES), F32)],
        compiler_params=_params(("arbitrary", "arbitrary")),
        name="pre0",
    )(x, g, w_qkv, w_f, b_f)


def _head_masks(shape):
    lane = lax.broadcasted_iota(jnp.int32, shape, len(shape) - 1)
    return lane < HEAD_DIM


def _fox_kernel(q_ref, k_ref, v_ref, fcol_ref, ft_ref, o_ref, m_ref, l_ref, acc_ref, *, bk):
    bq = q_ref.shape[1]
    hp = pl.program_id(1)
    qi = pl.program_id(2)
    q = q_ref[0]
    lo_lanes = _head_masks(q.shape)
    zero = jnp.zeros_like(q)
    qh = (jnp.where(lo_lanes, q, zero), jnp.where(lo_lanes, zero, q))
    fc = fcol_ref[0]
    lane = lax.broadcasted_iota(jnp.int32, fc.shape, 1)
    fq = tuple(jnp.sum(jnp.where(lane == 2 * hp + h, fc, 0.0), axis=-1, keepdims=True)
               for h in range(2))

    m_ref[...] = jnp.full_like(m_ref, NEG_INF)
    l_ref[...] = jnp.zeros_like(l_ref)
    acc_ref[...] = jnp.zeros_like(acc_ref)

    def block(j, masked):
        k0 = pl.multiple_of(j * bk, bk)
        k = k_ref[0, pl.ds(k0, bk), :]
        v = v_ref[0, pl.ds(k0, bk), :]
        for h in range(2):
            s = lax.dot_general(qh[h], k, _NT, preferred_element_type=F32)
            fs = ft_ref[0, pl.ds(2 * hp + h, 1), pl.ds(k0, bk)]
            s = s + (fq[h] - fs)
            if masked:
                t_idx = qi * bq + lax.broadcasted_iota(jnp.int32, s.shape, 0)
                s_idx = k0 + lax.broadcasted_iota(jnp.int32, s.shape, 1)
                s = jnp.where(s_idx <= t_idx, s, NEG_INF)
            m_old = m_ref[h]
            m_new = jnp.maximum(m_old, jnp.max(s, axis=-1, keepdims=True))
            p = jnp.exp(s - m_new)
            alpha = jnp.exp(m_old - m_new)
            l_ref[h] = alpha * l_ref[h] + jnp.sum(p, axis=-1, keepdims=True)
            acc_ref[h] = alpha * acc_ref[h] + jnp.dot(
                p.astype(BF16), v, preferred_element_type=F32)
            m_ref[h] = m_new

    n_diag = bq // bk
    lax.fori_loop(0, qi * n_diag, lambda j, c: (block(j, False), c)[1], 0)
    for d in range(n_diag):
        block(qi * n_diag + d, True)

    o0 = acc_ref[0] / l_ref[0]
    o1 = acc_ref[1] / l_ref[1]
    o_ref[0] = jnp.where(_head_masks(o0.shape), o0, o1).astype(o_ref.dtype)


def _fox(qkv, fcol, ft, bq, bk):
    B, S, _ = qkv.shape
    n_pair = 4
    return pl.pallas_call(
        functools.partial(_fox_kernel, bk=bk),
        grid=(B, n_pair, S // bq),
        in_specs=[
            pl.BlockSpec((1, bq, LANES), lambda b, h, i: (b, i, h)),
            pl.BlockSpec((1, S, LANES), lambda b, h, i: (b, 0, n_pair + h)),
            pl.BlockSpec((1, S, LANES), lambda b, h, i: (b, 0, 2 * n_pair + h)),
            pl.BlockSpec((1, bq, LANES), lambda b, h, i: (b, i, 0)),
            pl.BlockSpec((1, 8, S), lambda b, h, i: (b, 0, 0)),
        ],
        out_specs=pl.BlockSpec((1, bq, LANES), lambda b, h, i: (b, i, h)),
        out_shape=jax.ShapeDtypeStruct((B, S, n_pair * LANES), BF16),
        scratch_shapes=[
            pltpu.VMEM((2, bq, 1), F32),
            pltpu.VMEM((2, bq, 1), F32),
            pltpu.VMEM((2, bq, LANES), F32),
        ],
        compiler_params=_params(("arbitrary", "arbitrary", "arbitrary")),
        name="fox",
    )(qkv, qkv, qkv, fcol, ft)


def _sb_kernel(q_ref, k_ref, v_ref, tri_ref, o_ref, r_ref, acc_ref, *, bk):
    bq = q_ref.shape[1]
    qi = pl.program_id(2)
    q = q_ref[0]
    lo_lanes = _head_masks(q.shape)
    zero = jnp.zeros_like(q)
    qh = (jnp.where(lo_lanes, q, zero), jnp.where(lo_lanes, zero, q))
    tri = tri_ref[...]

    r_ref[...] = jnp.zeros_like(r_ref)
    acc_ref[...] = jnp.zeros_like(acc_ref)

    def block(j, masked):
        k0 = pl.multiple_of(j * bk, bk)
        k = k_ref[0, pl.ds(k0, bk), :]
        v = v_ref[0, pl.ds(k0, bk), :]
        for h in range(2):
            z = lax.dot_general(qh[h], k, _NT, preferred_element_type=F32)
            l1 = jnp.minimum(-z, 0.0) - jnp.log(1.0 + jnp.exp(-jnp.abs(z)))
            if masked:
                t_idx = qi * bq + lax.broadcasted_iota(jnp.int32, z.shape, 0)
                s_idx = k0 + lax.broadcasted_iota(jnp.int32, z.shape, 1)
                strict = s_idx < t_idx
                l1 = jnp.where(strict, l1, 0.0)
            hi = l1.astype(BF16)
            lo = (l1 - hi.astype(F32)).astype(BF16)
            sfx = (jnp.dot(hi, tri, preferred_element_type=F32)
                   + jnp.dot(lo, tri, preferred_element_type=F32))
            a = jnp.exp(z + l1 + sfx + r_ref[h])
            if masked:
                a = jnp.where(strict, a, 0.0)
            acc_ref[h] = acc_ref[h] + jnp.dot(a.astype(BF16), v, preferred_element_type=F32)
            r_ref[h] = r_ref[h] + (sfx[:, 0:1] + l1[:, 0:1])

    n_diag = bq // bk
    for d in reversed(range(n_diag)):
        block(qi * n_diag + d, True)
    n_full = qi * n_diag
    lax.fori_loop(0, n_full, lambda i, c: (block(n_full - 1 - i, False), c)[1], 0)

    o0 = acc_ref[0]
    o_ref[0] = jnp.where(_head_masks(o0.shape), o0, acc_ref[1]).astype(o_ref.dtype)


def _sb(qkv, tri, bq, bk):
    B, S, _ = qkv.shape
    n_pair = 4
    base = 3 * n_pair
    return pl.pallas_call(
        functools.partial(_sb_kernel, bk=bk),
        grid=(B, n_pair, S // bq),
        in_specs=[
            pl.BlockSpec((1, bq, LANES), lambda b, h, i: (b, i, base + h)),
            pl.BlockSpec((1, S, LANES), lambda b, h, i: (b, 0, base + n_pair + h)),
            pl.BlockSpec((1, S, LANES), lambda b, h, i: (b, 0, base + 2 * n_pair + h)),
            pl.BlockSpec((bk, bk), lambda b, h, i: (0, 0)),
        ],
        out_specs=pl.BlockSpec((1, bq, LANES), lambda b, h, i: (b, i, h)),
        out_shape=jax.ShapeDtypeStruct((B, S, n_pair * LANES), BF16),
        scratch_shapes=[
            pltpu.VMEM((2, bq, 1), F32),
            pltpu.VMEM((2, bq, LANES), F32),
        ],
        compiler_params=_params(("arbitrary", "arbitrary", "arbitrary")),
        name="sb",
    )(qkv, qkv, qkv, tri)


def _pre1_kernel(x_ref, g_ref, w_ref, pos_ref, inv_ref, o_ref, *, n_rot):
    hb = _rms(x_ref[...], g_ref[...]).astype(BF16)
    proj = jnp.dot(hb, w_ref[...], preferred_element_type=F32)
    ang = pos_ref[...].astype(F32) * inv_ref[...]
    lane = lax.broadcasted_iota(jnp.int32, ang.shape, 1)
    first = (lane % HEAD_DIM) < (HEAD_DIM // 2)
    cos = jnp.cos(ang)
    sin = jnp.sin(ang)
    sin = jnp.where(first, -sin, sin)
    half = HEAD_DIM // 2
    for c in range(n_rot // LANES):
        xs = proj[:, c * LANES:(c + 1) * LANES]
        rot = jnp.where(first, pltpu.roll(xs, LANES - half, axis=1), pltpu.roll(xs, half, axis=1))
        o_ref[:, c * LANES:(c + 1) * LANES] = (xs * cos + rot * sin).astype(BF16)
    o_ref[:, n_rot:] = proj[:, n_rot:].astype(BF16)


def _pre1(h, g, w, pos, inv, tm, n_rot):
    T, D = h.shape
    N = w.shape[1]
    return pl.pallas_call(
        functools.partial(_pre1_kernel, n_rot=n_rot),
        grid=(T // tm,),
        in_specs=[
            pl.BlockSpec((tm, D), lambda i: (i, 0)),
            pl.BlockSpec((1, D), lambda i: (0, 0)),
            pl.BlockSpec((D, N), lambda i: (0, 0)),
            pl.BlockSpec((tm, 1), lambda i: (i, 0)),
            pl.BlockSpec((1, LANES), lambda i: (0, 0)),
        ],
        out_specs=pl.BlockSpec((tm, N), lambda i: (i, 0)),
        out_shape=jax.ShapeDtypeStruct((T, N), BF16),
        compiler_params=_params(("arbitrary",)),
        name="pre1",
    )(h, g, w, pos, inv)


def _swa_kernel(sink_ref, q_ref, kp_ref, ko_ref, vp_ref, vo_ref, o_ref, *, n_kv, group):
    i = pl.program_id(1)
    W = q_ref.shape[1]
    qi = lax.broadcasted_iota(jnp.int32, (W, 2 * W), 0)
    kj = lax.broadcasted_iota(jnp.int32, (W, 2 * W), 1)
    rel = qi + W - kj
    valid = (rel >= 0) & (rel < W) & ((kj >= W) | (i > 0))
    valid = jnp.concatenate([valid] * group, axis=0)
    rows = lax.broadcasted_iota(jnp.int32, (group * W, 1), 0)
    for g in range(n_kv):
        sl = slice(g * HEAD_DIM, (g + 1) * HEAD_DIM)
        k = jnp.concatenate([kp_ref[0, :, sl], ko_ref[0, :, sl]], axis=0)
        v = jnp.concatenate([vp_ref[0, :, sl], vo_ref[0, :, sl]], axis=0)
        qs = jnp.concatenate(
            [q_ref[0, :, (g * group + a) * HEAD_DIM:(g * group + a + 1) * HEAD_DIM]
             for a in range(group)], axis=0)
        s = lax.dot_general(qs, k, _NT, preferred_element_type=F32)
        s = jnp.where(valid, s, NEG_INF)
        sink = jnp.zeros((group * W, 1), F32)
        for a in range(group):
            sink = jnp.where(rows // W == a, sink_ref[g * group + a], sink)
        m = jnp.maximum(jnp.max(s, axis=-1, keepdims=True), sink)
        p = jnp.exp(s - m)
        denom = jnp.sum(p, axis=-1, keepdims=True) + jnp.exp(sink - m)
        o = jnp.dot(p.astype(BF16), v, preferred_element_type=F32) / denom
        for a in range(group):
            hq = g * group + a
            o_ref[0, :, hq * HEAD_DIM:(hq + 1) * HEAD_DIM] = o[a * W:(a + 1) * W].astype(o_ref.dtype)


def _swa(qkv, sinks, n_q, n_kv):
    B, S, _ = qkv.shape
    W = WINDOW
    qw, kw = n_q * HEAD_DIM, n_kv * HEAD_DIM
    kcol, vcol = qw // kw, qw // kw + 1
    prev = lambda i: jnp.maximum(i - 1, 0)
    return pl.pallas_call(
        functools.partial(_swa_kernel, n_kv=n_kv, group=n_q // n_kv),
        grid_spec=pltpu.PrefetchScalarGridSpec(
            num_scalar_prefetch=1,
            grid=(B, S // W),
            in_specs=[
                pl.BlockSpec((1, W, qw), lambda b, i, s: (b, i, 0)),
                pl.BlockSpec((1, W, kw), lambda b, i, s: (b, prev(i), kcol)),
                pl.BlockSpec((1, W, kw), lambda b, i, s: (b, i, kcol)),
                pl.BlockSpec((1, W, kw), lambda b, i, s: (b, prev(i), vcol)),
                pl.BlockSpec((1, W, kw), lambda b, i, s: (b, i, vcol)),
            ],
            out_specs=pl.BlockSpec((1, W, qw), lambda b, i, s: (b, i, 0)),
        ),
        out_shape=jax.ShapeDtypeStruct((B, S, qw), BF16),
        compiler_params=_params(("arbitrary", "arbitrary")),
        name="swa",
    )(sinks, qkv, qkv, qkv, qkv, qkv)


def _post_kernel(*refs, n_mix, final_norm):
    h_ref = refs[0]
    mix_refs = refs[1:1 + n_mix]
    (p_ref, wo_ref, gf_ref, wg_ref, wu_ref, wd_ref, gp_ref, wpg_ref, wpp_ref, gfin_ref,
     o_ref) = refs[1 + n_mix:]
    h = h_ref[...]
    off = 0
    for m_ref in mix_refs:
        w = m_ref.shape[1]
        h = h + jnp.dot(m_ref[...], wo_ref[off:off + w, :], preferred_element_type=F32)
        off += w
    hb = _rms(h, gf_ref[...]).astype(BF16)
    g = jnp.dot(hb, wg_ref[...], preferred_element_type=F32)
    u = jnp.dot(hb, wu_ref[...], preferred_element_type=F32)
    act = (g * jax.nn.sigmoid(g) * u).astype(BF16)
    h = h + jnp.dot(act, wd_ref[...], preferred_element_type=F32)
    gate = jax.nn.sigmoid(jnp.dot(_rms(h, gp_ref[...]).astype(BF16), wpg_ref[...],
                                  preferred_element_type=F32))
    h = h + gate * jnp.dot(p_ref[...].astype(BF16), wpp_ref[...], preferred_element_type=F32)
    if final_norm:
        h = _rms(h, gfin_ref[...])
    o_ref[...] = h


def _post(h, mixes, p, wo, gf, wg, wu, wd, gp, wpg, wpp, gfin, tm, final_norm):
    T, D = h.shape
    row = lambda w: pl.BlockSpec((tm, w), lambda i: (i, 0))
    full = lambda a: pl.BlockSpec(a.shape, lambda i: (0, 0))
    consts = (wo, gf, wg, wu, wd, gp, wpg, wpp, gfin)
    return pl.pallas_call(
        functools.partial(_post_kernel, n_mix=len(mixes), final_norm=final_norm),
        grid=(T // tm,),
        in_specs=[row(D)] + [row(m.shape[1]) for m in mixes] + [row(p.shape[1])]
        + [full(c) for c in consts],
        out_specs=row(D),
        out_shape=jax.ShapeDtypeStruct((T, D), F32),
        compiler_params=_params(("arbitrary",)),
        name="post",
    )(h, *mixes, p, *consts)


def kernel(x, p, positions, norm_mix, norm_ffn, norm_ple, norm_final, ev_w_in, ev_b_f, ev_w_out,
           od_w_in, od_sinks, od_w_out, ffn_w_gate, ffn_w_up, ffn_w_down, ple_w_proj, ple_w_gate):
    B, S, D = x.shape
    T = B * S
    n_heads = D // HEAD_DIM
    n_fox = n_heads // 2
    fox_w = n_fox * HEAD_DIM
    n_q, n_kv = n_heads, 4
    scale = HEAD_DIM ** -0.5
    row = lambda a: a.reshape(1, -1)

    w_in = ev_w_in[0]
    col_scale = jnp.ones((6 * fox_w,), F32).at[:fox_w].set(scale).at[3 * fox_w:4 * fox_w].set(scale)
    w_qkv = (w_in[:, :6 * fox_w] * col_scale).astype(BF16)
    w_f = jnp.pad(w_in[:, 6 * fox_w:], ((0, 0), (0, LANES - n_fox))).astype(BF16)
    b_f = jnp.pad(ev_b_f[0], (0, LANES - n_fox)).reshape(1, LANES)
    qkv0, fcol, ft = _pre0(x, row(norm_mix[0]), w_qkv, w_f, b_f, tm=min(512, S))

    bq = bk = min(256, S)
    o_fox = _fox(qkv0, fcol, ft, bq, bk)
    tri = (jnp.arange(bk)[:, None] > jnp.arange(bk)[None, :]).astype(BF16)
    o_sb = _sb(qkv0, tri, bq, bk)

    tm = min(512, T)
    h = _post(x.reshape(T, D), [o_fox.reshape(T, fox_w), o_sb.reshape(T, fox_w)],
              p[0].reshape(T, -1), ev_w_out[0].astype(BF16), row(norm_ffn[0]),
              ffn_w_gate[0].astype(BF16), ffn_w_up[0].astype(BF16), ffn_w_down[0].astype(BF16),
              row(norm_ple[0]), ple_w_gate[0].astype(BF16), ple_w_proj[0].astype(BF16),
              row(norm_final), tm, final_norm=False)

    qw = n_q * HEAD_DIM
    kw = n_kv * HEAD_DIM
    col_scale1 = jnp.ones((qw + 2 * kw,), F32).at[:qw].set(scale)
    w1 = (od_w_in[0] * col_scale1).astype(BF16)
    half = HEAD_DIM // 2
    inv = ROPE_THETA ** (-jnp.arange(half, dtype=F32) / half)
    inv = jnp.tile(inv, LANES // half).reshape(1, LANES)
    qkv1 = _pre1(h, row(norm_mix[1]), w1, positions.reshape(T, 1), inv, tm, n_rot=qw + kw)
    o_swa = _swa(qkv1.reshape(B, S, -1), od_sinks[0], n_q, n_kv)
    out = _post(h, [o_swa.reshape(T, qw)], p[1].reshape(T, -1), od_w_out[0].astype(BF16),
                row(norm_ffn[1]), ffn_w_gate[1].astype(BF16), ffn_w_up[1].astype(BF16),
                ffn_w_down[1].astype(BF16), row(norm_ple[1]), ple_w_gate[1].astype(BF16),
                ple_w_proj[1].astype(BF16), row(norm_final), tm, final_norm=True)
    return out.reshape(B, S, D)
```

```python
import functools

import jax
import jax.numpy as jnp
from jax import lax
from jax.experimental import pallas as pl
from jax.experimental.pallas import tpu as pltpu

F32 = jnp.float32
BF16 = jnp.bfloat16

HEAD_DIM = 64
LANES = 128
BLK = 256
V_ROWS = 80
FEAT_ROWS = 16
WINDOW = 128
ROPE_THETA = 10000.0
EPS = 1e-6
NEG_INF = -1e30
SKIP_LOG = 60.0
VMEM_LIMIT = 56 * 1024 * 1024

_NT = (((1,), (1,)), ((), ()))


def _params(sem):
    return pltpu.CompilerParams(dimension_semantics=sem, vmem_limit_bytes=VMEM_LIMIT)


def _rms(x, g):
    return x * lax.rsqrt(jnp.mean(x * x, axis=-1, keepdims=True) + EPS) * g


def _log_sigmoid(x):
    return jnp.minimum(x, 0.0) - jnp.log(1.0 + jnp.exp(-jnp.abs(x)))


def _split3(x):
    a = x.astype(BF16)
    r = x - a.astype(F32)
    b = r.astype(BF16)
    c = (r - b.astype(F32)).astype(BF16)
    return a, b, c


def _pre0_kernel(x_ref, g_ref, wt_ref, wk_ref, wf_ref, bf_ref, pk_ref, aq_ref,
                 qtf_ref, qts_ref, vtf_ref, vts_ref, kf_ref, ks_ref, feat_ref,
                 gt_ref, qn_ref, fb_ref, kmsq_ref, carry_ref, *, n_fox):
    tm = x_ref.shape[1]
    fw = n_fox * HEAD_DIM

    @pl.when(pl.program_id(1) == 0)
    def _():
        carry_ref[...] = jnp.zeros_like(carry_ref)

    hb = _rms(x_ref[0], g_ref[...]).astype(BF16)

    def tproj(c):
        return lax.dot_general(wt_ref[c * fw:(c + 1) * fw, :], hb, _NT,
                               preferred_element_type=F32).astype(BF16)

    qtf = tproj(0)
    qtf_ref[0] = qtf
    vtf = tproj(1)
    lane_row = lax.broadcasted_iota(jnp.int32, (V_ROWS - HEAD_DIM, tm), 0)
    ones_pad = jnp.where(lane_row == 0, 1.0, 0.0).astype(BF16)
    for h in range(n_fox):
        vtf_ref[0, h, 0, 0:HEAD_DIM, :] = vtf[h * HEAD_DIM:(h + 1) * HEAD_DIM, :]
        vtf_ref[0, h, 0, HEAD_DIM:V_ROWS, :] = ones_pad
    qts_ref[0] = tproj(2)
    vts = tproj(3)
    for h in range(n_fox):
        vts_ref[0, h, 0] = vts[h * HEAD_DIM:(h + 1) * HEAD_DIM, :]

    q32 = qtf.astype(F32)
    qn_rows = [jnp.sqrt(jnp.sum(jnp.square(q32[h * HEAD_DIM:(h + 1) * HEAD_DIM, :]),
                                axis=0, keepdims=True)) for h in range(n_fox)]
    qn_ref[0] = jnp.concatenate(qn_rows, axis=0)

    kk = jnp.dot(hb, wk_ref[...], preferred_element_type=F32)
    nk = n_fox * LANES
    ks_ref[0] = kk[:, nk:].astype(BF16)
    kfox = kk[:, :nk].astype(BF16)
    k32 = kfox.astype(F32)
    lane = lax.broadcasted_iota(jnp.int32, (1, LANES), 1)
    kmsq = jnp.zeros((1, LANES), F32)
    for h in range(n_fox):
        ss = jnp.sum(jnp.square(k32[:, h * LANES:(h + 1) * LANES]), axis=-1, keepdims=True)
        kmsq = jnp.where(lane == h, jnp.max(ss, axis=0, keepdims=True), kmsq)
    kmsq_ref[0, 0] = kmsq

    gate = jnp.dot(hb, wf_ref[...], preferred_element_type=F32) + bf_ref[...]
    lf = _log_sigmoid(gate)
    row = lax.broadcasted_iota(jnp.int32, (tm, tm), 0)
    col = lax.broadcasted_iota(jnp.int32, (tm, tm), 1)
    tri = jnp.where(row >= col, 1.0, 0.0).astype(BF16)
    G = jnp.zeros((tm, LANES), F32)
    for piece in _split3(lf):
        G = G + jnp.dot(tri, piece, preferred_element_type=F32)
    fb_ref[0, 0] = carry_ref[...]
    carry_ref[...] = carry_ref[...] + G[tm - 1:tm, :]

    gp = jnp.concatenate(_split3(G), axis=1)
    kfeat = jnp.dot(gp, pk_ref[...], preferred_element_type=F32)
    lane_k = lax.broadcasted_iota(jnp.int32, (1, nk), 1) % LANES
    kones = jnp.where((lane_k >= HEAD_DIM) & (lane_k < HEAD_DIM + 3), 1.0, 0.0)
    kf_ref[0] = (k32 + kfeat + kones).astype(BF16)

    GT = G.T
    gt_ref[0] = GT[0:8, :]
    gpt = jnp.concatenate(_split3(GT), axis=0)
    qfeat = jnp.dot(aq_ref[...], gpt, preferred_element_type=F32)
    frow = lax.broadcasted_iota(jnp.int32, (n_fox * FEAT_ROWS, 1), 0) % FEAT_ROWS
    qones = jnp.where((frow >= 3) & (frow < 6), 1.0, 0.0)
    feat_ref[0] = (qfeat + qones).astype(BF16)


def _pre0(x, g, wt, wk, wf, bf, pk, aq, n_fox):
    B, S, D = x.shape
    tm = BLK
    nb = S // tm
    fw = n_fox * HEAD_DIM
    const = lambda a: pl.BlockSpec(a.shape, lambda b, s: (0,) * a.ndim)
    tok_lane = lambda rows: pl.BlockSpec((1, rows, tm), lambda b, s: (b, 0, s))
    return pl.pallas_call(
        functools.partial(_pre0_kernel, n_fox=n_fox),
        grid=(B, nb),
        in_specs=[pl.BlockSpec((1, tm, D), lambda b, s: (b, s, 0))]
        + [const(a) for a in (g, wt, wk, wf, bf, pk, aq)],
        out_specs=[
            tok_lane(fw),
            tok_lane(fw),
            pl.BlockSpec((1, n_fox, 1, V_ROWS, tm), lambda b, s: (b, 0, s, 0, 0)),
            pl.BlockSpec((1, n_fox, 1, HEAD_DIM, tm), lambda b, s: (b, 0, s, 0, 0)),
            pl.BlockSpec((1, tm, n_fox * LANES), lambda b, s: (b, s, 0)),
            pl.BlockSpec((1, tm, fw), lambda b, s: (b, s, 0)),
            tok_lane(n_fox * FEAT_ROWS),
            tok_lane(8),
            tok_lane(8),
            pl.BlockSpec((1, 1, 1, LANES), lambda b, s: (b, s, 0, 0)),
            pl.BlockSpec((1, 1, 1, LANES), lambda b, s: (b, s, 0, 0)),
        ],
        out_shape=[
            jax.ShapeDtypeStruct((B, fw, S), BF16),
            jax.ShapeDtypeStruct((B, fw, S), BF16),
            jax.ShapeDtypeStruct((B, n_fox, nb, V_ROWS, tm), BF16),
            jax.ShapeDtypeStruct((B, n_fox, nb, HEAD_DIM, tm), BF16),
            jax.ShapeDtypeStruct((B, S, n_fox * LANES), BF16),
            jax.ShapeDtypeStruct((B, S, fw), BF16),
            jax.ShapeDtypeStruct((B, n_fox * FEAT_ROWS, S), BF16),
            jax.ShapeDtypeStruct((B, 8, S), F32),
            jax.ShapeDtypeStruct((B, 8, S), F32),
            jax.ShapeDtypeStruct((B, nb, 1, LANES), F32),
            jax.ShapeDtypeStruct((B, nb, 1, LANES), F32),
        ],
        scratch_shapes=[pltpu.VMEM((1, LANES), F32)],
        compiler_params=_params(("arbitrary", "arbitrary")),
        name="pre0",
    )(x, g, wt, wk, wf, bf, pk, aq)


def _fox_kernel(fb_ref, qt_ref, feat_ref, k_ref, vt_ref, gt_ref, qn_ref, kmsq_ref,
                o_ref, m_ref, acc_ref):
    b = pl.program_id(0)
    hp = pl.program_id(1)
    qi = pl.program_id(2)
    bq = qt_ref.shape[2]
    row = lax.broadcasted_iota(jnp.int32, (BLK, bq), 0)
    col = lax.broadcasted_iota(jnp.int32, (BLK, bq), 1)
    causal = row <= col
    kmax_sq = jnp.max(kmsq_ref[0], axis=0)
    lane = lax.broadcasted_iota(jnp.int32, (1, LANES), 1)

    pad = jnp.zeros((LANES - HEAD_DIM - FEAT_ROWS, bq), BF16)
    qaug = [jnp.concatenate([qt_ref[0, h * HEAD_DIM:(h + 1) * HEAD_DIM, :],
                             feat_ref[0, h * FEAT_ROWS:(h + 1) * FEAT_ROWS, :], pad], axis=0)
            for h in range(2)]

    m_ref[...] = jnp.full_like(m_ref, NEG_INF)
    acc_ref[...] = jnp.zeros_like(acc_ref)

    def block(j, diag):
        k0 = pl.multiple_of(j * BLK, BLK)
        for h in range(2):
            hg = 2 * hp + h
            kb = k_ref[0, pl.ds(k0, BLK), h * LANES:(h + 1) * LANES]
            st = jnp.dot(kb, qaug[h], preferred_element_type=F32)
            if diag:
                st = jnp.where(causal, st, NEG_INF)
                c = 0.0
            else:
                c = fb_ref[b, hg, qi] - fb_ref[b, hg, j]
            m_old = m_ref[h]
            m_new = jnp.maximum(m_old, jnp.max(st, axis=0, keepdims=True) + c)
            p = jnp.exp(st - (m_new - c))
            alpha = jnp.exp(m_old - m_new)
            acc_ref[h] = alpha * acc_ref[h] + jnp.dot(vt_ref[0, h, j], p.astype(BF16),
                                                      preferred_element_type=F32)
            m_ref[h] = m_new

    block(qi, True)

    th = []
    for h in range(2):
        hg = 2 * hp + h
        kmax = jnp.sqrt(jnp.sum(jnp.where(lane == hg, kmax_sq, 0.0), axis=-1, keepdims=True))
        bound = qn_ref[0, pl.ds(hg, 1), :] * kmax + gt_ref[0, pl.ds(hg, 1), :] - m_ref[h]
        th.append(jnp.max(bound) + SKIP_LOG)

    def needed(j):
        jn = jnp.maximum(j, 0) + 1
        gap0 = fb_ref[b, 2 * hp, jn] - fb_ref[b, 2 * hp, qi]
        gap1 = fb_ref[b, 2 * hp + 1, jn] - fb_ref[b, 2 * hp + 1, qi]
        return (j >= 0) & ((gap0 <= th[0]) | (gap1 <= th[1]))

    def body(j):
        block(j, False)
        return j - 1

    lax.while_loop(needed, body, qi - 1)

    out_t = jnp.concatenate(
        [acc_ref[h, 0:HEAD_DIM, :] / acc_ref[h, HEAD_DIM:HEAD_DIM + 1, :] for h in range(2)], axis=0)
    o_ref[0] = out_t.T.astype(o_ref.dtype)


def _fox(fb, qt, feat, kf, vt, gt, qn, kmsq):
    B, _, S = qt.shape
    n_pair = qt.shape[1] // LANES
    nb = S // BLK
    return pl.pallas_call(
        _fox_kernel,
        grid_spec=pltpu.PrefetchScalarGridSpec(
            num_scalar_prefetch=1,
            grid=(B, n_pair, nb),
            in_specs=[
                pl.BlockSpec((1, LANES, BLK), lambda b, h, i, s: (b, h, i)),
                pl.BlockSpec((1, 2 * FEAT_ROWS, BLK), lambda b, h, i, s: (b, h, i)),
                pl.BlockSpec((1, S, 2 * LANES), lambda b, h, i, s: (b, 0, h)),
                pl.BlockSpec((1, 2, nb, V_ROWS, BLK), lambda b, h, i, s: (b, h, 0, 0, 0)),
                pl.BlockSpec((1, 8, BLK), lambda b, h, i, s: (b, 0, i)),
                pl.BlockSpec((1, 8, BLK), lambda b, h, i, s: (b, 0, i)),
                pl.BlockSpec((1, nb, 1, LANES), lambda b, h, i, s: (b, 0, 0, 0)),
            ],
            out_specs=pl.BlockSpec((1, BLK, LANES), lambda b, h, i, s: (b, i, h)),
            scratch_shapes=[
                pltpu.VMEM((2, 1, BLK), F32),
                pltpu.VMEM((2, V_ROWS, BLK), F32),
            ],
        ),
        out_shape=jax.ShapeDtypeStruct((B, S, n_pair * LANES), BF16),
        compiler_params=_params(("arbitrary", "arbitrary", "arbitrary")),
        name="fox",
    )(fb, qt, feat, kf, vt, gt, qn, kmsq)


def _sb_kernel(qt_ref, k_ref, vt_ref, tri_ref, o_ref, r_ref, acc_ref):
    qi = pl.program_id(2)
    bq = qt_ref.shape[2]
    row = lax.broadcasted_iota(jnp.int32, (BLK, bq), 0)
    col = lax.broadcasted_iota(jnp.int32, (BLK, bq), 1)
    strict = row < col
    qt = qt_ref[0]
    top = lax.broadcasted_iota(jnp.int32, qt.shape, 0) < HEAD_DIM
    zero = jnp.zeros_like(qt)
    qh = (jnp.where(top, qt, zero), jnp.where(top, zero, qt))
    tri = tri_ref[...]

    r_ref[...] = jnp.zeros_like(r_ref)
    acc_ref[...] = jnp.zeros_like(acc_ref)

    def block(j, diag):
        k0 = pl.multiple_of(j * BLK, BLK)
        kb = k_ref[0, pl.ds(k0, BLK), :]
        for h in range(2):
            z = jnp.dot(kb, qh[h], preferred_element_type=F32)
            l1 = jnp.minimum(-z, 0.0) - jnp.log(1.0 + jnp.exp(-jnp.abs(z)))
            if diag:
                l1 = jnp.where(strict, l1, 0.0)
            hi = l1.astype(BF16)
            lo = (l1 - hi.astype(F32)).astype(BF16)
            sfx = (jnp.dot(tri, hi, preferred_element_type=F32)
                   + jnp.dot(tri, lo, preferred_element_type=F32))
            r_old = r_ref[h]
            a = jnp.exp(z + l1 + sfx + r_old)
            if diag:
                a = jnp.where(strict, a, 0.0)
            acc_ref[h] = acc_ref[h] + jnp.dot(vt_ref[0, h, j], a.astype(BF16),
                                              preferred_element_type=F32)
            r_ref[h] = r_old + (sfx[0:1, :] + l1[0:1, :])
        return jnp.maximum(jnp.max(r_ref[0]), jnp.max(r_ref[1]))

    rmax = block(qi, True)

    def body(carry):
        j, _ = carry
        return j - 1, block(j, False)

    lax.while_loop(lambda c: (c[0] >= 0) & (c[1] > -SKIP_LOG), body, (qi - 1, rmax))

    o_ref[0] = jnp.concatenate([acc_ref[0], acc_ref[1]], axis=0).T.astype(o_ref.dtype)


def _sb(qt, ks, vt, tri):
    B, _, S = qt.shape
    n_pair = qt.shape[1] // LANES
    nb = S // BLK
    return pl.pallas_call(
        _sb_kernel,
        grid=(B, n_pair, nb),
        in_specs=[
            pl.BlockSpec((1, LANES, BLK), lambda b, h, i: (b, h, i)),
            pl.BlockSpec((1, S, LANES), lambda b, h, i: (b, 0, h)),
            pl.BlockSpec((1, 2, nb, HEAD_DIM, BLK), lambda b, h, i: (b, h, 0, 0, 0)),
            pl.BlockSpec((BLK, BLK), lambda b, h, i: (0, 0)),
        ],
        out_specs=pl.BlockSpec((1, BLK, LANES), lambda b, h, i: (b, i, h)),
        out_shape=jax.ShapeDtypeStruct((B, S, n_pair * LANES), BF16),
        scratch_shapes=[
            pltpu.VMEM((2, 1, BLK), F32),
            pltpu.VMEM((2, HEAD_DIM, BLK), F32),
        ],
        compiler_params=_params(("arbitrary", "arbitrary", "arbitrary")),
        name="sb",
    )(qt, ks, vt, tri)


def _pre1_kernel(x_ref, g_ref, w_ref, pos_ref, inv_ref, o_ref, *, n_rot):
    hb = _rms(x_ref[...], g_ref[...]).astype(BF16)
    proj = jnp.dot(hb, w_ref[...], preferred_element_type=F32)
    ang = pos_ref[...].astype(F32) * inv_ref[...]
    lane = lax.broadcasted_iota(jnp.int32, ang.shape, 1)
    first = (lane % HEAD_DIM) < (HEAD_DIM // 2)
    cos = jnp.cos(ang)
    sin = jnp.sin(ang)
    sin = jnp.where(first, -sin, sin)
    half = HEAD_DIM // 2
    for c in range(n_rot // LANES):
        xs = proj[:, c * LANES:(c + 1) * LANES]
        rot = jnp.where(first, pltpu.roll(xs, LANES - half, axis=1), pltpu.roll(xs, half, axis=1))
        o_ref[:, c * LANES:(c + 1) * LANES] = (xs * cos + rot * sin).astype(BF16)
    o_ref[:, n_rot:] = proj[:, n_rot:].astype(BF16)


def _pre1(h, g, w, pos, inv, tm, n_rot):
    T, D = h.shape
    N = w.shape[1]
    return pl.pallas_call(
        functools.partial(_pre1_kernel, n_rot=n_rot),
        grid=(T // tm,),
        in_specs=[
            pl.BlockSpec((tm, D), lambda i: (i, 0)),
            pl.BlockSpec((1, D), lambda i: (0, 0)),
            pl.BlockSpec((D, N), lambda i: (0, 0)),
            pl.BlockSpec((tm, 1), lambda i: (i, 0)),
            pl.BlockSpec((1, LANES), lambda i: (0, 0)),
        ],
        out_specs=pl.BlockSpec((tm, N), lambda i: (i, 0)),
        out_shape=jax.ShapeDtypeStruct((T, N), BF16),
        compiler_params=_params(("arbitrary",)),
        name="pre1",
    )(h, g, w, pos, inv)


def _swa_kernel(sink_ref, q_ref, kp_ref, ko_ref, vp_ref, vo_ref, o_ref, *, n_kv, group):
    i = pl.program_id(1)
    W = q_ref.shape[1]
    qi = lax.broadcasted_iota(jnp.int32, (W, 2 * W), 0)
    kj = lax.broadcasted_iota(jnp.int32, (W, 2 * W), 1)
    rel = qi + W - kj
    valid = (rel >= 0) & (rel < W) & ((kj >= W) | (i > 0))
    valid = jnp.concatenate([valid] * group, axis=0)
    rows = lax.broadcasted_iota(jnp.int32, (group * W, 1), 0)
    for g in range(n_kv):
        sl = slice(g * HEAD_DIM, (g + 1) * HEAD_DIM)
        k = jnp.concatenate([kp_ref[0, :, sl], ko_ref[0, :, sl]], axis=0)
        v = jnp.concatenate([vp_ref[0, :, sl], vo_ref[0, :, sl]], axis=0)
        qs = jnp.concatenate(
            [q_ref[0, :, (g * group + a) * HEAD_DIM:(g * group + a + 1) * HEAD_DIM]
             for a in range(group)], axis=0)
        s = lax.dot_general(qs, k, _NT, preferred_element_type=F32)
        s = jnp.where(valid, s, NEG_INF)
        sink = jnp.zeros((group * W, 1), F32)
        for a in range(group):
            sink = jnp.where(rows // W == a, sink_ref[g * group + a], sink)
        m = jnp.maximum(jnp.max(s, axis=-1, keepdims=True), sink)
        p = jnp.exp(s - m)
        denom = jnp.sum(p, axis=-1, keepdims=True) + jnp.exp(sink - m)
        o = jnp.dot(p.astype(BF16), v, preferred_element_type=F32) / denom
        for a in range(group):
            hq = g * group + a
            o_ref[0, :, hq * HEAD_DIM:(hq + 1) * HEAD_DIM] = o[a * W:(a + 1) * W].astype(o_ref.dtype)


def _swa(qkv, sinks, n_q, n_kv):
    B, S, _ = qkv.shape
    W = WINDOW
    qw, kw = n_q * HEAD_DIM, n_kv * HEAD_DIM
    kcol, vcol = qw // kw, qw // kw + 1
    prev = lambda i: jnp.maximum(i - 1, 0)
    return pl.pallas_call(
        functools.partial(_swa_kernel, n_kv=n_kv, group=n_q // n_kv),
        grid_spec=pltpu.PrefetchScalarGridSpec(
            num_scalar_prefetch=1,
            grid=(B, S // W),
            in_specs=[
                pl.BlockSpec((1, W, qw), lambda b, i, s: (b, i, 0)),
                pl.BlockSpec((1, W, kw), lambda b, i, s: (b, prev(i), kcol)),
                pl.BlockSpec((1, W, kw), lambda b, i, s: (b, i, kcol)),
                pl.BlockSpec((1, W, kw), lambda b, i, s: (b, prev(i), vcol)),
                pl.BlockSpec((1, W, kw), lambda b, i, s: (b, i, vcol)),
            ],
            out_specs=pl.BlockSpec((1, W, qw), lambda b, i, s: (b, i, 0)),
        ),
        out_shape=jax.ShapeDtypeStruct((B, S, qw), BF16),
        compiler_params=_params(("arbitrary", "arbitrary")),
        name="swa",
    )(sinks, qkv, qkv, qkv, qkv, qkv)


def _post_kernel(*refs, n_mix, final_norm):
    h_ref = refs[0]
    mix_refs = refs[1:1 + n_mix]
    (p_ref, wo_ref, gf_ref, wg_ref, wu_ref, wd_ref, gp_ref, wpg_ref, wpp_ref, gfin_ref,
     o_ref) = refs[1 + n_mix:]
    h = h_ref[...]
    off = 0
    for m_ref in mix_refs:
        w = m_ref.shape[1]
        h = h + jnp.dot(m_ref[...], wo_ref[off:off + w, :], preferred_element_type=F32)
        off += w
    hb = _rms(h, gf_ref[...]).astype(BF16)
    g = jnp.dot(hb, wg_ref[...], preferred_element_type=F32)
    u = jnp.dot(hb, wu_ref[...], preferred_element_type=F32)
    act = (g * jax.nn.sigmoid(g) * u).astype(BF16)
    h = h + jnp.dot(act, wd_ref[...], preferred_element_type=F32)
    gate = jax.nn.sigmoid(jnp.dot(_rms(h, gp_ref[...]).astype(BF16), wpg_ref[...],
                                  preferred_element_type=F32))
    h = h + gate * jnp.dot(p_ref[...].astype(BF16), wpp_ref[...], preferred_element_type=F32)
    if final_norm:
        h = _rms(h, gfin_ref[...])
    o_ref[...] = h


def _post(h, mixes, p, wo, gf, wg, wu, wd, gp, wpg, wpp, gfin, tm, final_norm):
    T, D = h.shape
    row = lambda w: pl.BlockSpec((tm, w), lambda i: (i, 0))
    full = lambda a: pl.BlockSpec(a.shape, lambda i: (0, 0))
    consts = (wo, gf, wg, wu, wd, gp, wpg, wpp, gfin)
    return pl.pallas_call(
        functools.partial(_post_kernel, n_mix=len(mixes), final_norm=final_norm),
        grid=(T // tm,),
        in_specs=[row(D)] + [row(m.shape[1]) for m in mixes] + [row(p.shape[1])]
        + [full(c) for c in consts],
        out_specs=row(D),
        out_shape=jax.ShapeDtypeStruct((T, D), F32),
        compiler_params=_params(("arbitrary",)),
        name="post",
    )(h, *mixes, p, *consts)


def _layer0_weights(w_in, b_f, n_fox):
    fw = n_fox * HEAD_DIM
    scale = HEAD_DIM ** -0.5
    D = w_in.shape[0]
    qa, ka, va, qs, ks, vs = (w_in[:, i * fw:(i + 1) * fw] for i in range(6))
    wt = jnp.concatenate([qa * scale, va, qs * scale, vs], axis=1).T.astype(BF16)
    ka_pad = jnp.pad(ka.reshape(D, n_fox, HEAD_DIM), ((0, 0), (0, 0), (0, LANES - HEAD_DIM)))
    wk = jnp.concatenate([ka_pad.reshape(D, n_fox * LANES), ks], axis=1).astype(BF16)
    wf = jnp.pad(w_in[:, 6 * fw:], ((0, 0), (0, LANES - n_fox))).astype(BF16)
    bf = jnp.pad(b_f, (0, LANES - n_fox)).reshape(1, LANES)
    heads = jnp.arange(n_fox)
    pk = jnp.zeros((3 * LANES, n_fox * LANES), F32)
    aq = jnp.zeros((n_fox * FEAT_ROWS, 3 * LANES), F32)
    for piece in range(3):
        pk = pk.at[piece * LANES + heads, heads * LANES + HEAD_DIM + 3 + piece].set(-1.0)
        aq = aq.at[heads * FEAT_ROWS + piece, piece * LANES + heads].set(1.0)
    return wt, wk, wf, bf, pk.astype(BF16), aq.astype(BF16)


def kernel(x, p, positions, norm_mix, norm_ffn, norm_ple, norm_final, ev_w_in, ev_b_f, ev_w_out,
           od_w_in, od_sinks, od_w_out, ffn_w_gate, ffn_w_up, ffn_w_down, ple_w_proj, ple_w_gate):
    B, S, D = x.shape
    T = B * S
    n_heads = D // HEAD_DIM
    n_fox = n_heads // 2
    fox_w = n_fox * HEAD_DIM
    n_q, n_kv = n_heads, 4
    assert S % BLK == 0 and n_fox == 8
    row = lambda a: a.reshape(1, -1)

    wt, wk, wf, bf, pk, aq = _layer0_weights(ev_w_in[0], ev_b_f[0], n_fox)
    (qtf, qts, vtf, vts, kf, ks, feat, gt, qn, fb, kmsq) = _pre0(
        x, row(norm_mix[0]), wt, wk, wf, bf, pk, aq, n_fox)
    fb_heads = fb[:, :, 0, :n_fox].transpose(0, 2, 1)
    o_fox = _fox(fb_heads, qtf, feat, kf, vtf, gt, qn, kmsq)
    tri = (jnp.arange(BLK)[None, :] > jnp.arange(BLK)[:, None]).astype(BF16)
    o_sb = _sb(qts, ks, vts, tri)

    tm = min(512, T)
    h = _post(x.reshape(T, D), [o_fox.reshape(T, fox_w), o_sb.reshape(T, fox_w)],
              p[0].reshape(T, -1), ev_w_out[0].astype(BF16), row(norm_ffn[0]),
              ffn_w_gate[0].astype(BF16), ffn_w_up[0].astype(BF16), ffn_w_down[0].astype(BF16),
              row(norm_ple[0]), ple_w_gate[0].astype(BF16), ple_w_proj[0].astype(BF16),
              row(norm_final), tm, final_norm=False)

    qw = n_q * HEAD_DIM
    kw = n_kv * HEAD_DIM
    col_scale1 = jnp.ones((qw + 2 * kw,), F32).at[:qw].set(HEAD_DIM ** -0.5)
    w1 = (od_w_in[0] * col_scale1).astype(BF16)
    half = HEAD_DIM // 2
    inv = ROPE_THETA ** (-jnp.arange(half, dtype=F32) / half)
    inv = jnp.tile(inv, LANES // half).reshape(1, LANES)
    qkv1 = _pre1(h, row(norm_mix[1]), w1, positions.reshape(T, 1), inv, tm, n_rot=qw + kw)
    o_swa = _swa(qkv1.reshape(B, S, -1), od_sinks[0], n_q, n_kv)
    out = _post(h, [o_swa.reshape(T, qw)], p[1].reshape(T, -1), od_w_out[0].astype(BF16),
                row(norm_ffn[1]), ffn_w_gate[1].astype(BF16), ffn_w_up[1].astype(BF16),
                ffn_w_down[1].astype(BF16), row(norm_ple[1]), ple_w_gate[1].astype(BF16),
                ple_w_proj[1].astype(BF16), row(norm_final), tm, final_norm=True)
    return out.reshape(B, S, D)
```

```python
import functools

import jax
import jax.numpy as jnp
from jax import lax
from jax.experimental import pallas as pl
from jax.experimental.pallas import tpu as pltpu

F32 = jnp.float32
BF16 = jnp.bfloat16

HEAD_DIM = 64
LANES = 128
BLK = 256
V_ROWS = 80
FEAT_ROWS = 16
WINDOW = 128
ROPE_THETA = 10000.0
EPS = 1e-6
NEG_INF = -1e30
SKIP_LOG = 60.0
VMEM_LIMIT = 56 * 1024 * 1024

_NT = (((1,), (1,)), ((), ()))


def _params(sem):
    return pltpu.CompilerParams(dimension_semantics=sem, vmem_limit_bytes=VMEM_LIMIT)


def _rms(x, g):
    return x * lax.rsqrt(jnp.mean(x * x, axis=-1, keepdims=True) + EPS) * g


def _log_sigmoid(x):
    return jnp.minimum(x, 0.0) - jnp.log(1.0 + jnp.exp(-jnp.abs(x)))


def _split3(x):
    a = x.astype(BF16)
    r = x - a.astype(F32)
    b = r.astype(BF16)
    c = (r - b.astype(F32)).astype(BF16)
    return a, b, c


def _pre0_kernel(x_ref, g_ref, wt_ref, wk_ref, wf_ref, bf_ref, pk_ref, aq_ref,
                 qtf_ref, qts_ref, vtf_ref, vts_ref, kf_ref, ks_ref, feat_ref,
                 gt_ref, qn_ref, fb_ref, kmsq_ref, carry_ref, *, n_fox):
    tm = x_ref.shape[1]
    fw = n_fox * HEAD_DIM

    @pl.when(pl.program_id(1) == 0)
    def _():
        carry_ref[...] = jnp.zeros_like(carry_ref)

    hb = _rms(x_ref[0], g_ref[...]).astype(BF16)

    def tproj(c):
        return lax.dot_general(wt_ref[c * fw:(c + 1) * fw, :], hb, _NT,
                               preferred_element_type=F32).astype(BF16)

    qtf = tproj(0)
    qtf_ref[0] = qtf
    vtf = tproj(1)
    lane_row = lax.broadcasted_iota(jnp.int32, (V_ROWS - HEAD_DIM, tm), 0)
    ones_pad = jnp.where(lane_row == 0, 1.0, 0.0).astype(BF16)
    for h in range(n_fox):
        vtf_ref[0, h, 0, 0:HEAD_DIM, :] = vtf[h * HEAD_DIM:(h + 1) * HEAD_DIM, :]
        vtf_ref[0, h, 0, HEAD_DIM:V_ROWS, :] = ones_pad
    qts_ref[0] = tproj(2)
    vts = tproj(3)
    for h in range(n_fox):
        vts_ref[0, h, 0] = vts[h * HEAD_DIM:(h + 1) * HEAD_DIM, :]

    q32 = qtf.astype(F32)
    qn_rows = [jnp.sqrt(jnp.sum(jnp.square(q32[h * HEAD_DIM:(h + 1) * HEAD_DIM, :]),
                                axis=0, keepdims=True)) for h in range(n_fox)]
    qn_ref[0] = jnp.concatenate(qn_rows, axis=0)

    kk = jnp.dot(hb, wk_ref[...], preferred_element_type=F32)
    nk = n_fox * LANES
    ks_ref[0] = kk[:, nk:].astype(BF16)
    kfox = kk[:, :nk].astype(BF16)
    k32 = kfox.astype(F32)
    lane = lax.broadcasted_iota(jnp.int32, (1, LANES), 1)
    kmsq = jnp.zeros((1, LANES), F32)
    for h in range(n_fox):
        ss = jnp.sum(jnp.square(k32[:, h * LANES:(h + 1) * LANES]), axis=-1, keepdims=True)
        kmsq = jnp.where(lane == h, jnp.max(ss, axis=0, keepdims=True), kmsq)
    kmsq_ref[0, 0] = kmsq

    gate = jnp.dot(hb, wf_ref[...], preferred_element_type=F32) + bf_ref[...]
    lf = _log_sigmoid(gate)
    row = lax.broadcasted_iota(jnp.int32, (tm, tm), 0)
    col = lax.broadcasted_iota(jnp.int32, (tm, tm), 1)
    tri = jnp.where(row >= col, 1.0, 0.0).astype(BF16)
    G = jnp.zeros((tm, LANES), F32)
    for piece in _split3(lf):
        G = G + jnp.dot(tri, piece, preferred_element_type=F32)
    fb_ref[0, 0] = carry_ref[...]
    carry_ref[...] = carry_ref[...] + G[tm - 1:tm, :]

    gp = jnp.concatenate(_split3(G), axis=1)
    kfeat = jnp.dot(gp, pk_ref[...], preferred_element_type=F32)
    lane_k = lax.broadcasted_iota(jnp.int32, (1, nk), 1) % LANES
    kones = jnp.where((lane_k >= HEAD_DIM) & (lane_k < HEAD_DIM + 3), 1.0, 0.0)
    kf_ref[0] = (k32 + kfeat + kones).astype(BF16)

    GT = G.T
    gt_ref[0] = GT[0:8, :]
    gpt = jnp.concatenate(_split3(GT), axis=0)
    qfeat = jnp.dot(aq_ref[...], gpt, preferred_element_type=F32)
    frow = lax.broadcasted_iota(jnp.int32, (n_fox * FEAT_ROWS, 1), 0) % FEAT_ROWS
    qones = jnp.where((frow >= 3) & (frow < 6), 1.0, 0.0)
    feat_ref[0] = (qfeat + qones).astype(BF16)


def _pre0(x, g, wt, wk, wf, bf, pk, aq, n_fox):
    B, S, D = x.shape
    tm = BLK
    nb = S // tm
    fw = n_fox * HEAD_DIM
    const = lambda a: pl.BlockSpec(a.shape, lambda b, s: (0,) * a.ndim)
    tok_lane = lambda rows: pl.BlockSpec((1, rows, tm), lambda b, s: (b, 0, s))
    return pl.pallas_call(
        functools.partial(_pre0_kernel, n_fox=n_fox),
        grid=(B, nb),
        in_specs=[pl.BlockSpec((1, tm, D), lambda b, s: (b, s, 0))]
        + [const(a) for a in (g, wt, wk, wf, bf, pk, aq)],
        out_specs=[
            tok_lane(fw),
            tok_lane(fw),
            pl.BlockSpec((1, n_fox, 1, V_ROWS, tm), lambda b, s: (b, 0, s, 0, 0)),
            pl.BlockSpec((1, n_fox, 1, HEAD_DIM, tm), lambda b, s: (b, 0, s, 0, 0)),
            pl.BlockSpec((1, tm, n_fox * LANES), lambda b, s: (b, s, 0)),
            pl.BlockSpec((1, tm, fw), lambda b, s: (b, s, 0)),
            tok_lane(n_fox * FEAT_ROWS),
            tok_lane(8),
            tok_lane(8),
            pl.BlockSpec((1, 1, 1, LANES), lambda b, s: (b, s, 0, 0)),
            pl.BlockSpec((1, 1, 1, LANES), lambda b, s: (b, s, 0, 0)),
        ],
        out_shape=[
            jax.ShapeDtypeStruct((B, fw, S), BF16),
            jax.ShapeDtypeStruct((B, fw, S), BF16),
            jax.ShapeDtypeStruct((B, n_fox, nb, V_ROWS, tm), BF16),
            jax.ShapeDtypeStruct((B, n_fox, nb, HEAD_DIM, tm), BF16),
            jax.ShapeDtypeStruct((B, S, n_fox * LANES), BF16),
            jax.ShapeDtypeStruct((B, S, fw), BF16),
            jax.ShapeDtypeStruct((B, n_fox * FEAT_ROWS, S), BF16),
            jax.ShapeDtypeStruct((B, 8, S), F32),
            jax.ShapeDtypeStruct((B, 8, S), F32),
            jax.ShapeDtypeStruct((B, nb, 1, LANES), F32),
            jax.ShapeDtypeStruct((B, nb, 1, LANES), F32),
        ],
        scratch_shapes=[pltpu.VMEM((1, LANES), F32)],
        compiler_params=_params(("arbitrary", "arbitrary")),
        name="pre0",
    )(x, g, wt, wk, wf, bf, pk, aq)


def _fox_kernel(fb_ref, qt_ref, feat_ref, k_ref, vt_ref, gt_ref, qn_ref, kmsq_ref,
                o_ref, m_ref, acc_ref):
    b = pl.program_id(0)
    hp = pl.program_id(1)
    qi = pl.program_id(2)
    bq = qt_ref.shape[2]
    row = lax.broadcasted_iota(jnp.int32, (BLK, bq), 0)
    col = lax.broadcasted_iota(jnp.int32, (BLK, bq), 1)
    causal = row <= col
    kmax_sq = jnp.max(kmsq_ref[0], axis=0)
    lane = lax.broadcasted_iota(jnp.int32, (1, LANES), 1)

    pad = jnp.zeros((LANES - HEAD_DIM - FEAT_ROWS, bq), BF16)
    qaug = [jnp.concatenate([qt_ref[0, h * HEAD_DIM:(h + 1) * HEAD_DIM, :],
                             feat_ref[0, h * FEAT_ROWS:(h + 1) * FEAT_ROWS, :], pad], axis=0)
            for h in range(2)]

    jp = jnp.maximum(qi - 1, 0)
    kp0 = pl.multiple_of(jp * BLK, BLK)
    kd0 = pl.multiple_of(qi * BLK, BLK)
    heads = range(2)
    st = [jnp.dot(jnp.concatenate([k_ref[0, pl.ds(kp0, BLK), h * LANES:(h + 1) * LANES],
                                   k_ref[0, pl.ds(kd0, BLK), h * LANES:(h + 1) * LANES]], axis=0),
                  qaug[h], preferred_element_type=F32) for h in heads]
    p_first = []
    for h in heads:
        hg = 2 * hp + h
        cp = jnp.where(qi > 0, fb_ref[b, hg, qi] - fb_ref[b, hg, jp], NEG_INF)
        st_p = st[h][0:BLK]
        st_d = jnp.where(causal, st[h][BLK:2 * BLK], NEG_INF)
        m = jnp.maximum(jnp.max(st_d, axis=0, keepdims=True),
                        jnp.max(st_p, axis=0, keepdims=True) + cp)
        m_ref[h] = m
        p_first.append(jnp.concatenate([jnp.exp(st_p - (m - cp)), jnp.exp(st_d - m)],
                                       axis=0).astype(BF16))
    for h in heads:
        vt2 = jnp.concatenate([vt_ref[0, h, jp], vt_ref[0, h, qi]], axis=1)
        acc_ref[h] = jnp.dot(vt2, p_first[h], preferred_element_type=F32)

    def block(j):
        k0 = pl.multiple_of(j * BLK, BLK)
        st = [jnp.dot(k_ref[0, pl.ds(k0, BLK), h * LANES:(h + 1) * LANES], qaug[h],
                      preferred_element_type=F32) for h in heads]
        p, alpha = [], []
        for h in heads:
            hg = 2 * hp + h
            c = fb_ref[b, hg, qi] - fb_ref[b, hg, j]
            m_old = m_ref[h]
            m_new = jnp.maximum(m_old, jnp.max(st[h], axis=0, keepdims=True) + c)
            p.append(jnp.exp(st[h] - (m_new - c)).astype(BF16))
            alpha.append(jnp.exp(m_old - m_new))
            m_ref[h] = m_new
        for h in heads:
            acc_ref[h] = alpha[h] * acc_ref[h] + jnp.dot(vt_ref[0, h, j], p[h],
                                                         preferred_element_type=F32)

    th = []
    for h in range(2):
        hg = 2 * hp + h
        kmax = jnp.sqrt(jnp.sum(jnp.where(lane == hg, kmax_sq, 0.0), axis=-1, keepdims=True))
        bound = qn_ref[0, pl.ds(hg, 1), :] * kmax + gt_ref[0, pl.ds(hg, 1), :] - m_ref[h]
        th.append(jnp.max(bound) + SKIP_LOG)

    def needed(j):
        jn = jnp.clip(j + 1, 0, qi)
        gap0 = fb_ref[b, 2 * hp, jn] - fb_ref[b, 2 * hp, qi]
        gap1 = fb_ref[b, 2 * hp + 1, jn] - fb_ref[b, 2 * hp + 1, qi]
        return (j >= 0) & ((gap0 <= th[0]) | (gap1 <= th[1]))

    def body(j):
        block(j)
        return j - 1

    lax.while_loop(needed, body, qi - 2)

    out_t = jnp.concatenate(
        [acc_ref[h, 0:HEAD_DIM, :] / acc_ref[h, HEAD_DIM:HEAD_DIM + 1, :] for h in range(2)], axis=0)
    o_ref[0] = out_t.T.astype(o_ref.dtype)


def _fox(fb, qt, feat, kf, vt, gt, qn, kmsq):
    B, _, S = qt.shape
    n_pair = qt.shape[1] // LANES
    nb = S // BLK
    return pl.pallas_call(
        _fox_kernel,
        grid_spec=pltpu.PrefetchScalarGridSpec(
            num_scalar_prefetch=1,
            grid=(B, n_pair, nb),
            in_specs=[
                pl.BlockSpec((1, LANES, BLK), lambda b, h, i, s: (b, h, i)),
                pl.BlockSpec((1, 2 * FEAT_ROWS, BLK), lambda b, h, i, s: (b, h, i)),
                pl.BlockSpec((1, S, 2 * LANES), lambda b, h, i, s: (b, 0, h)),
                pl.BlockSpec((1, 2, nb, V_ROWS, BLK), lambda b, h, i, s: (b, h, 0, 0, 0)),
                pl.BlockSpec((1, 8, BLK), lambda b, h, i, s: (b, 0, i)),
                pl.BlockSpec((1, 8, BLK), lambda b, h, i, s: (b, 0, i)),
                pl.BlockSpec((1, nb, 1, LANES), lambda b, h, i, s: (b, 0, 0, 0)),
            ],
            out_specs=pl.BlockSpec((1, BLK, LANES), lambda b, h, i, s: (b, i, h)),
            scratch_shapes=[
                pltpu.VMEM((2, 1, BLK), F32),
                pltpu.VMEM((2, V_ROWS, BLK), F32),
            ],
        ),
        out_shape=jax.ShapeDtypeStruct((B, S, n_pair * LANES), BF16),
        compiler_params=_params(("arbitrary", "arbitrary", "arbitrary")),
        name="fox",
    )(fb, qt, feat, kf, vt, gt, qn, kmsq)


def _sb_kernel(qt_ref, k_ref, vt_ref, tri_ref, o_ref, r_ref, acc_ref):
    qi = pl.program_id(2)
    bq = qt_ref.shape[2]
    row = lax.broadcasted_iota(jnp.int32, (BLK, bq), 0)
    col = lax.broadcasted_iota(jnp.int32, (BLK, bq), 1)
    strict = row < col
    qt = qt_ref[0]
    top = lax.broadcasted_iota(jnp.int32, qt.shape, 0) < HEAD_DIM
    zero = jnp.zeros_like(qt)
    qh = (jnp.where(top, qt, zero), jnp.where(top, zero, qt))
    tri2 = tri_ref[...]
    heads = range(2)

    def log1m_beta(z):
        return jnp.minimum(-z, 0.0) - jnp.log(1.0 + jnp.exp(-jnp.abs(z)))

    def suffix(l1):
        hi = l1.astype(BF16)
        lo = (l1 - hi.astype(F32)).astype(BF16)
        return jnp.dot(tri2, jnp.concatenate([hi, lo], axis=0), preferred_element_type=F32)

    jp = jnp.maximum(qi - 1, 0)
    kp0 = pl.multiple_of(jp * BLK, BLK)
    kd0 = pl.multiple_of(qi * BLK, BLK)
    has_prev = qi > 0
    cp = jnp.where(has_prev, 0.0, NEG_INF)
    k2 = jnp.concatenate([k_ref[0, pl.ds(kp0, BLK), :], k_ref[0, pl.ds(kd0, BLK), :]], axis=0)
    z2 = [jnp.dot(k2, qh[h], preferred_element_type=F32) for h in heads]
    l1p = [log1m_beta(z2[h][0:BLK]) for h in heads]
    l1d = [jnp.where(strict, log1m_beta(z2[h][BLK:2 * BLK]), 0.0) for h in heads]
    sfp = [suffix(l1p[h]) for h in heads]
    sfd = [suffix(l1d[h]) for h in heads]
    a_first = []
    for h in heads:
        tot_d = sfd[h][0:1, :] + l1d[h][0:1, :]
        tot_p = sfp[h][0:1, :] + l1p[h][0:1, :]
        a_d = jnp.where(strict, jnp.exp(z2[h][BLK:2 * BLK] + l1d[h] + sfd[h]), 0.0)
        a_p = jnp.exp(z2[h][0:BLK] + l1p[h] + sfp[h] + (tot_d + cp))
        a_first.append(jnp.concatenate([a_p, a_d], axis=0).astype(BF16))
        r_ref[h] = tot_d + jnp.where(has_prev, tot_p, 0.0)
    for h in heads:
        vt2 = jnp.concatenate([vt_ref[0, h, jp], vt_ref[0, h, qi]], axis=1)
        acc_ref[h] = jnp.dot(vt2, a_first[h], preferred_element_type=F32)

    def block(j):
        k0 = pl.multiple_of(j * BLK, BLK)
        kb = k_ref[0, pl.ds(k0, BLK), :]
        z = [jnp.dot(kb, qh[h], preferred_element_type=F32) for h in heads]
        l1 = [log1m_beta(z[h]) for h in heads]
        sfx = [suffix(l1[h]) for h in heads]
        a = []
        for h in heads:
            r_old = r_ref[h]
            a.append(jnp.exp(z[h] + l1[h] + sfx[h] + r_old).astype(BF16))
            r_ref[h] = r_old + (sfx[h][0:1, :] + l1[h][0:1, :])
        for h in heads:
            acc_ref[h] = acc_ref[h] + jnp.dot(vt_ref[0, h, j], a[h], preferred_element_type=F32)
        return jnp.maximum(jnp.max(r_ref[0]), jnp.max(r_ref[1]))

    def body(carry):
        j, _ = carry
        return j - 1, block(j)

    rmax = jnp.maximum(jnp.max(r_ref[0]), jnp.max(r_ref[1]))
    lax.while_loop(lambda c: (c[0] >= 0) & (c[1] > -SKIP_LOG), body, (qi - 2, rmax))

    o_ref[0] = jnp.concatenate([acc_ref[0], acc_ref[1]], axis=0).T.astype(o_ref.dtype)


def _sb(qt, ks, vt, tri):
    B, _, S = qt.shape
    n_pair = qt.shape[1] // LANES
    nb = S // BLK
    return pl.pallas_call(
        _sb_kernel,
        grid=(B, n_pair, nb),
        in_specs=[
            pl.BlockSpec((1, LANES, BLK), lambda b, h, i: (b, h, i)),
            pl.BlockSpec((1, S, LANES), lambda b, h, i: (b, 0, h)),
            pl.BlockSpec((1, 2, nb, HEAD_DIM, BLK), lambda b, h, i: (b, h, 0, 0, 0)),
            pl.BlockSpec((BLK, 2 * BLK), lambda b, h, i: (0, 0)),
        ],
        out_specs=pl.BlockSpec((1, BLK, LANES), lambda b, h, i: (b, i, h)),
        out_shape=jax.ShapeDtypeStruct((B, S, n_pair * LANES), BF16),
        scratch_shapes=[
            pltpu.VMEM((2, 1, BLK), F32),
            pltpu.VMEM((2, HEAD_DIM, BLK), F32),
        ],
        compiler_params=_params(("arbitrary", "arbitrary", "arbitrary")),
        name="sb",
    )(qt, ks, vt, tri)


def _pre1_kernel(x_ref, g_ref, wt_ref, pos_ref, inv_ref, qt_ref, k_ref, vt_ref, *, n_q, n_kv):
    tm = x_ref.shape[1]
    half = HEAD_DIM // 2
    hb = _rms(x_ref[0], g_ref[...]).astype(BF16)
    ang = inv_ref[...] * pos_ref[0].astype(F32)
    cos = jnp.cos(ang)
    sin = jnp.sin(ang)

    def proj_t(r0, rows):
        return lax.dot_general(wt_ref[r0:r0 + rows, :], hb, _NT, preferred_element_type=F32)

    def rope_t(x):
        x1, x2 = x[0:half], x[half:HEAD_DIM]
        return jnp.concatenate([x1 * cos - x2 * sin, x2 * cos + x1 * sin], axis=0)

    qw = n_q * HEAD_DIM
    for c in range(n_q // 4):
        qt = proj_t(c * 4 * HEAD_DIM, 4 * HEAD_DIM)
        for a in range(4):
            hq = c * 4 + a
            qt_ref[0, hq * HEAD_DIM:(hq + 1) * HEAD_DIM, :] = rope_t(
                qt[a * HEAD_DIM:(a + 1) * HEAD_DIM]).astype(BF16)
    kt = proj_t(qw, n_kv * HEAD_DIM)
    zpad = jnp.zeros((LANES - HEAD_DIM, tm), F32)
    for g in range(n_kv):
        kg = jnp.concatenate([rope_t(kt[g * HEAD_DIM:(g + 1) * HEAD_DIM]), zpad], axis=0)
        k_ref[0, :, g * LANES:(g + 1) * LANES] = kg.T.astype(BF16)
    vt = proj_t(qw + n_kv * HEAD_DIM, n_kv * HEAD_DIM).astype(BF16)
    pad_row = lax.broadcasted_iota(jnp.int32, (V_ROWS - HEAD_DIM, tm), 0)
    ones_pad = jnp.where(pad_row == 0, 1.0, 0.0).astype(BF16)
    for g in range(n_kv):
        vt_ref[0, g * V_ROWS:g * V_ROWS + HEAD_DIM, :] = vt[g * HEAD_DIM:(g + 1) * HEAD_DIM]
        vt_ref[0, g * V_ROWS + HEAD_DIM:(g + 1) * V_ROWS, :] = ones_pad


def _pre1(h, g, wt, pos, inv, tm, n_q, n_kv):
    B, S, D = h.shape
    return pl.pallas_call(
        functools.partial(_pre1_kernel, n_q=n_q, n_kv=n_kv),
        grid=(B, S // tm),
        in_specs=[
            pl.BlockSpec((1, tm, D), lambda b, i: (b, i, 0)),
            pl.BlockSpec((1, D), lambda b, i: (0, 0)),
            pl.BlockSpec(wt.shape, lambda b, i: (0, 0)),
            pl.BlockSpec((1, 1, tm), lambda b, i: (b, 0, i)),
            pl.BlockSpec(inv.shape, lambda b, i: (0, 0)),
        ],
        out_specs=[
            pl.BlockSpec((1, n_q * HEAD_DIM, tm), lambda b, i: (b, 0, i)),
            pl.BlockSpec((1, tm, n_kv * LANES), lambda b, i: (b, i, 0)),
            pl.BlockSpec((1, n_kv * V_ROWS, tm), lambda b, i: (b, 0, i)),
        ],
        out_shape=[
            jax.ShapeDtypeStruct((B, n_q * HEAD_DIM, S), BF16),
            jax.ShapeDtypeStruct((B, S, n_kv * LANES), BF16),
            jax.ShapeDtypeStruct((B, n_kv * V_ROWS, S), BF16),
        ],
        compiler_params=_params(("arbitrary", "arbitrary")),
        name="pre1",
    )(h, g, wt, pos, inv)


def _swa_kernel(sink_ref, qt_ref, kp_ref, ko_ref, vp_ref, vo_ref, o_ref, *, n_kv, group):
    i = pl.program_id(1)
    W = qt_ref.shape[2]
    r = lax.broadcasted_iota(jnp.int32, (2 * W, W), 0)
    c = lax.broadcasted_iota(jnp.int32, (2 * W, W), 1)
    rel = c + W - r
    valid = (rel >= 0) & (rel < W) & ((r >= W) | (i > 0))
    valid = jnp.concatenate([valid] * group, axis=1)
    seg = lax.broadcasted_iota(jnp.int32, (1, group * W), 1) // W
    zpad = jnp.zeros((LANES - HEAD_DIM, group * W), BF16)
    groups = range(n_kv)
    st = []
    for g in groups:
        kb = jnp.concatenate([kp_ref[0, :, g * LANES:(g + 1) * LANES],
                              ko_ref[0, :, g * LANES:(g + 1) * LANES]], axis=0)
        qg = jnp.concatenate(
            [qt_ref[0, (g * group + a) * HEAD_DIM:(g * group + a + 1) * HEAD_DIM, :]
             for a in range(group)], axis=1)
        st.append(jnp.dot(kb, jnp.concatenate([qg, zpad], axis=0), preferred_element_type=F32))
    p, sink_term = [], []
    for g in groups:
        sg = jnp.where(valid, st[g], NEG_INF)
        sink = jnp.zeros((1, group * W), F32)
        for a in range(group):
            sink = jnp.where(seg == a, sink_ref[g * group + a], sink)
        m = jnp.maximum(jnp.max(sg, axis=0, keepdims=True), sink)
        p.append(jnp.exp(sg - m).astype(BF16))
        sink_term.append(jnp.exp(sink - m))
    acc = []
    for g in groups:
        vt = jnp.concatenate([vp_ref[0, g * V_ROWS:(g + 1) * V_ROWS, :],
                              vo_ref[0, g * V_ROWS:(g + 1) * V_ROWS, :]], axis=1)
        acc.append(jnp.dot(vt, p[g], preferred_element_type=F32))
    for g in groups:
        o = acc[g][0:HEAD_DIM] / (acc[g][HEAD_DIM:HEAD_DIM + 1] + sink_term[g])
        for a in range(0, group, 2):
            pair = jnp.concatenate([o[:, a * W:(a + 1) * W], o[:, (a + 1) * W:(a + 2) * W]], axis=0)
            l0 = (g * group + a) * HEAD_DIM
            o_ref[0, :, l0:l0 + 2 * HEAD_DIM] = pair.T.astype(o_ref.dtype)


def _swa(qt, kpad, vt, sinks, n_q, n_kv):
    B, _, S = qt.shape
    W = WINDOW
    prev = lambda i: jnp.maximum(i - 1, 0)
    return pl.pallas_call(
        functools.partial(_swa_kernel, n_kv=n_kv, group=n_q // n_kv),
        grid_spec=pltpu.PrefetchScalarGridSpec(
            num_scalar_prefetch=1,
            grid=(B, S // W),
            in_specs=[
                pl.BlockSpec((1, n_q * HEAD_DIM, W), lambda b, i, s: (b, 0, i)),
                pl.BlockSpec((1, W, n_kv * LANES), lambda b, i, s: (b, prev(i), 0)),
                pl.BlockSpec((1, W, n_kv * LANES), lambda b, i, s: (b, i, 0)),
                pl.BlockSpec((1, n_kv * V_ROWS, W), lambda b, i, s: (b, 0, prev(i))),
                pl.BlockSpec((1, n_kv * V_ROWS, W), lambda b, i, s: (b, 0, i)),
            ],
            out_specs=pl.BlockSpec((1, W, n_q * HEAD_DIM), lambda b, i, s: (b, i, 0)),
        ),
        out_shape=jax.ShapeDtypeStruct((B, S, n_q * HEAD_DIM), BF16),
        compiler_params=_params(("arbitrary", "arbitrary")),
        name="swa",
    )(sinks, qt, kpad, kpad, vt, vt)


def _post_kernel(*refs, n_mix, final_norm):
    h_ref = refs[0]
    mix_refs = refs[1:1 + n_mix]
    (p_ref, wo_ref, gf_ref, wg_ref, wu_ref, wd_ref, gp_ref, wpg_ref, wpp_ref, gfin_ref,
     o_ref) = refs[1 + n_mix:]
    h = h_ref[...]
    off = 0
    for m_ref in mix_refs:
        w = m_ref.shape[1]
        h = h + jnp.dot(m_ref[...], wo_ref[off:off + w, :], preferred_element_type=F32)
        off += w
    hb = _rms(h, gf_ref[...]).astype(BF16)
    g = jnp.dot(hb, wg_ref[...], preferred_element_type=F32)
    u = jnp.dot(hb, wu_ref[...], preferred_element_type=F32)
    act = (g * jax.nn.sigmoid(g) * u).astype(BF16)
    h = h + jnp.dot(act, wd_ref[...], preferred_element_type=F32)
    gate = jax.nn.sigmoid(jnp.dot(_rms(h, gp_ref[...]).astype(BF16), wpg_ref[...],
                                  preferred_element_type=F32))
    h = h + gate * jnp.dot(p_ref[...].astype(BF16), wpp_ref[...], preferred_element_type=F32)
    if final_norm:
        h = _rms(h, gfin_ref[...])
    o_ref[...] = h


def _post(h, mixes, p, wo, gf, wg, wu, wd, gp, wpg, wpp, gfin, tm, final_norm):
    T, D = h.shape
    row = lambda w: pl.BlockSpec((tm, w), lambda i: (i, 0))
    full = lambda a: pl.BlockSpec(a.shape, lambda i: (0, 0))
    consts = (wo, gf, wg, wu, wd, gp, wpg, wpp, gfin)
    return pl.pallas_call(
        functools.partial(_post_kernel, n_mix=len(mixes), final_norm=final_norm),
        grid=(T // tm,),
        in_specs=[row(D)] + [row(m.shape[1]) for m in mixes] + [row(p.shape[1])]
        + [full(c) for c in consts],
        out_specs=row(D),
        out_shape=jax.ShapeDtypeStruct((T, D), F32),
        compiler_params=_params(("arbitrary",)),
        name="post",
    )(h, *mixes, p, *consts)


def _layer0_weights(w_in, b_f, n_fox):
    fw = n_fox * HEAD_DIM
    scale = HEAD_DIM ** -0.5
    D = w_in.shape[0]
    qa, ka, va, qs, ks, vs = (w_in[:, i * fw:(i + 1) * fw] for i in range(6))
    wt = jnp.concatenate([qa * scale, va, qs * scale, vs], axis=1).T.astype(BF16)
    ka_pad = jnp.pad(ka.reshape(D, n_fox, HEAD_DIM), ((0, 0), (0, 0), (0, LANES - HEAD_DIM)))
    wk = jnp.concatenate([ka_pad.reshape(D, n_fox * LANES), ks], axis=1).astype(BF16)
    wf = jnp.pad(w_in[:, 6 * fw:], ((0, 0), (0, LANES - n_fox))).astype(BF16)
    bf = jnp.pad(b_f, (0, LANES - n_fox)).reshape(1, LANES)
    heads = jnp.arange(n_fox)
    pk = jnp.zeros((3 * LANES, n_fox * LANES), F32)
    aq = jnp.zeros((n_fox * FEAT_ROWS, 3 * LANES), F32)
    for piece in range(3):
        pk = pk.at[piece * LANES + heads, heads * LANES + HEAD_DIM + 3 + piece].set(-1.0)
        aq = aq.at[heads * FEAT_ROWS + piece, piece * LANES + heads].set(1.0)
    return wt, wk, wf, bf, pk.astype(BF16), aq.astype(BF16)


def kernel(x, p, positions, norm_mix, norm_ffn, norm_ple, norm_final, ev_w_in, ev_b_f, ev_w_out,
           od_w_in, od_sinks, od_w_out, ffn_w_gate, ffn_w_up, ffn_w_down, ple_w_proj, ple_w_gate):
    B, S, D = x.shape
    T = B * S
    n_heads = D // HEAD_DIM
    n_fox = n_heads // 2
    fox_w = n_fox * HEAD_DIM
    n_q, n_kv = n_heads, 4
    assert S % BLK == 0 and n_fox == 8
    row = lambda a: a.reshape(1, -1)

    wt, wk, wf, bf, pk, aq = _layer0_weights(ev_w_in[0], ev_b_f[0], n_fox)
    (qtf, qts, vtf, vts, kf, ks, feat, gt, qn, fb, kmsq) = _pre0(
        x, row(norm_mix[0]), wt, wk, wf, bf, pk, aq, n_fox)
    fb_heads = fb[:, :, 0, :n_fox].transpose(0, 2, 1)
    o_fox = _fox(fb_heads, qtf, feat, kf, vtf, gt, qn, kmsq)
    tri = (jnp.arange(BLK)[None, :] > jnp.arange(BLK)[:, None]).astype(BF16)
    o_sb = _sb(qts, ks, vts, jnp.concatenate([tri, tri], axis=1))

    tm = min(512, T)
    h = _post(x.reshape(T, D), [o_fox.reshape(T, fox_w), o_sb.reshape(T, fox_w)],
              p[0].reshape(T, -1), ev_w_out[0].astype(BF16), row(norm_ffn[0]),
              ffn_w_gate[0].astype(BF16), ffn_w_up[0].astype(BF16), ffn_w_down[0].astype(BF16),
              row(norm_ple[0]), ple_w_gate[0].astype(BF16), ple_w_proj[0].astype(BF16),
              row(norm_final), tm, final_norm=False)

    qw = n_q * HEAD_DIM
    kw = n_kv * HEAD_DIM
    col_scale1 = jnp.ones((qw + 2 * kw,), F32).at[:qw].set(HEAD_DIM ** -0.5)
    w1t = (od_w_in[0] * col_scale1).T.astype(BF16)
    half = HEAD_DIM // 2
    inv = (ROPE_THETA ** (-jnp.arange(half, dtype=F32) / half)).reshape(half, 1)
    qt1, k1, vt1 = _pre1(h.reshape(B, S, D), row(norm_mix[1]), w1t, positions.reshape(B, 1, S),
                         inv, min(tm, S), n_q, n_kv)
    o_swa = _swa(qt1, k1, vt1, od_sinks[0], n_q, n_kv)
    out = _post(h, [o_swa.reshape(T, qw)], p[1].reshape(T, -1), od_w_out[0].astype(BF16),
                row(norm_ffn[1]), ffn_w_gate[1].astype(BF16), ffn_w_up[1].astype(BF16),
                ffn_w_down[1].astype(BF16), row(norm_ple[1]), ple_w_gate[1].astype(BF16),
                ple_w_proj[1].astype(BF16), row(norm_final), tm, final_norm=True)
    return out.reshape(B, S, D)
```

```python
import functools

import jax
import jax.numpy as jnp
from jax import lax
from jax.experimental import pallas as pl
from jax.experimental.pallas import tpu as pltpu

F32 = jnp.float32
BF16 = jnp.bfloat16

HEAD_DIM = 64
LANES = 128
BLK = 256
V_ROWS = 80
FEAT_ROWS = 16
WINDOW = 128
ROPE_THETA = 10000.0
EPS = 1e-6
NEG_INF = -1e30
LOG2E = 1.4426950408889634
SKIP_LOG2 = 60.0 * LOG2E
ATT_HEADS = 4
SWA_SUB = 2
VMEM_LIMIT = 56 * 1024 * 1024

_NT = (((1,), (1,)), ((), ()))


def _params(sem):
    return pltpu.CompilerParams(dimension_semantics=sem, vmem_limit_bytes=VMEM_LIMIT)


def _rms(x, g):
    return x * lax.rsqrt(jnp.mean(x * x, axis=-1, keepdims=True) + EPS) * g


def _log_sigmoid(x):
    return jnp.minimum(x, 0.0) - jnp.log(1.0 + jnp.exp(-jnp.abs(x)))


def _split3(x):
    a = x.astype(BF16)
    r = x - a.astype(F32)
    b = r.astype(BF16)
    c = (r - b.astype(F32)).astype(BF16)
    return a, b, c


def _pre0_kernel(x_ref, g_ref, wt_ref, wk_ref, wf_ref, bf_ref, pk_ref, aq_ref,
                 qtf_ref, qts_ref, vtf_ref, vts_ref, kf_ref, ks_ref, feat_ref,
                 gt_ref, qn_ref, fb_ref, kmsq_ref, carry_ref, *, n_fox):
    tm = x_ref.shape[1]
    fw = n_fox * HEAD_DIM

    @pl.when(pl.program_id(1) == 0)
    def _():
        carry_ref[...] = jnp.zeros_like(carry_ref)

    hb = _rms(x_ref[0], g_ref[...]).astype(BF16)

    def tproj(c):
        return lax.dot_general(wt_ref[c * fw:(c + 1) * fw, :], hb, _NT,
                               preferred_element_type=F32).astype(BF16)

    qtf = tproj(0)
    qtf_ref[0] = qtf
    vtf = tproj(1)
    lane_row = lax.broadcasted_iota(jnp.int32, (V_ROWS - HEAD_DIM, tm), 0)
    ones_pad = jnp.where(lane_row == 0, 1.0, 0.0).astype(BF16)
    for h in range(n_fox):
        vtf_ref[0, h, 0, 0:HEAD_DIM, :] = vtf[h * HEAD_DIM:(h + 1) * HEAD_DIM, :]
        vtf_ref[0, h, 0, HEAD_DIM:V_ROWS, :] = ones_pad
    qts_ref[0] = tproj(2)
    vts = tproj(3)
    for h in range(n_fox):
        vts_ref[0, h, 0] = vts[h * HEAD_DIM:(h + 1) * HEAD_DIM, :]

    q32 = qtf.astype(F32)
    qn_rows = [jnp.sqrt(jnp.sum(jnp.square(q32[h * HEAD_DIM:(h + 1) * HEAD_DIM, :]),
                                axis=0, keepdims=True)) for h in range(n_fox)]
    qn_ref[0] = jnp.concatenate(qn_rows, axis=0)

    kk = jnp.dot(hb, wk_ref[...], preferred_element_type=F32)
    nk = n_fox * LANES
    ks_ref[0] = kk[:, nk:].astype(BF16)
    kfox = kk[:, :nk].astype(BF16)
    k32 = kfox.astype(F32)
    lane = lax.broadcasted_iota(jnp.int32, (1, LANES), 1)
    kmsq = jnp.zeros((1, LANES), F32)
    for h in range(n_fox):
        ss = jnp.sum(jnp.square(k32[:, h * LANES:(h + 1) * LANES]), axis=-1, keepdims=True)
        kmsq = jnp.where(lane == h, jnp.max(ss, axis=0, keepdims=True), kmsq)
    kmsq_ref[0, 0] = kmsq

    gate = jnp.dot(hb, wf_ref[...], preferred_element_type=F32) + bf_ref[...]
    lf = _log_sigmoid(gate) * LOG2E
    row = lax.broadcasted_iota(jnp.int32, (tm, tm), 0)
    col = lax.broadcasted_iota(jnp.int32, (tm, tm), 1)
    tri = jnp.where(row >= col, 1.0, 0.0).astype(BF16)
    G = jnp.zeros((tm, LANES), F32)
    for piece in _split3(lf):
        G = G + jnp.dot(tri, piece, preferred_element_type=F32)
    fb_ref[0, 0] = carry_ref[...]
    carry_ref[...] = carry_ref[...] + G[tm - 1:tm, :]

    gp = jnp.concatenate(_split3(G), axis=1)
    kfeat = jnp.dot(gp, pk_ref[...], preferred_element_type=F32)
    lane_k = lax.broadcasted_iota(jnp.int32, (1, nk), 1) % LANES
    kones = jnp.where((lane_k >= HEAD_DIM) & (lane_k < HEAD_DIM + 3), 1.0, 0.0)
    kf_ref[0] = (k32 + kfeat + kones).astype(BF16)

    GT = G.T
    gt_ref[0] = GT[0:8, :]
    gpt = jnp.concatenate(_split3(GT), axis=0)
    qfeat = jnp.dot(aq_ref[...], gpt, preferred_element_type=F32)
    frow = lax.broadcasted_iota(jnp.int32, (n_fox * FEAT_ROWS, 1), 0) % FEAT_ROWS
    qones = jnp.where((frow >= 3) & (frow < 6), 1.0, 0.0)
    feat_ref[0] = (qfeat + qones).astype(BF16)


def _pre0(x, g, wt, wk, wf, bf, pk, aq, n_fox):
    B, S, D = x.shape
    tm = BLK
    nb = S // tm
    fw = n_fox * HEAD_DIM
    const = lambda a: pl.BlockSpec(a.shape, lambda b, s: (0,) * a.ndim)
    tok_lane = lambda rows: pl.BlockSpec((1, rows, tm), lambda b, s: (b, 0, s))
    return pl.pallas_call(
        functools.partial(_pre0_kernel, n_fox=n_fox),
        grid=(B, nb),
        in_specs=[pl.BlockSpec((1, tm, D), lambda b, s: (b, s, 0))]
        + [const(a) for a in (g, wt, wk, wf, bf, pk, aq)],
        out_specs=[
            tok_lane(fw),
            tok_lane(fw),
            pl.BlockSpec((1, n_fox, 1, V_ROWS, tm), lambda b, s: (b, 0, s, 0, 0)),
            pl.BlockSpec((1, n_fox, 1, HEAD_DIM, tm), lambda b, s: (b, 0, s, 0, 0)),
            pl.BlockSpec((1, tm, n_fox * LANES), lambda b, s: (b, s, 0)),
            pl.BlockSpec((1, tm, fw), lambda b, s: (b, s, 0)),
            tok_lane(n_fox * FEAT_ROWS),
            tok_lane(8),
            tok_lane(8),
            pl.BlockSpec((1, 1, 1, LANES), lambda b, s: (b, s, 0, 0)),
            pl.BlockSpec((1, 1, 1, LANES), lambda b, s: (b, s, 0, 0)),
        ],
        out_shape=[
            jax.ShapeDtypeStruct((B, fw, S), BF16),
            jax.ShapeDtypeStruct((B, fw, S), BF16),
            jax.ShapeDtypeStruct((B, n_fox, nb, V_ROWS, tm), BF16),
            jax.ShapeDtypeStruct((B, n_fox, nb, HEAD_DIM, tm), BF16),
            jax.ShapeDtypeStruct((B, S, n_fox * LANES), BF16),
            jax.ShapeDtypeStruct((B, S, fw), BF16),
            jax.ShapeDtypeStruct((B, n_fox * FEAT_ROWS, S), BF16),
            jax.ShapeDtypeStruct((B, 8, S), F32),
            jax.ShapeDtypeStruct((B, 8, S), F32),
            jax.ShapeDtypeStruct((B, nb, 1, LANES), F32),
            jax.ShapeDtypeStruct((B, nb, 1, LANES), F32),
        ],
        scratch_shapes=[pltpu.VMEM((1, LANES), F32)],
        compiler_params=_params(("arbitrary", "arbitrary")),
        name="pre0",
    )(x, g, wt, wk, wf, bf, pk, aq)


def _fox_kernel(fb_ref, qt_ref, feat_ref, k_ref, vt_ref, gt_ref, qn_ref, kmsq_ref,
                o_ref, m_ref, acc_ref):
    b = pl.program_id(0)
    hp = pl.program_id(1)
    qi = pl.program_id(2)
    bq = qt_ref.shape[2]
    n_h = acc_ref.shape[0]
    heads = range(n_h)
    hg = [n_h * hp + h for h in heads]
    row = lax.broadcasted_iota(jnp.int32, (BLK, bq), 0)
    col = lax.broadcasted_iota(jnp.int32, (BLK, bq), 1)
    causal = row <= col
    kmax_sq = jnp.max(kmsq_ref[0], axis=0)
    lane = lax.broadcasted_iota(jnp.int32, (1, LANES), 1)

    pad = jnp.zeros((LANES - HEAD_DIM - FEAT_ROWS, bq), BF16)
    qaug = [jnp.concatenate([qt_ref[0, h * HEAD_DIM:(h + 1) * HEAD_DIM, :],
                             feat_ref[0, h * FEAT_ROWS:(h + 1) * FEAT_ROWS, :], pad], axis=0)
            for h in heads]

    jp = jnp.maximum(qi - 1, 0)
    kp0 = pl.multiple_of(jp * BLK, BLK)
    kd0 = pl.multiple_of(qi * BLK, BLK)
    st = [jnp.dot(jnp.concatenate([k_ref[0, pl.ds(kp0, BLK), h * LANES:(h + 1) * LANES],
                                   k_ref[0, pl.ds(kd0, BLK), h * LANES:(h + 1) * LANES]], axis=0),
                  qaug[h], preferred_element_type=F32) for h in heads]
    p_first = []
    for h in heads:
        cp = jnp.where(qi > 0, fb_ref[b, hg[h], qi] - fb_ref[b, hg[h], jp], NEG_INF)
        st_p = st[h][0:BLK]
        st_d = jnp.where(causal, st[h][BLK:2 * BLK], NEG_INF)
        m = jnp.maximum(jnp.max(st_d, axis=0, keepdims=True),
                        jnp.max(st_p, axis=0, keepdims=True) + cp)
        m_ref[h] = m
        p_first.append(jnp.concatenate([jnp.exp2(st_p - (m - cp)), jnp.exp2(st_d - m)],
                                       axis=0).astype(BF16))
    for h in heads:
        vt2 = jnp.concatenate([vt_ref[0, h, jp], vt_ref[0, h, qi]], axis=1)
        acc_ref[h] = jnp.dot(vt2, p_first[h], preferred_element_type=F32)

    def block(j):
        k0 = pl.multiple_of(j * BLK, BLK)
        st = [jnp.dot(k_ref[0, pl.ds(k0, BLK), h * LANES:(h + 1) * LANES], qaug[h],
                      preferred_element_type=F32) for h in heads]
        p, alpha = [], []
        for h in heads:
            c = fb_ref[b, hg[h], qi] - fb_ref[b, hg[h], j]
            m_old = m_ref[h]
            m_new = jnp.maximum(m_old, jnp.max(st[h], axis=0, keepdims=True) + c)
            p.append(jnp.exp2(st[h] - (m_new - c)).astype(BF16))
            alpha.append(jnp.exp2(m_old - m_new))
            m_ref[h] = m_new
        for h in heads:
            acc_ref[h] = alpha[h] * acc_ref[h] + jnp.dot(vt_ref[0, h, j], p[h],
                                                         preferred_element_type=F32)

    th = []
    for h in heads:
        kmax = jnp.sqrt(jnp.sum(jnp.where(lane == hg[h], kmax_sq, 0.0), axis=-1, keepdims=True))
        bound = qn_ref[0, pl.ds(hg[h], 1), :] * kmax + gt_ref[0, pl.ds(hg[h], 1), :] - m_ref[h]
        th.append(jnp.max(bound) + SKIP_LOG2)

    def needed(j):
        jn = jnp.clip(j + 1, 0, qi)
        need = fb_ref[b, hg[0], jn] - fb_ref[b, hg[0], qi] <= th[0]
        for h in heads[1:]:
            need = need | (fb_ref[b, hg[h], jn] - fb_ref[b, hg[h], qi] <= th[h])
        return (j >= 0) & need

    def body(j):
        block(j)
        return j - 1

    lax.while_loop(needed, body, qi - 2)

    out_t = jnp.concatenate(
        [acc_ref[h, 0:HEAD_DIM, :] / acc_ref[h, HEAD_DIM:HEAD_DIM + 1, :] for h in heads], axis=0)
    o_ref[0] = out_t.T.astype(o_ref.dtype)


def _fox(fb, qt, feat, kf, vt, gt, qn, kmsq):
    B, _, S = qt.shape
    n_h = ATT_HEADS
    n_grp = qt.shape[1] // (n_h * HEAD_DIM)
    nb = S // BLK
    return pl.pallas_call(
        _fox_kernel,
        grid_spec=pltpu.PrefetchScalarGridSpec(
            num_scalar_prefetch=1,
            grid=(B, n_grp, nb),
            in_specs=[
                pl.BlockSpec((1, n_h * HEAD_DIM, BLK), lambda b, h, i, s: (b, h, i)),
                pl.BlockSpec((1, n_h * FEAT_ROWS, BLK), lambda b, h, i, s: (b, h, i)),
                pl.BlockSpec((1, S, n_h * LANES), lambda b, h, i, s: (b, 0, h)),
                pl.BlockSpec((1, n_h, nb, V_ROWS, BLK), lambda b, h, i, s: (b, h, 0, 0, 0)),
                pl.BlockSpec((1, 8, BLK), lambda b, h, i, s: (b, 0, i)),
                pl.BlockSpec((1, 8, BLK), lambda b, h, i, s: (b, 0, i)),
                pl.BlockSpec((1, nb, 1, LANES), lambda b, h, i, s: (b, 0, 0, 0)),
            ],
            out_specs=pl.BlockSpec((1, BLK, n_h * HEAD_DIM), lambda b, h, i, s: (b, i, h)),
            scratch_shapes=[
                pltpu.VMEM((n_h, 1, BLK), F32),
                pltpu.VMEM((n_h, V_ROWS, BLK), F32),
            ],
        ),
        out_shape=jax.ShapeDtypeStruct((B, S, n_grp * n_h * HEAD_DIM), BF16),
        compiler_params=_params(("arbitrary", "arbitrary", "arbitrary")),
        name="fox",
    )(fb, qt, feat, kf, vt, gt, qn, kmsq)


def _sb_kernel(qt_ref, k_ref, vt_ref, tri_ref, o_ref, r_ref, acc_ref):
    qi = pl.program_id(2)
    bq = qt_ref.shape[2]
    n_h = acc_ref.shape[0]
    heads = range(n_h)
    row = lax.broadcasted_iota(jnp.int32, (BLK, bq), 0)
    col = lax.broadcasted_iota(jnp.int32, (BLK, bq), 1)
    strict = row < col
    top = lax.broadcasted_iota(jnp.int32, (LANES, bq), 0) < HEAD_DIM
    zero = jnp.zeros((LANES, bq), BF16)
    qh = []
    for h in heads:
        pair = qt_ref[0, (h // 2) * LANES:(h // 2 + 1) * LANES, :]
        qh.append(jnp.where(top, pair, zero) if h % 2 == 0 else jnp.where(top, zero, pair))
    tri2 = tri_ref[...]

    def kpair(k0, h):
        return k_ref[0, pl.ds(k0, BLK), (h // 2) * LANES:(h // 2 + 1) * LANES]

    def log1m_beta(z):
        nz = -z
        return jnp.minimum(nz, 0.0) - jnp.log2(1.0 + jnp.exp2(jnp.minimum(z, nz)))

    def suffix(l1):
        hi = l1.astype(BF16)
        lo = (l1 - hi.astype(F32)).astype(BF16)
        return jnp.dot(tri2, jnp.concatenate([hi, lo], axis=0), preferred_element_type=F32)

    jp = jnp.maximum(qi - 1, 0)
    kp0 = pl.multiple_of(jp * BLK, BLK)
    kd0 = pl.multiple_of(qi * BLK, BLK)
    has_prev = qi > 0
    cp = jnp.where(has_prev, 0.0, NEG_INF)
    z2 = [jnp.dot(jnp.concatenate([kpair(kp0, h), kpair(kd0, h)], axis=0), qh[h],
                  preferred_element_type=F32) for h in heads]
    l1p = [log1m_beta(z2[h][0:BLK]) for h in heads]
    l1d = [jnp.where(strict, log1m_beta(z2[h][BLK:2 * BLK]), 0.0) for h in heads]
    sfp = [suffix(l1p[h]) for h in heads]
    sfd = [suffix(l1d[h]) for h in heads]
    a_first = []
    for h in heads:
        tot_d = sfd[h][0:1, :] + l1d[h][0:1, :]
        tot_p = sfp[h][0:1, :] + l1p[h][0:1, :]
        a_d = jnp.where(strict, jnp.exp2(z2[h][BLK:2 * BLK] + l1d[h] + sfd[h]), 0.0)
        a_p = jnp.exp2(z2[h][0:BLK] + l1p[h] + sfp[h] + (tot_d + cp))
        a_first.append(jnp.concatenate([a_p, a_d], axis=0).astype(BF16))
        r_ref[h] = tot_d + jnp.where(has_prev, tot_p, 0.0)
    for h in heads:
        vt2 = jnp.concatenate([vt_ref[0, h, jp], vt_ref[0, h, qi]], axis=1)
        acc_ref[h] = jnp.dot(vt2, a_first[h], preferred_element_type=F32)

    def rmax():
        out = jnp.max(r_ref[0])
        for h in heads[1:]:
            out = jnp.maximum(out, jnp.max(r_ref[h]))
        return out

    def block(j):
        k0 = pl.multiple_of(j * BLK, BLK)
        z = [jnp.dot(kpair(k0, h), qh[h], preferred_element_type=F32) for h in heads]
        l1 = [log1m_beta(z[h]) for h in heads]
        sfx = [suffix(l1[h]) for h in heads]
        a = []
        for h in heads:
            r_old = r_ref[h]
            a.append(jnp.exp2(z[h] + l1[h] + sfx[h] + r_old).astype(BF16))
            r_ref[h] = r_old + (sfx[h][0:1, :] + l1[h][0:1, :])
        for h in heads:
            acc_ref[h] = acc_ref[h] + jnp.dot(vt_ref[0, h, j], a[h], preferred_element_type=F32)
        return rmax()

    def body(carry):
        j, _ = carry
        return j - 1, block(j)

    lax.while_loop(lambda c: (c[0] >= 0) & (c[1] > -SKIP_LOG2), body, (qi - 2, rmax()))

    o_ref[0] = jnp.concatenate([acc_ref[h] for h in heads], axis=0).T.astype(o_ref.dtype)


def _sb(qt, ks, vt, tri):
    B, _, S = qt.shape
    n_h = ATT_HEADS
    n_grp = qt.shape[1] // (n_h * HEAD_DIM)
    nb = S // BLK
    return pl.pallas_call(
        _sb_kernel,
        grid=(B, n_grp, nb),
        in_specs=[
            pl.BlockSpec((1, n_h * HEAD_DIM, BLK), lambda b, h, i: (b, h, i)),
            pl.BlockSpec((1, S, n_h * HEAD_DIM), lambda b, h, i: (b, 0, h)),
            pl.BlockSpec((1, n_h, nb, HEAD_DIM, BLK), lambda b, h, i: (b, h, 0, 0, 0)),
            pl.BlockSpec((BLK, 2 * BLK), lambda b, h, i: (0, 0)),
        ],
        out_specs=pl.BlockSpec((1, BLK, n_h * HEAD_DIM), lambda b, h, i: (b, i, h)),
        out_shape=jax.ShapeDtypeStruct((B, S, n_grp * n_h * HEAD_DIM), BF16),
        scratch_shapes=[
            pltpu.VMEM((n_h, 1, BLK), F32),
            pltpu.VMEM((n_h, HEAD_DIM, BLK), F32),
        ],
        compiler_params=_params(("arbitrary", "arbitrary", "arbitrary")),
        name="sb",
    )(qt, ks, vt, tri)


def _pre1_kernel(x_ref, g_ref, wt_ref, pos_ref, inv_ref, qt_ref, k_ref, vt_ref, *, n_q, n_kv):
    tm = x_ref.shape[1]
    half = HEAD_DIM // 2
    hb = _rms(x_ref[0], g_ref[...]).astype(BF16)
    ang = inv_ref[...] * pos_ref[0].astype(F32)
    cos = jnp.cos(ang)
    sin = jnp.sin(ang)

    def proj_t(r0, rows):
        return lax.dot_general(wt_ref[r0:r0 + rows, :], hb, _NT, preferred_element_type=F32)

    def rope_t(x):
        x1, x2 = x[0:half], x[half:HEAD_DIM]
        return jnp.concatenate([x1 * cos - x2 * sin, x2 * cos + x1 * sin], axis=0)

    qw = n_q * HEAD_DIM
    for c in range(n_q // 4):
        qt = proj_t(c * 4 * HEAD_DIM, 4 * HEAD_DIM)
        for a in range(4):
            hq = c * 4 + a
            qt_ref[0, hq * HEAD_DIM:(hq + 1) * HEAD_DIM, :] = rope_t(
                qt[a * HEAD_DIM:(a + 1) * HEAD_DIM]).astype(BF16)
    kt = proj_t(qw, n_kv * HEAD_DIM)
    zpad = jnp.zeros((LANES - HEAD_DIM, tm), F32)
    for g in range(n_kv):
        kg = jnp.concatenate([rope_t(kt[g * HEAD_DIM:(g + 1) * HEAD_DIM]), zpad], axis=0)
        k_ref[0, :, g * LANES:(g + 1) * LANES] = kg.T.astype(BF16)
    vt = proj_t(qw + n_kv * HEAD_DIM, n_kv * HEAD_DIM).astype(BF16)
    pad_row = lax.broadcasted_iota(jnp.int32, (V_ROWS - HEAD_DIM, tm), 0)
    ones_pad = jnp.where(pad_row == 0, 1.0, 0.0).astype(BF16)
    for g in range(n_kv):
        vt_ref[0, g * V_ROWS:g * V_ROWS + HEAD_DIM, :] = vt[g * HEAD_DIM:(g + 1) * HEAD_DIM]
        vt_ref[0, g * V_ROWS + HEAD_DIM:(g + 1) * V_ROWS, :] = ones_pad


def _pre1(h, g, wt, pos, inv, tm, n_q, n_kv):
    B, S, D = h.shape
    return pl.pallas_call(
        functools.partial(_pre1_kernel, n_q=n_q, n_kv=n_kv),
        grid=(B, S // tm),
        in_specs=[
            pl.BlockSpec((1, tm, D), lambda b, i: (b, i, 0)),
            pl.BlockSpec((1, D), lambda b, i: (0, 0)),
            pl.BlockSpec(wt.shape, lambda b, i: (0, 0)),
            pl.BlockSpec((1, 1, tm), lambda b, i: (b, 0, i)),
            pl.BlockSpec(inv.shape, lambda b, i: (0, 0)),
        ],
        out_specs=[
            pl.BlockSpec((1, n_q * HEAD_DIM, tm), lambda b, i: (b, 0, i)),
            pl.BlockSpec((1, tm, n_kv * LANES), lambda b, i: (b, i, 0)),
            pl.BlockSpec((1, n_kv * V_ROWS, tm), lambda b, i: (b, 0, i)),
        ],
        out_shape=[
            jax.ShapeDtypeStruct((B, n_q * HEAD_DIM, S), BF16),
            jax.ShapeDtypeStruct((B, S, n_kv * LANES), BF16),
            jax.ShapeDtypeStruct((B, n_kv * V_ROWS, S), BF16),
        ],
        compiler_params=_params(("arbitrary", "arbitrary")),
        name="pre1",
    )(h, g, wt, pos, inv)


def _swa_kernel(sink_ref, qt_ref, kp_ref, ko_ref, vp_ref, vo_ref, o_ref, *, n_kv, group):
    i = pl.program_id(1)
    W = WINDOW
    n_sub = qt_ref.shape[2] // W
    r = lax.broadcasted_iota(jnp.int32, (2 * W, W), 0)
    c = lax.broadcasted_iota(jnp.int32, (2 * W, W), 1)
    rel = c + W - r
    band = (rel >= 0) & (rel < W)
    valid = [jnp.concatenate([band & ((r >= W) | (i > 0)) if u == 0 else band] * group, axis=1)
             for u in range(n_sub)]
    seg = lax.broadcasted_iota(jnp.int32, (1, group * W), 1) // W
    zpad = jnp.zeros((LANES - HEAD_DIM, group * W), BF16)
    chains = [(u, g) for u in range(n_sub) for g in range(n_kv)]

    def keys(u, g):
        ls = slice(g * LANES, (g + 1) * LANES)
        if u == 0:
            return jnp.concatenate([kp_ref[0, :, ls], ko_ref[0, 0:W, ls]], axis=0)
        return ko_ref[0, (u - 1) * W:(u + 1) * W, ls]

    def values(u, g):
        rs = slice(g * V_ROWS, (g + 1) * V_ROWS)
        if u == 0:
            return jnp.concatenate([vp_ref[0, rs, :], vo_ref[0, rs, 0:W]], axis=1)
        return vo_ref[0, rs, (u - 1) * W:(u + 1) * W]

    st = []
    for u, g in chains:
        qg = jnp.concatenate(
            [qt_ref[0, (g * group + a) * HEAD_DIM:(g * group + a + 1) * HEAD_DIM, u * W:(u + 1) * W]
             for a in range(group)], axis=1)
        st.append(jnp.dot(keys(u, g), jnp.concatenate([qg, zpad], axis=0),
                          preferred_element_type=F32))
    p, sink_term = [], []
    for n, (u, g) in enumerate(chains):
        sg = jnp.where(valid[u], st[n], NEG_INF)
        sink = jnp.zeros((1, group * W), F32)
        for a in range(group):
            sink = jnp.where(seg == a, sink_ref[g * group + a], sink)
        m = jnp.maximum(jnp.max(sg, axis=0, keepdims=True), sink)
        p.append(jnp.exp(sg - m).astype(BF16))
        sink_term.append(jnp.exp(sink - m))
    acc = [jnp.dot(values(u, g), p[n], preferred_element_type=F32)
           for n, (u, g) in enumerate(chains)]
    for n, (u, g) in enumerate(chains):
        o = acc[n][0:HEAD_DIM] / (acc[n][HEAD_DIM:HEAD_DIM + 1] + sink_term[n])
        for a in range(0, group, 2):
            pair = jnp.concatenate([o[:, a * W:(a + 1) * W], o[:, (a + 1) * W:(a + 2) * W]], axis=0)
            l0 = (g * group + a) * HEAD_DIM
            o_ref[0, u * W:(u + 1) * W, l0:l0 + 2 * HEAD_DIM] = pair.T.astype(o_ref.dtype)


def _swa(qt, kpad, vt, sinks, n_q, n_kv):
    B, _, S = qt.shape
    W = WINDOW
    n_sub = SWA_SUB
    prev = lambda i: jnp.maximum(n_sub * i - 1, 0)
    return pl.pallas_call(
        functools.partial(_swa_kernel, n_kv=n_kv, group=n_q // n_kv),
        grid_spec=pltpu.PrefetchScalarGridSpec(
            num_scalar_prefetch=1,
            grid=(B, S // (n_sub * W)),
            in_specs=[
                pl.BlockSpec((1, n_q * HEAD_DIM, n_sub * W), lambda b, i, s: (b, 0, i)),
                pl.BlockSpec((1, W, n_kv * LANES), lambda b, i, s: (b, prev(i), 0)),
                pl.BlockSpec((1, n_sub * W, n_kv * LANES), lambda b, i, s: (b, i, 0)),
                pl.BlockSpec((1, n_kv * V_ROWS, W), lambda b, i, s: (b, 0, prev(i))),
                pl.BlockSpec((1, n_kv * V_ROWS, n_sub * W), lambda b, i, s: (b, 0, i)),
            ],
            out_specs=pl.BlockSpec((1, n_sub * W, n_q * HEAD_DIM), lambda b, i, s: (b, i, 0)),
        ),
        out_shape=jax.ShapeDtypeStruct((B, S, n_q * HEAD_DIM), BF16),
        compiler_params=_params(("arbitrary", "arbitrary")),
        name="swa",
    )(sinks, qt, kpad, kpad, vt, vt)


def _post_kernel(*refs, n_mix, final_norm):
    h_ref = refs[0]
    mix_refs = refs[1:1 + n_mix]
    (p_ref, wo_ref, gf_ref, wg_ref, wu_ref, wd_ref, gp_ref, wpg_ref, wpp_ref, gfin_ref,
     o_ref) = refs[1 + n_mix:]
    h = h_ref[...]
    off = 0
    for m_ref in mix_refs:
        w = m_ref.shape[1]
        h = h + jnp.dot(m_ref[...], wo_ref[off:off + w, :], preferred_element_type=F32)
        off += w
    hb = _rms(h, gf_ref[...]).astype(BF16)
    g = jnp.dot(hb, wg_ref[...], preferred_element_type=F32)
    u = jnp.dot(hb, wu_ref[...], preferred_element_type=F32)
    act = (g * jax.nn.sigmoid(g) * u).astype(BF16)
    h = h + jnp.dot(act, wd_ref[...], preferred_element_type=F32)
    gate = jax.nn.sigmoid(jnp.dot(_rms(h, gp_ref[...]).astype(BF16), wpg_ref[...],
                                  preferred_element_type=F32))
    h = h + gate * jnp.dot(p_ref[...].astype(BF16), wpp_ref[...], preferred_element_type=F32)
    if final_norm:
        h = _rms(h, gfin_ref[...])
    o_ref[...] = h


def _post(h, mixes, p_all, layer, wo, gf, wg, wu, wd, gp, wpg, wpp, gfin, tm, final_norm):
    T, D = h.shape
    row = lambda w: pl.BlockSpec((tm, w), lambda i: (i, 0))
    full = lambda a: pl.BlockSpec(a.shape, lambda i: (0, 0))
    lay = lambda a: pl.BlockSpec((None,) + a.shape[1:], lambda i: (layer, 0, 0),
                                 pipeline_mode=pl.Buffered(1))
    return pl.pallas_call(
        functools.partial(_post_kernel, n_mix=len(mixes), final_norm=final_norm),
        grid=(T // tm,),
        in_specs=[row(D)] + [row(m.shape[1]) for m in mixes]
        + [pl.BlockSpec((None, tm, p_all.shape[2]), lambda i: (layer, i, 0))]
        + [full(wo)] + [lay(a) for a in (gf, wg, wu, wd, gp, wpg, wpp)] + [full(gfin)],
        out_specs=row(D),
        out_shape=jax.ShapeDtypeStruct((T, D), F32),
        compiler_params=_params(("arbitrary",)),
        name="post",
    )(h, *mixes, p_all, wo, gf, wg, wu, wd, gp, wpg, wpp, gfin)


def _layer0_weights(w_in, b_f, n_fox):
    fw = n_fox * HEAD_DIM
    scale = HEAD_DIM ** -0.5 * LOG2E
    D = w_in.shape[0]
    qa, ka, va, qs, ks, vs = (w_in[:, i * fw:(i + 1) * fw] for i in range(6))
    wt = jnp.concatenate([qa * scale, va, qs * scale, vs], axis=1).T.astype(BF16)
    ka_pad = jnp.pad(ka.reshape(D, n_fox, HEAD_DIM), ((0, 0), (0, 0), (0, LANES - HEAD_DIM)))
    wk = jnp.concatenate([ka_pad.reshape(D, n_fox * LANES), ks], axis=1).astype(BF16)
    wf = jnp.pad(w_in[:, 6 * fw:], ((0, 0), (0, LANES - n_fox))).astype(BF16)
    bf = jnp.pad(b_f, (0, LANES - n_fox)).reshape(1, LANES)
    heads = jnp.arange(n_fox)
    pk = jnp.zeros((3 * LANES, n_fox * LANES), F32)
    aq = jnp.zeros((n_fox * FEAT_ROWS, 3 * LANES), F32)
    for piece in range(3):
        pk = pk.at[piece * LANES + heads, heads * LANES + HEAD_DIM + 3 + piece].set(-1.0)
        aq = aq.at[heads * FEAT_ROWS + piece, piece * LANES + heads].set(1.0)
    return wt, wk, wf, bf, pk.astype(BF16), aq.astype(BF16)


def kernel(x, p, positions, norm_mix, norm_ffn, norm_ple, norm_final, ev_w_in, ev_b_f, ev_w_out,
           od_w_in, od_sinks, od_w_out, ffn_w_gate, ffn_w_up, ffn_w_down, ple_w_proj, ple_w_gate):
    B, S, D = x.shape
    T = B * S
    n_heads = D // HEAD_DIM
    n_fox = n_heads // 2
    fox_w = n_fox * HEAD_DIM
    n_q, n_kv = n_heads, 4
    assert S % BLK == 0 and n_fox == 8
    row = lambda a: a.reshape(1, -1)

    wt, wk, wf, bf, pk, aq = _layer0_weights(ev_w_in[0], ev_b_f[0], n_fox)
    (qtf, qts, vtf, vts, kf, ks, feat, gt, qn, fb, kmsq) = _pre0(
        x, row(norm_mix[0]), wt, wk, wf, bf, pk, aq, n_fox)
    fb_heads = fb[:, :, 0, :n_fox].transpose(0, 2, 1)
    o_fox = _fox(fb_heads, qtf, feat, kf, vtf, gt, qn, kmsq)
    tri = (jnp.arange(BLK)[None, :] > jnp.arange(BLK)[:, None]).astype(BF16)
    o_sb = _sb(qts, ks, vts, jnp.concatenate([tri, tri], axis=1))

    tm = min(512, T)
    depth = norm_ffn.shape[0]
    p_all = p.reshape(depth, T, -1)
    stacked = (norm_ffn.reshape(depth, 1, D), ffn_w_gate.astype(BF16), ffn_w_up.astype(BF16),
               ffn_w_down.astype(BF16), norm_ple.reshape(depth, 1, D), ple_w_gate.astype(BF16),
               ple_w_proj.astype(BF16))
    h = _post(x.reshape(T, D), [o_fox.reshape(T, fox_w), o_sb.reshape(T, fox_w)], p_all, 0,
              ev_w_out[0].astype(BF16), *stacked, row(norm_final), tm, final_norm=False)

    qw = n_q * HEAD_DIM
    kw = n_kv * HEAD_DIM
    col_scale1 = jnp.ones((qw + 2 * kw,), F32).at[:qw].set(HEAD_DIM ** -0.5)
    w1t = (od_w_in[0] * col_scale1).T.astype(BF16)
    half = HEAD_DIM // 2
    inv = (ROPE_THETA ** (-jnp.arange(half, dtype=F32) / half)).reshape(half, 1)
    qt1, k1, vt1 = _pre1(h.reshape(B, S, D), row(norm_mix[1]), w1t, positions.reshape(B, 1, S),
                         inv, min(tm, S), n_q, n_kv)
    o_swa = _swa(qt1, k1, vt1, od_sinks[0], n_q, n_kv)
    out = _post(h, [o_swa.reshape(T, qw)], p_all, 1, od_w_out[0].astype(BF16), *stacked,
                row(norm_final), tm, final_norm=True)
    return out.reshape(B, S, D)
```

```python
import functools

import jax
import jax.numpy as jnp
from jax import lax
from jax.experimental import pallas as pl
from jax.experimental.pallas import tpu as pltpu

F32 = jnp.float32
BF16 = jnp.bfloat16

HEAD_DIM = 64
LANES = 128
BLK = 256
V_ROWS = 80
FEAT_ROWS = 16
WINDOW = 128
ROPE_THETA = 10000.0
EPS = 1e-6
NEG_INF = -1e30
LOG2E = 1.4426950408889634
SKIP_LOG2 = 60.0 * LOG2E
FOX_HEADS = 8
SB_HEADS = 4
SWA_SUB = 2
VMEM_LIMIT = 56 * 1024 * 1024

_NT = (((1,), (1,)), ((), ()))


def _params(sem):
    return pltpu.CompilerParams(dimension_semantics=sem, vmem_limit_bytes=VMEM_LIMIT)


def _rms(x, g):
    return x * lax.rsqrt(jnp.mean(x * x, axis=-1, keepdims=True) + EPS) * g


def _log_sigmoid(x):
    return jnp.minimum(x, 0.0) - jnp.log(1.0 + jnp.exp(-jnp.abs(x)))


def _split3(x):
    a = x.astype(BF16)
    r = x - a.astype(F32)
    b = r.astype(BF16)
    c = (r - b.astype(F32)).astype(BF16)
    return a, b, c


def _pre0_kernel(x_ref, g_ref, wt_ref, wk_ref, wf_ref, bf_ref, pk_ref, aq_ref,
                 qtf_ref, qts_ref, vtf_ref, vts_ref, kf_ref, ks_ref, feat_ref,
                 gt_ref, qn_ref, fb_ref, kmsq_ref, carry_ref, *, n_fox):
    tm = x_ref.shape[1]
    fw = n_fox * HEAD_DIM

    @pl.when(pl.program_id(1) == 0)
    def _():
        carry_ref[...] = jnp.zeros_like(carry_ref)

    hb = _rms(x_ref[0], g_ref[...]).astype(BF16)

    def tproj(c):
        return lax.dot_general(wt_ref[c * fw:(c + 1) * fw, :], hb, _NT,
                               preferred_element_type=F32).astype(BF16)

    qtf = tproj(0)
    qtf_ref[0] = qtf
    vtf = tproj(1)
    lane_row = lax.broadcasted_iota(jnp.int32, (V_ROWS - HEAD_DIM, tm), 0)
    ones_pad = jnp.where(lane_row == 0, 1.0, 0.0).astype(BF16)
    for h in range(n_fox):
        vtf_ref[0, h, 0, 0:HEAD_DIM, :] = vtf[h * HEAD_DIM:(h + 1) * HEAD_DIM, :]
        vtf_ref[0, h, 0, HEAD_DIM:V_ROWS, :] = ones_pad
    qts_ref[0] = tproj(2)
    vts = tproj(3)
    for h in range(n_fox):
        vts_ref[0, h, 0] = vts[h * HEAD_DIM:(h + 1) * HEAD_DIM, :]

    q32 = qtf.astype(F32)
    qn_rows = [jnp.sqrt(jnp.sum(jnp.square(q32[h * HEAD_DIM:(h + 1) * HEAD_DIM, :]),
                                axis=0, keepdims=True)) for h in range(n_fox)]
    qn_ref[0] = jnp.concatenate(qn_rows, axis=0)

    kk = jnp.dot(hb, wk_ref[...], preferred_element_type=F32)
    ks_ref[0] = kk[:, fw:].astype(BF16)
    kfox = kk[:, :fw].astype(BF16)
    k32 = kfox.astype(F32)
    lane = lax.broadcasted_iota(jnp.int32, (1, LANES), 1)
    first_head = lane < HEAD_DIM
    kmsq = jnp.zeros((1, LANES), F32)
    for pr in range(n_fox // 2):
        sq = jnp.square(k32[:, pr * LANES:(pr + 1) * LANES])
        for e in range(2):
            mine = first_head if e == 0 else jnp.logical_not(first_head)
            ss = jnp.sum(jnp.where(mine, sq, 0.0), axis=-1, keepdims=True)
            kmsq = jnp.where(lane == 2 * pr + e, jnp.max(ss, axis=0, keepdims=True), kmsq)
    kmsq_ref[0, 0] = kmsq

    gate = jnp.dot(hb, wf_ref[...], preferred_element_type=F32) + bf_ref[...]
    lf = _log_sigmoid(gate) * LOG2E
    row = lax.broadcasted_iota(jnp.int32, (tm, tm), 0)
    col = lax.broadcasted_iota(jnp.int32, (tm, tm), 1)
    tri = jnp.where(row >= col, 1.0, 0.0).astype(BF16)
    G = jnp.zeros((tm, LANES), F32)
    for piece in _split3(lf):
        G = G + jnp.dot(tri, piece, preferred_element_type=F32)
    fb_ref[0, 0] = carry_ref[...]
    carry_ref[...] = carry_ref[...] + G[tm - 1:tm, :]

    g_hi, g_mid, g_lo = _split3(G)
    lane_t = lax.broadcasted_iota(jnp.int32, (tm, LANES), 1)
    gp = jnp.where(lane_t < n_fox, g_hi, jnp.where(lane_t < 2 * n_fox, g_mid, g_lo))
    kfeat = jnp.dot(gp, pk_ref[...], preferred_element_type=F32)
    kones = jnp.where((lane % FEAT_ROWS < 3) & (lane < 2 * FEAT_ROWS), 1.0, 0.0)
    for pr in range(n_fox // 2):
        kf_ref[0, :, 2 * pr * LANES:(2 * pr + 1) * LANES] = kfox[:, pr * LANES:(pr + 1) * LANES]
        kf_ref[0, :, (2 * pr + 1) * LANES:(2 * pr + 2) * LANES] = (
            kfeat[:, pr * LANES:(pr + 1) * LANES] + kones).astype(BF16)

    GT = G.T
    gt_ref[0] = GT[0:8, :]
    t_hi, t_mid, t_lo = _split3(GT)
    row_t = lax.broadcasted_iota(jnp.int32, (LANES, tm), 0)
    gpt = jnp.where(row_t < n_fox, t_hi, jnp.where(row_t < 2 * n_fox, t_mid, t_lo))
    qfeat = jnp.dot(aq_ref[...], gpt, preferred_element_type=F32)
    frow = lax.broadcasted_iota(jnp.int32, (n_fox * FEAT_ROWS, 1), 0) % FEAT_ROWS
    qones = jnp.where((frow >= 3) & (frow < 6), 1.0, 0.0)
    feat_ref[0] = (qfeat + qones).astype(BF16)


def _pre0(x, g, wt, wk, wf, bf, pk, aq, n_fox):
    B, S, D = x.shape
    tm = BLK
    nb = S // tm
    fw = n_fox * HEAD_DIM
    const = lambda a: pl.BlockSpec(a.shape, lambda b, s: (0,) * a.ndim)
    tok_lane = lambda rows: pl.BlockSpec((1, rows, tm), lambda b, s: (b, 0, s))
    return pl.pallas_call(
        functools.partial(_pre0_kernel, n_fox=n_fox),
        grid=(B, nb),
        in_specs=[pl.BlockSpec((1, tm, D), lambda b, s: (b, s, 0))]
        + [const(a) for a in (g, wt, wk, wf, bf, pk, aq)],
        out_specs=[
            tok_lane(fw),
            tok_lane(fw),
            pl.BlockSpec((1, n_fox, 1, V_ROWS, tm), lambda b, s: (b, 0, s, 0, 0)),
            pl.BlockSpec((1, n_fox, 1, HEAD_DIM, tm), lambda b, s: (b, 0, s, 0, 0)),
            pl.BlockSpec((1, tm, n_fox * LANES), lambda b, s: (b, s, 0)),
            pl.BlockSpec((1, tm, fw), lambda b, s: (b, s, 0)),
            tok_lane(n_fox * FEAT_ROWS),
            tok_lane(8),
            tok_lane(8),
            pl.BlockSpec((1, 1, 1, LANES), lambda b, s: (b, s, 0, 0)),
            pl.BlockSpec((1, 1, 1, LANES), lambda b, s: (b, s, 0, 0)),
        ],
        out_shape=[
            jax.ShapeDtypeStruct((B, fw, S), BF16),
            jax.ShapeDtypeStruct((B, fw, S), BF16),
            jax.ShapeDtypeStruct((B, n_fox, nb, V_ROWS, tm), BF16),
            jax.ShapeDtypeStruct((B, n_fox, nb, HEAD_DIM, tm), BF16),
            jax.ShapeDtypeStruct((B, S, n_fox * LANES), BF16),
            jax.ShapeDtypeStruct((B, S, fw), BF16),
            jax.ShapeDtypeStruct((B, n_fox * FEAT_ROWS, S), BF16),
            jax.ShapeDtypeStruct((B, 8, S), F32),
            jax.ShapeDtypeStruct((B, 8, S), F32),
            jax.ShapeDtypeStruct((B, nb, 1, LANES), F32),
            jax.ShapeDtypeStruct((B, nb, 1, LANES), F32),
        ],
        scratch_shapes=[pltpu.VMEM((1, LANES), F32)],
        compiler_params=_params(("arbitrary", "arbitrary")),
        name="pre0",
    )(x, g, wt, wk, wf, bf, pk, aq)


def _fox_kernel(fb_ref, qt_ref, feat_ref, k_ref, vt_ref, gt_ref, qn_ref, kmsq_ref,
                o_ref, m_ref, acc_ref):
    b = pl.program_id(0)
    hp = pl.program_id(1)
    qi = pl.program_id(2)
    bq = qt_ref.shape[2]
    n_h = acc_ref.shape[0]
    heads = range(n_h)
    hg = [n_h * hp + h for h in heads]
    row = lax.broadcasted_iota(jnp.int32, (BLK, bq), 0)
    col = lax.broadcasted_iota(jnp.int32, (BLK, bq), 1)
    causal = row <= col
    kmax_sq = jnp.max(kmsq_ref[0], axis=0)
    lane = lax.broadcasted_iota(jnp.int32, (1, LANES), 1)

    def zeros(rows):
        return jnp.zeros((rows, bq), BF16)

    qaug = []
    for h in heads:
        q = qt_ref[0, h * HEAD_DIM:(h + 1) * HEAD_DIM, :]
        f = feat_ref[0, h * FEAT_ROWS:(h + 1) * FEAT_ROWS, :]
        parts = [q, zeros(HEAD_DIM), f, zeros(FEAT_ROWS)] if h % 2 == 0 else \
                [zeros(HEAD_DIM), q, zeros(FEAT_ROWS), f]
        qaug.append(jnp.concatenate(parts + [zeros(LANES - 2 * FEAT_ROWS)], axis=0))

    def kpair(k0, h):
        return k_ref[0, pl.ds(k0, BLK), (h // 2) * 2 * LANES:(h // 2 + 1) * 2 * LANES]

    jp = jnp.maximum(qi - 1, 0)
    kp0 = pl.multiple_of(jp * BLK, BLK)
    kd0 = pl.multiple_of(qi * BLK, BLK)
    st = [jnp.dot(jnp.concatenate([kpair(kp0, h), kpair(kd0, h)], axis=0),
                  qaug[h], preferred_element_type=F32) for h in heads]
    p_first = []
    for h in heads:
        cp = jnp.where(qi > 0, fb_ref[b, hg[h], qi] - fb_ref[b, hg[h], jp], NEG_INF)
        st_p = st[h][0:BLK]
        st_d = jnp.where(causal, st[h][BLK:2 * BLK], NEG_INF)
        m = jnp.maximum(jnp.max(st_d, axis=0, keepdims=True),
                        jnp.max(st_p, axis=0, keepdims=True) + cp)
        m_ref[h] = m
        p_first.append(jnp.concatenate([jnp.exp2(st_p - (m - cp)), jnp.exp2(st_d - m)],
                                       axis=0).astype(BF16))
    for h in heads:
        vt2 = jnp.concatenate([vt_ref[0, h, jp], vt_ref[0, h, qi]], axis=1)
        acc_ref[h] = jnp.dot(vt2, p_first[h], preferred_element_type=F32)

    def block(j):
        k0 = pl.multiple_of(j * BLK, BLK)
        st = [jnp.dot(kpair(k0, h), qaug[h], preferred_element_type=F32)
              for h in heads]
        p, alpha = [], []
        for h in heads:
            c = fb_ref[b, hg[h], qi] - fb_ref[b, hg[h], j]
            m_old = m_ref[h]
            m_new = jnp.maximum(m_old, jnp.max(st[h], axis=0, keepdims=True) + c)
            p.append(jnp.exp2(st[h] - (m_new - c)).astype(BF16))
            alpha.append(jnp.exp2(m_old - m_new))
            m_ref[h] = m_new
        for h in heads:
            acc_ref[h] = alpha[h] * acc_ref[h] + jnp.dot(vt_ref[0, h, j], p[h],
                                                         preferred_element_type=F32)

    th = []
    for h in heads:
        kmax = jnp.sqrt(jnp.sum(jnp.where(lane == hg[h], kmax_sq, 0.0), axis=-1, keepdims=True))
        bound = qn_ref[0, pl.ds(hg[h], 1), :] * kmax + gt_ref[0, pl.ds(hg[h], 1), :] - m_ref[h]
        th.append(jnp.max(bound) + SKIP_LOG2)

    def needed(j):
        jn = jnp.clip(j + 1, 0, qi)
        need = fb_ref[b, hg[0], jn] - fb_ref[b, hg[0], qi] <= th[0]
        for h in heads[1:]:
            need = need | (fb_ref[b, hg[h], jn] - fb_ref[b, hg[h], qi] <= th[h])
        return (j >= 0) & need

    def body(j):
        block(j)
        return j - 1

    lax.while_loop(needed, body, qi - 2)

    out_t = jnp.concatenate(
        [acc_ref[h, 0:HEAD_DIM, :] / acc_ref[h, HEAD_DIM:HEAD_DIM + 1, :] for h in heads], axis=0)
    o_ref[0] = out_t.T.astype(o_ref.dtype)


def _fox(fb, qt, feat, kf, vt, gt, qn, kmsq):
    B, _, S = qt.shape
    n_h = FOX_HEADS
    n_grp = qt.shape[1] // (n_h * HEAD_DIM)
    nb = S // BLK
    return pl.pallas_call(
        _fox_kernel,
        grid_spec=pltpu.PrefetchScalarGridSpec(
            num_scalar_prefetch=1,
            grid=(B, n_grp, nb),
            in_specs=[
                pl.BlockSpec((1, n_h * HEAD_DIM, BLK), lambda b, h, i, s: (b, h, i)),
                pl.BlockSpec((1, n_h * FEAT_ROWS, BLK), lambda b, h, i, s: (b, h, i)),
                pl.BlockSpec((1, S, n_h * LANES), lambda b, h, i, s: (b, 0, h),
                             pipeline_mode=pl.Buffered(1)),
                pl.BlockSpec((1, n_h, nb, V_ROWS, BLK), lambda b, h, i, s: (b, h, 0, 0, 0),
                             pipeline_mode=pl.Buffered(1)),
                pl.BlockSpec((1, 8, BLK), lambda b, h, i, s: (b, 0, i)),
                pl.BlockSpec((1, 8, BLK), lambda b, h, i, s: (b, 0, i)),
                pl.BlockSpec((1, nb, 1, LANES), lambda b, h, i, s: (b, 0, 0, 0)),
            ],
            out_specs=pl.BlockSpec((1, BLK, n_h * HEAD_DIM), lambda b, h, i, s: (b, i, h)),
            scratch_shapes=[
                pltpu.VMEM((n_h, 1, BLK), F32),
                pltpu.VMEM((n_h, V_ROWS, BLK), F32),
            ],
        ),
        out_shape=jax.ShapeDtypeStruct((B, S, n_grp * n_h * HEAD_DIM), BF16),
        compiler_params=_params(("arbitrary", "arbitrary", "arbitrary")),
        name="fox",
    )(fb, qt, feat, kf, vt, gt, qn, kmsq)


def _sb_kernel(qt_ref, k_ref, vt_ref, tri_ref, o_ref, r_ref, acc_ref):
    qi = pl.program_id(2)
    bq = qt_ref.shape[2]
    n_h = acc_ref.shape[0]
    heads = range(n_h)
    row = lax.broadcasted_iota(jnp.int32, (BLK, bq), 0)
    col = lax.broadcasted_iota(jnp.int32, (BLK, bq), 1)
    strict = row < col
    top = lax.broadcasted_iota(jnp.int32, (LANES, bq), 0) < HEAD_DIM
    zero = jnp.zeros((LANES, bq), BF16)
    qh = []
    for h in heads:
        pair = qt_ref[0, (h // 2) * LANES:(h // 2 + 1) * LANES, :]
        qh.append(jnp.where(top, pair, zero) if h % 2 == 0 else jnp.where(top, zero, pair))
    tri = tri_ref[...]

    def kpair(k0, h):
        return k_ref[0, pl.ds(k0, BLK), (h // 2) * LANES:(h // 2 + 1) * LANES]

    def log1m_beta(z):
        nz = -z
        return jnp.minimum(nz, 0.0) - jnp.log2(1.0 + jnp.exp2(jnp.minimum(z, nz)))

    def suffix(l1):
        return jnp.dot(tri, l1.astype(BF16), preferred_element_type=F32)

    jp = jnp.maximum(qi - 1, 0)
    kp0 = pl.multiple_of(jp * BLK, BLK)
    kd0 = pl.multiple_of(qi * BLK, BLK)
    has_prev = qi > 0
    cp = jnp.where(has_prev, 0.0, NEG_INF)
    z2 = [jnp.dot(jnp.concatenate([kpair(kp0, h), kpair(kd0, h)], axis=0), qh[h],
                  preferred_element_type=F32) for h in heads]
    l1p = [log1m_beta(z2[h][0:BLK]) for h in heads]
    l1d = [jnp.where(strict, log1m_beta(z2[h][BLK:2 * BLK]), 0.0) for h in heads]
    sfp = [suffix(l1p[h]) for h in heads]
    sfd = [suffix(l1d[h]) for h in heads]
    a_first = []
    for h in heads:
        tot_d = sfd[h][0:1, :] + l1d[h][0:1, :]
        tot_p = sfp[h][0:1, :] + l1p[h][0:1, :]
        a_d = jnp.where(strict, jnp.exp2(z2[h][BLK:2 * BLK] + l1d[h] + sfd[h]), 0.0)
        a_p = jnp.exp2(z2[h][0:BLK] + l1p[h] + sfp[h] + (tot_d + cp))
        a_first.append(jnp.concatenate([a_p, a_d], axis=0).astype(BF16))
        r_ref[h] = tot_d + jnp.where(has_prev, tot_p, 0.0)
    for h in heads:
        vt2 = jnp.concatenate([vt_ref[0, h, jp], vt_ref[0, h, qi]], axis=1)
        acc_ref[h] = jnp.dot(vt2, a_first[h], preferred_element_type=F32)

    def rmax():
        out = jnp.max(r_ref[0])
        for h in heads[1:]:
            out = jnp.maximum(out, jnp.max(r_ref[h]))
        return out

    def block(j):
        k0 = pl.multiple_of(j * BLK, BLK)
        z = [jnp.dot(kpair(k0, h), qh[h], preferred_element_type=F32) for h in heads]
        l1 = [log1m_beta(z[h]) for h in heads]
        sfx = [suffix(l1[h]) for h in heads]
        a = []
        for h in heads:
            r_old = r_ref[h]
            a.append(jnp.exp2(z[h] + l1[h] + sfx[h] + r_old).astype(BF16))
            r_ref[h] = r_old + (sfx[h][0:1, :] + l1[h][0:1, :])
        for h in heads:
            acc_ref[h] = acc_ref[h] + jnp.dot(vt_ref[0, h, j], a[h], preferred_element_type=F32)
        return rmax()

    def body(carry):
        j, _ = carry
        return j - 1, block(j)

    lax.while_loop(lambda c: (c[0] >= 0) & (c[1] > -SKIP_LOG2), body, (qi - 2, rmax()))

    o_ref[0] = jnp.concatenate([acc_ref[h] for h in heads], axis=0).T.astype(o_ref.dtype)


def _sb(qt, ks, vt, tri):
    B, _, S = qt.shape
    n_h = SB_HEADS
    n_grp = qt.shape[1] // (n_h * HEAD_DIM)
    nb = S // BLK
    return pl.pallas_call(
        _sb_kernel,
        grid=(B, n_grp, nb),
        in_specs=[
            pl.BlockSpec((1, n_h * HEAD_DIM, BLK), lambda b, h, i: (b, h, i)),
            pl.BlockSpec((1, S, n_h * HEAD_DIM), lambda b, h, i: (b, 0, h)),
            pl.BlockSpec((1, n_h, nb, HEAD_DIM, BLK), lambda b, h, i: (b, h, 0, 0, 0)),
            pl.BlockSpec((BLK, BLK), lambda b, h, i: (0, 0)),
        ],
        out_specs=pl.BlockSpec((1, BLK, n_h * HEAD_DIM), lambda b, h, i: (b, i, h)),
        out_shape=jax.ShapeDtypeStruct((B, S, n_grp * n_h * HEAD_DIM), BF16),
        scratch_shapes=[
            pltpu.VMEM((n_h, 1, BLK), F32),
            pltpu.VMEM((n_h, HEAD_DIM, BLK), F32),
        ],
        compiler_params=_params(("arbitrary", "arbitrary", "arbitrary")),
        name="sb",
    )(qt, ks, vt, tri)


def _pre1_kernel(x_ref, g_ref, wt_ref, pos_ref, inv_ref, qt_ref, k_ref, vt_ref, *, n_q, n_kv):
    tm = x_ref.shape[1]
    half = HEAD_DIM // 2
    hb = _rms(x_ref[0], g_ref[...]).astype(BF16)
    ang = inv_ref[...] * pos_ref[0].astype(F32)
    cos = jnp.cos(ang)
    sin = jnp.sin(ang)

    def proj_t(r0, rows):
        return lax.dot_general(wt_ref[r0:r0 + rows, :], hb, _NT, preferred_element_type=F32)

    def rope_t(x):
        x1, x2 = x[0:half], x[half:HEAD_DIM]
        return jnp.concatenate([x1 * cos - x2 * sin, x2 * cos + x1 * sin], axis=0)

    qw = n_q * HEAD_DIM
    for c in range(n_q // 4):
        qt = proj_t(c * 4 * HEAD_DIM, 4 * HEAD_DIM)
        for a in range(4):
            hq = c * 4 + a
            qt_ref[0, hq * HEAD_DIM:(hq + 1) * HEAD_DIM, :] = rope_t(
                qt[a * HEAD_DIM:(a + 1) * HEAD_DIM]).astype(BF16)
    kt = proj_t(qw, n_kv * HEAD_DIM)
    zpad = jnp.zeros((LANES - HEAD_DIM, tm), F32)
    for g in range(n_kv):
        kg = jnp.concatenate([rope_t(kt[g * HEAD_DIM:(g + 1) * HEAD_DIM]), zpad], axis=0)
        k_ref[0, :, g * LANES:(g + 1) * LANES] = kg.T.astype(BF16)
    vt = proj_t(qw + n_kv * HEAD_DIM, n_kv * HEAD_DIM).astype(BF16)
    pad_row = lax.broadcasted_iota(jnp.int32, (V_ROWS - HEAD_DIM, tm), 0)
    ones_pad = jnp.where(pad_row == 0, 1.0, 0.0).astype(BF16)
    for g in range(n_kv):
        vt_ref[0, g * V_ROWS:g * V_ROWS + HEAD_DIM, :] = vt[g * HEAD_DIM:(g + 1) * HEAD_DIM]
        vt_ref[0, g * V_ROWS + HEAD_DIM:(g + 1) * V_ROWS, :] = ones_pad


def _pre1(h, g, wt, pos, inv, tm, n_q, n_kv):
    B, S, D = h.shape
    return pl.pallas_call(
        functools.partial(_pre1_kernel, n_q=n_q, n_kv=n_kv),
        grid=(B, S // tm),
        in_specs=[
            pl.BlockSpec((1, tm, D), lambda b, i: (b, i, 0)),
            pl.BlockSpec((1, D), lambda b, i: (0, 0)),
            pl.BlockSpec(wt.shape, lambda b, i: (0, 0)),
            pl.BlockSpec((1, 1, tm), lambda b, i: (b, 0, i)),
            pl.BlockSpec(inv.shape, lambda b, i: (0, 0)),
        ],
        out_specs=[
            pl.BlockSpec((1, n_q * HEAD_DIM, tm), lambda b, i: (b, 0, i)),
            pl.BlockSpec((1, tm, n_kv * LANES), lambda b, i: (b, i, 0)),
            pl.BlockSpec((1, n_kv * V_ROWS, tm), lambda b, i: (b, 0, i)),
        ],
        out_shape=[
            jax.ShapeDtypeStruct((B, n_q * HEAD_DIM, S), BF16),
            jax.ShapeDtypeStruct((B, S, n_kv * LANES), BF16),
            jax.ShapeDtypeStruct((B, n_kv * V_ROWS, S), BF16),
        ],
        compiler_params=_params(("arbitrary", "arbitrary")),
        name="pre1",
    )(h, g, wt, pos, inv)


def _swa_kernel(sink_ref, qt_ref, kp_ref, ko_ref, vp_ref, vo_ref, o_ref, *, n_kv, group):
    i = pl.program_id(1)
    W = WINDOW
    n_sub = qt_ref.shape[2] // W
    r = lax.broadcasted_iota(jnp.int32, (2 * W, W), 0)
    c = lax.broadcasted_iota(jnp.int32, (2 * W, W), 1)
    rel = c + W - r
    band = (rel >= 0) & (rel < W)
    valid = [jnp.concatenate([band & ((r >= W) | (i > 0)) if u == 0 else band] * group, axis=1)
             for u in range(n_sub)]
    seg = lax.broadcasted_iota(jnp.int32, (1, group * W), 1) // W
    zpad = jnp.zeros((LANES - HEAD_DIM, group * W), BF16)
    chains = [(u, g) for u in range(n_sub) for g in range(n_kv)]

    def keys(u, g):
        ls = slice(g * LANES, (g + 1) * LANES)
        if u == 0:
            return jnp.concatenate([kp_ref[0, :, ls], ko_ref[0, 0:W, ls]], axis=0)
        return ko_ref[0, (u - 1) * W:(u + 1) * W, ls]

    def values(u, g):
        rs = slice(g * V_ROWS, (g + 1) * V_ROWS)
        if u == 0:
            return jnp.concatenate([vp_ref[0, rs, :], vo_ref[0, rs, 0:W]], axis=1)
        return vo_ref[0, rs, (u - 1) * W:(u + 1) * W]

    st = []
    for u, g in chains:
        qg = jnp.concatenate(
            [qt_ref[0, (g * group + a) * HEAD_DIM:(g * group + a + 1) * HEAD_DIM, u * W:(u + 1) * W]
             for a in range(group)], axis=1)
        st.append(jnp.dot(keys(u, g), jnp.concatenate([qg, zpad], axis=0),
                          preferred_element_type=F32))
    p, sink_term = [], []
    for n, (u, g) in enumerate(chains):
        sg = jnp.where(valid[u], st[n], NEG_INF)
        sink = jnp.zeros((1, group * W), F32)
        for a in range(group):
            sink = jnp.where(seg == a, sink_ref[g * group + a] * LOG2E, sink)
        m = jnp.maximum(jnp.max(sg, axis=0, keepdims=True), sink)
        p.append(jnp.exp2(sg - m).astype(BF16))
        sink_term.append(jnp.exp2(sink - m))
    acc = [jnp.dot(values(u, g), p[n], preferred_element_type=F32)
           for n, (u, g) in enumerate(chains)]
    for n, (u, g) in enumerate(chains):
        o = acc[n][0:HEAD_DIM] / (acc[n][HEAD_DIM:HEAD_DIM + 1] + sink_term[n])
        for a in range(0, group, 2):
            pair = jnp.concatenate([o[:, a * W:(a + 1) * W], o[:, (a + 1) * W:(a + 2) * W]], axis=0)
            l0 = (g * group + a) * HEAD_DIM
            o_ref[0, u * W:(u + 1) * W, l0:l0 + 2 * HEAD_DIM] = pair.T.astype(o_ref.dtype)


def _swa(qt, kpad, vt, sinks, n_q, n_kv):
    B, _, S = qt.shape
    W = WINDOW
    n_sub = SWA_SUB
    prev = lambda i: jnp.maximum(n_sub * i - 1, 0)
    return pl.pallas_call(
        functools.partial(_swa_kernel, n_kv=n_kv, group=n_q // n_kv),
        grid_spec=pltpu.PrefetchScalarGridSpec(
            num_scalar_prefetch=1,
            grid=(B, S // (n_sub * W)),
            in_specs=[
                pl.BlockSpec((1, n_q * HEAD_DIM, n_sub * W), lambda b, i, s: (b, 0, i)),
                pl.BlockSpec((1, W, n_kv * LANES), lambda b, i, s: (b, prev(i), 0)),
                pl.BlockSpec((1, n_sub * W, n_kv * LANES), lambda b, i, s: (b, i, 0)),
                pl.BlockSpec((1, n_kv * V_ROWS, W), lambda b, i, s: (b, 0, prev(i))),
                pl.BlockSpec((1, n_kv * V_ROWS, n_sub * W), lambda b, i, s: (b, 0, i)),
            ],
            out_specs=pl.BlockSpec((1, n_sub * W, n_q * HEAD_DIM), lambda b, i, s: (b, i, 0)),
        ),
        out_shape=jax.ShapeDtypeStruct((B, S, n_q * HEAD_DIM), BF16),
        compiler_params=_params(("arbitrary", "arbitrary")),
        name="swa",
    )(sinks, qt, kpad, kpad, vt, vt)


def _post_kernel(*refs, n_mix, final_norm):
    h_ref = refs[0]
    mix_refs = refs[1:1 + n_mix]
    (p_ref, wo_ref, gf_ref, wg_ref, wu_ref, wd_ref, gp_ref, wpg_ref, wpp_ref, gfin_ref,
     o_ref) = refs[1 + n_mix:]
    h = h_ref[...]
    off = 0
    for m_ref in mix_refs:
        w = m_ref.shape[1]
        h = h + jnp.dot(m_ref[...], wo_ref[off:off + w, :], preferred_element_type=F32)
        off += w
    hb = _rms(h, gf_ref[...]).astype(BF16)
    g = jnp.dot(hb, wg_ref[...], preferred_element_type=F32)
    u = jnp.dot(hb, wu_ref[...], preferred_element_type=F32)
    act = (g * jax.nn.sigmoid(g) * u).astype(BF16)
    h = h + jnp.dot(act, wd_ref[...], preferred_element_type=F32)
    gate = jax.nn.sigmoid(jnp.dot(_rms(h, gp_ref[...]).astype(BF16), wpg_ref[...],
                                  preferred_element_type=F32))
    h = h + gate * jnp.dot(p_ref[...].astype(BF16), wpp_ref[...], preferred_element_type=F32)
    if final_norm:
        h = _rms(h, gfin_ref[...])
    o_ref[...] = h


def _post(h, mixes, p_all, layer, wo, gf, wg, wu, wd, gp, wpg, wpp, gfin, tm, final_norm):
    T, D = h.shape
    row = lambda w: pl.BlockSpec((tm, w), lambda i: (i, 0))
    full = lambda a: pl.BlockSpec(a.shape, lambda i: (0, 0))
    lay = lambda a: pl.BlockSpec((None,) + a.shape[1:], lambda i: (layer, 0, 0),
                                 pipeline_mode=pl.Buffered(1))
    return pl.pallas_call(
        functools.partial(_post_kernel, n_mix=len(mixes), final_norm=final_norm),
        grid=(T // tm,),
        in_specs=[row(D)] + [row(m.shape[1]) for m in mixes]
        + [pl.BlockSpec((None, tm, p_all.shape[2]), lambda i: (layer, i, 0))]
        + [full(wo)] + [lay(a) for a in (gf, wg, wu, wd, gp, wpg, wpp)] + [full(gfin)],
        out_specs=row(D),
        out_shape=jax.ShapeDtypeStruct((T, D), F32),
        compiler_params=_params(("arbitrary",)),
        name="post",
    )(h, *mixes, p_all, wo, gf, wg, wu, wd, gp, wpg, wpp, gfin)


def _layer0_weights(w_in, b_f, n_fox):
    fw = n_fox * HEAD_DIM
    scale = HEAD_DIM ** -0.5 * LOG2E
    D = w_in.shape[0]
    qa, ka, va, qs, ks, vs = (w_in[:, i * fw:(i + 1) * fw] for i in range(6))
    wt = jnp.concatenate([qa * scale, va, qs * scale, vs], axis=1).T.astype(BF16)
    wk = jnp.concatenate([ka, ks], axis=1).astype(BF16)
    gate_pad = ((0, 0), (0, LANES - 3 * n_fox))
    wf = jnp.pad(jnp.tile(w_in[:, 6 * fw:], (1, 3)), gate_pad).astype(BF16)
    bf = jnp.pad(jnp.tile(b_f.reshape(1, n_fox), (1, 3)), gate_pad)
    heads = jnp.arange(n_fox)
    pk = jnp.zeros((LANES, n_fox // 2 * LANES), F32)
    aq = jnp.zeros((n_fox * FEAT_ROWS, LANES), F32)
    for piece in range(3):
        pk = pk.at[piece * n_fox + heads,
                   heads // 2 * LANES + heads % 2 * FEAT_ROWS + 3 + piece].set(-1.0)
        aq = aq.at[heads * FEAT_ROWS + piece, piece * n_fox + heads].set(1.0)
    return wt, wk, wf, bf, pk.astype(BF16), aq.astype(BF16)


def kernel(x, p, positions, norm_mix, norm_ffn, norm_ple, norm_final, ev_w_in, ev_b_f, ev_w_out,
           od_w_in, od_sinks, od_w_out, ffn_w_gate, ffn_w_up, ffn_w_down, ple_w_proj, ple_w_gate):
    B, S, D = x.shape
    T = B * S
    n_heads = D // HEAD_DIM
    n_fox = n_heads // 2
    fox_w = n_fox * HEAD_DIM
    n_q, n_kv = n_heads, 4
    assert S % BLK == 0 and n_fox == 8
    row = lambda a: a.reshape(1, -1)

    wt, wk, wf, bf, pk, aq = _layer0_weights(ev_w_in[0], ev_b_f[0], n_fox)
    (qtf, qts, vtf, vts, kf, ks, feat, gt, qn, fb, kmsq) = _pre0(
        x, row(norm_mix[0]), wt, wk, wf, bf, pk, aq, n_fox)
    fb_heads = fb[:, :, 0, :n_fox].transpose(0, 2, 1)
    o_fox = _fox(fb_heads, qtf, feat, kf, vtf, gt, qn, kmsq)
    tri = (jnp.arange(BLK)[None, :] > jnp.arange(BLK)[:, None]).astype(BF16)
    o_sb = _sb(qts, ks, vts, tri)

    tm = min(512, T)
    depth = norm_ffn.shape[0]
    p_all = p.reshape(depth, T, -1)
    stacked = (norm_ffn.reshape(depth, 1, D), ffn_w_gate.astype(BF16), ffn_w_up.astype(BF16),
               ffn_w_down.astype(BF16), norm_ple.reshape(depth, 1, D), ple_w_gate.astype(BF16),
               ple_w_proj.astype(BF16))
    h = _post(x.reshape(T, D), [o_fox.reshape(T, fox_w), o_sb.reshape(T, fox_w)], p_all, 0,
              ev_w_out[0].astype(BF16), *stacked, row(norm_final), tm, final_norm=False)

    qw = n_q * HEAD_DIM
    kw = n_kv * HEAD_DIM
    col_scale1 = jnp.ones((qw + 2 * kw,), F32).at[:qw].set(HEAD_DIM ** -0.5 * LOG2E)
    w1t = (od_w_in[0] * col_scale1).T.astype(BF16)
    half = HEAD_DIM // 2
    inv = (ROPE_THETA ** (-jnp.arange(half, dtype=F32) / half)).reshape(half, 1)
    qt1, k1, vt1 = _pre1(h.reshape(B, S, D), row(norm_mix[1]), w1t, positions.reshape(B, 1, S),
                         inv, min(tm, S), n_q, n_kv)
    o_swa = _swa(qt1, k1, vt1, od_sinks[0], n_q, n_kv)
    out = _post(h, [o_swa.reshape(T, qw)], p_all, 1, od_w_out[0].astype(BF16), *stacked,
                row(norm_final), tm, final_norm=True)
    return out.reshape(B, S, D)
```

```python
import functools

import jax
import jax.numpy as jnp
from jax import lax
from jax.experimental import pallas as pl
from jax.experimental.pallas import tpu as pltpu

F32 = jnp.float32
BF16 = jnp.bfloat16

HEAD_DIM = 64
LANES = 128
BLK = 256
V_ROWS = 80
FEAT_ROWS = 16
WINDOW = 128
ROPE_THETA = 10000.0
EPS = 1e-6
NEG_INF = -1e30
LOG2E = 1.4426950408889634
SKIP_LOG2 = 60.0 * LOG2E
FOX_HEADS = 8
SB_HEADS = 8
SWA_SUB = 4
VMEM_LIMIT = 56 * 1024 * 1024

_NT = (((1,), (1,)), ((), ()))


def _params(sem):
    return pltpu.CompilerParams(dimension_semantics=sem, vmem_limit_bytes=VMEM_LIMIT)


def _rms(x, g):
    return x * lax.rsqrt(jnp.mean(x * x, axis=-1, keepdims=True) + EPS) * g


def _log_sigmoid(x):
    return jnp.minimum(x, 0.0) - jnp.log(1.0 + jnp.exp(-jnp.abs(x)))


def _split3(x):
    a = x.astype(BF16)
    r = x - a.astype(F32)
    b = r.astype(BF16)
    c = (r - b.astype(F32)).astype(BF16)
    return a, b, c


def _pre0_kernel(x_ref, g_ref, wt_ref, wk_ref, wf_ref, bf_ref, pk_ref, aq_ref,
                 qtf_ref, qts_ref, vtf_ref, vts_ref, kf_ref, ks_ref, feat_ref,
                 gt_ref, qn_ref, fb_ref, kmsq_ref, carry_ref, *, n_fox):
    tm = x_ref.shape[1]
    fw = n_fox * HEAD_DIM

    @pl.when(pl.program_id(1) == 0)
    def _():
        carry_ref[...] = jnp.zeros_like(carry_ref)

    hb = _rms(x_ref[0], g_ref[...]).astype(BF16)

    gate = jnp.dot(hb, wf_ref[...], preferred_element_type=F32) + bf_ref[...]
    lf = _log_sigmoid(gate) * LOG2E
    row = lax.broadcasted_iota(jnp.int32, (tm, tm), 0)
    col = lax.broadcasted_iota(jnp.int32, (tm, tm), 1)
    tri = jnp.where(row >= col, 1.0, 0.0).astype(BF16)
    G = jnp.zeros((tm, LANES), F32)
    for piece in _split3(lf):
        G = G + jnp.dot(tri, piece, preferred_element_type=F32)
    fb_ref[0, 0] = carry_ref[...]
    carry_ref[...] = carry_ref[...] + G[tm - 1:tm, :]

    g_hi, g_mid, g_lo = _split3(G)
    lane_t = lax.broadcasted_iota(jnp.int32, (tm, LANES), 1)
    gp = jnp.where(lane_t < n_fox, g_hi, jnp.where(lane_t < 2 * n_fox, g_mid, g_lo))
    kfeat = jnp.dot(gp, pk_ref[...], preferred_element_type=F32)

    GT = G.T
    gt_ref[0] = GT[0:8, :]
    t_hi, t_mid, t_lo = _split3(GT)
    row_t = lax.broadcasted_iota(jnp.int32, (LANES, tm), 0)
    gpt = jnp.where(row_t < n_fox, t_hi, jnp.where(row_t < 2 * n_fox, t_mid, t_lo))
    qfeat = jnp.dot(aq_ref[...], gpt, preferred_element_type=F32)
    frow = lax.broadcasted_iota(jnp.int32, (n_fox * FEAT_ROWS, 1), 0) % FEAT_ROWS
    qones = jnp.where((frow >= 3) & (frow < 6), 1.0, 0.0)
    feat_ref[0] = (qfeat + qones).astype(BF16)

    def tproj(c):
        return lax.dot_general(wt_ref[c * fw:(c + 1) * fw, :], hb, _NT,
                               preferred_element_type=F32).astype(BF16)

    qtf = tproj(0)
    qtf_ref[0] = qtf
    vtf = tproj(1)
    lane_row = lax.broadcasted_iota(jnp.int32, (V_ROWS - HEAD_DIM, tm), 0)
    ones_pad = jnp.where(lane_row == 0, 1.0, 0.0).astype(BF16)
    for h in range(n_fox):
        vtf_ref[0, h, 0, 0:HEAD_DIM, :] = vtf[h * HEAD_DIM:(h + 1) * HEAD_DIM, :]
        vtf_ref[0, h, 0, HEAD_DIM:V_ROWS, :] = ones_pad
    qts_ref[0] = tproj(2)
    vts = tproj(3)
    for h in range(n_fox):
        vts_ref[0, h, 0] = vts[h * HEAD_DIM:(h + 1) * HEAD_DIM, :]

    q32 = qtf.astype(F32)
    qn_rows = [jnp.sqrt(jnp.sum(jnp.square(q32[h * HEAD_DIM:(h + 1) * HEAD_DIM, :]),
                                axis=0, keepdims=True)) for h in range(n_fox)]
    qn_ref[0] = jnp.concatenate(qn_rows, axis=0)

    kk = jnp.dot(hb, wk_ref[...], preferred_element_type=F32)
    ks_ref[0] = kk[:, fw:].astype(BF16)
    kfox = kk[:, :fw].astype(BF16)
    k32 = kfox.astype(F32)
    lane = lax.broadcasted_iota(jnp.int32, (1, LANES), 1)
    first_head = lane < HEAD_DIM
    kmsq = jnp.zeros((1, LANES), F32)
    for pr in range(n_fox // 2):
        sq = jnp.square(k32[:, pr * LANES:(pr + 1) * LANES])
        for e in range(2):
            mine = first_head if e == 0 else jnp.logical_not(first_head)
            ss = jnp.sum(jnp.where(mine, sq, 0.0), axis=-1, keepdims=True)
            kmsq = jnp.where(lane == 2 * pr + e, jnp.max(ss, axis=0, keepdims=True), kmsq)
    kmsq_ref[0, 0] = kmsq

    kones = jnp.where((lane % FEAT_ROWS < 3) & (lane < 2 * FEAT_ROWS), 1.0, 0.0)
    for pr in range(n_fox // 2):
        kf_ref[0, :, 2 * pr * LANES:(2 * pr + 1) * LANES] = kfox[:, pr * LANES:(pr + 1) * LANES]
        kf_ref[0, :, (2 * pr + 1) * LANES:(2 * pr + 2) * LANES] = (
            kfeat[:, pr * LANES:(pr + 1) * LANES] + kones).astype(BF16)


def _pre0(x, g, wt, wk, wf, bf, pk, aq, n_fox):
    B, S, D = x.shape
    tm = BLK
    nb = S // tm
    fw = n_fox * HEAD_DIM
    const = lambda a: pl.BlockSpec(a.shape, lambda b, s: (0,) * a.ndim)
    tok_lane = lambda rows: pl.BlockSpec((1, rows, tm), lambda b, s: (b, 0, s))
    return pl.pallas_call(
        functools.partial(_pre0_kernel, n_fox=n_fox),
        grid=(B, nb),
        in_specs=[pl.BlockSpec((1, tm, D), lambda b, s: (b, s, 0))]
        + [const(a) for a in (g, wt, wk, wf, bf, pk, aq)],
        out_specs=[
            tok_lane(fw),
            tok_lane(fw),
            pl.BlockSpec((1, n_fox, 1, V_ROWS, tm), lambda b, s: (b, 0, s, 0, 0)),
            pl.BlockSpec((1, n_fox, 1, HEAD_DIM, tm), lambda b, s: (b, 0, s, 0, 0)),
            pl.BlockSpec((1, tm, n_fox * LANES), lambda b, s: (b, s, 0)),
            pl.BlockSpec((1, tm, fw), lambda b, s: (b, s, 0)),
            tok_lane(n_fox * FEAT_ROWS),
            tok_lane(8),
            tok_lane(8),
            pl.BlockSpec((1, 1, 1, LANES), lambda b, s: (b, s, 0, 0)),
            pl.BlockSpec((1, 1, 1, LANES), lambda b, s: (b, s, 0, 0)),
        ],
        out_shape=[
            jax.ShapeDtypeStruct((B, fw, S), BF16),
            jax.ShapeDtypeStruct((B, fw, S), BF16),
            jax.ShapeDtypeStruct((B, n_fox, nb, V_ROWS, tm), BF16),
            jax.ShapeDtypeStruct((B, n_fox, nb, HEAD_DIM, tm), BF16),
            jax.ShapeDtypeStruct((B, S, n_fox * LANES), BF16),
            jax.ShapeDtypeStruct((B, S, fw), BF16),
            jax.ShapeDtypeStruct((B, n_fox * FEAT_ROWS, S), BF16),
            jax.ShapeDtypeStruct((B, 8, S), F32),
            jax.ShapeDtypeStruct((B, 8, S), F32),
            jax.ShapeDtypeStruct((B, nb, 1, LANES), F32),
            jax.ShapeDtypeStruct((B, nb, 1, LANES), F32),
        ],
        scratch_shapes=[pltpu.VMEM((1, LANES), F32)],
        compiler_params=_params(("arbitrary", "arbitrary")),
        name="pre0",
    )(x, g, wt, wk, wf, bf, pk, aq)


def _fox_kernel(fb_ref, qt_ref, feat_ref, k_ref, vt_ref, gt_ref, qn_ref, kmsq_ref,
                o_ref, m_ref, acc_ref):
    b = pl.program_id(0)
    hp = pl.program_id(1)
    qi = pl.program_id(2)
    bq = qt_ref.shape[2]
    n_h = acc_ref.shape[0]
    heads = range(n_h)
    hg = [n_h * hp + h for h in heads]
    row = lax.broadcasted_iota(jnp.int32, (BLK, bq), 0)
    col = lax.broadcasted_iota(jnp.int32, (BLK, bq), 1)
    causal = row <= col
    kmax_sq = jnp.max(kmsq_ref[0], axis=0)
    lane = lax.broadcasted_iota(jnp.int32, (1, LANES), 1)

    def zeros(rows):
        return jnp.zeros((rows, bq), BF16)

    qaug = []
    for h in heads:
        q = qt_ref[0, h * HEAD_DIM:(h + 1) * HEAD_DIM, :]
        f = feat_ref[0, h * FEAT_ROWS:(h + 1) * FEAT_ROWS, :]
        parts = [q, zeros(HEAD_DIM), f, zeros(FEAT_ROWS)] if h % 2 == 0 else \
                [zeros(HEAD_DIM), q, zeros(FEAT_ROWS), f]
        qaug.append(jnp.concatenate(parts + [zeros(LANES - 2 * FEAT_ROWS)], axis=0))

    def kpair(k0, h):
        return k_ref[0, pl.ds(k0, BLK), (h // 2) * 2 * LANES:(h // 2 + 1) * 2 * LANES]

    jp = jnp.maximum(qi - 1, 0)
    kp0 = pl.multiple_of(jp * BLK, BLK)
    kd0 = pl.multiple_of(qi * BLK, BLK)
    st = [jnp.dot(jnp.concatenate([kpair(kp0, h), kpair(kd0, h)], axis=0),
                  qaug[h], preferred_element_type=F32) for h in heads]
    p_first = []
    for h in heads:
        cp = jnp.where(qi > 0, fb_ref[b, hg[h], qi] - fb_ref[b, hg[h], jp], NEG_INF)
        st_p = st[h][0:BLK]
        st_d = jnp.where(causal, st[h][BLK:2 * BLK], NEG_INF)
        m = jnp.maximum(jnp.max(st_d, axis=0, keepdims=True),
                        jnp.max(st_p, axis=0, keepdims=True) + cp)
        m_ref[h] = m
        p_first.append(jnp.concatenate([jnp.exp2(st_p - (m - cp)), jnp.exp2(st_d - m)],
                                       axis=0).astype(BF16))
    for h in heads:
        vt2 = jnp.concatenate([vt_ref[0, h, jp], vt_ref[0, h, qi]], axis=1)
        acc_ref[h] = jnp.dot(vt2, p_first[h], preferred_element_type=F32)

    def block(j):
        k0 = pl.multiple_of(j * BLK, BLK)
        st = [jnp.dot(kpair(k0, h), qaug[h], preferred_element_type=F32)
              for h in heads]
        p, alpha = [], []
        for h in heads:
            c = fb_ref[b, hg[h], qi] - fb_ref[b, hg[h], j]
            m_old = m_ref[h]
            m_new = jnp.maximum(m_old, jnp.max(st[h], axis=0, keepdims=True) + c)
            p.append(jnp.exp2(st[h] - (m_new - c)).astype(BF16))
            alpha.append(jnp.exp2(m_old - m_new))
            m_ref[h] = m_new
        for h in heads:
            acc_ref[h] = alpha[h] * acc_ref[h] + jnp.dot(vt_ref[0, h, j], p[h],
                                                         preferred_element_type=F32)

    th = []
    for h in heads:
        kmax = jnp.sqrt(jnp.sum(jnp.where(lane == hg[h], kmax_sq, 0.0), axis=-1, keepdims=True))
        bound = qn_ref[0, pl.ds(hg[h], 1), :] * kmax + gt_ref[0, pl.ds(hg[h], 1), :] - m_ref[h]
        th.append(jnp.max(bound) + SKIP_LOG2)

    def needed(j):
        jn = jnp.clip(j + 1, 0, qi)
        need = fb_ref[b, hg[0], jn] - fb_ref[b, hg[0], qi] <= th[0]
        for h in heads[1:]:
            need = need | (fb_ref[b, hg[h], jn] - fb_ref[b, hg[h], qi] <= th[h])
        return (j >= 0) & need

    def body(j):
        block(j)
        return j - 1

    lax.while_loop(needed, body, qi - 2)

    out_t = jnp.concatenate(
        [acc_ref[h, 0:HEAD_DIM, :] / acc_ref[h, HEAD_DIM:HEAD_DIM + 1, :] for h in heads], axis=0)
    o_ref[0] = out_t.T.astype(o_ref.dtype)


def _fox(fb, qt, feat, kf, vt, gt, qn, kmsq):
    B, _, S = qt.shape
    n_h = FOX_HEADS
    n_grp = qt.shape[1] // (n_h * HEAD_DIM)
    nb = S // BLK
    return pl.pallas_call(
        _fox_kernel,
        grid_spec=pltpu.PrefetchScalarGridSpec(
            num_scalar_prefetch=1,
            grid=(B, n_grp, nb),
            in_specs=[
                pl.BlockSpec((1, n_h * HEAD_DIM, BLK), lambda b, h, i, s: (b, h, i)),
                pl.BlockSpec((1, n_h * FEAT_ROWS, BLK), lambda b, h, i, s: (b, h, i)),
                pl.BlockSpec((1, S, n_h * LANES), lambda b, h, i, s: (b, 0, h),
                             pipeline_mode=pl.Buffered(1)),
                pl.BlockSpec((1, n_h, nb, V_ROWS, BLK), lambda b, h, i, s: (b, h, 0, 0, 0),
                             pipeline_mode=pl.Buffered(1)),
                pl.BlockSpec((1, 8, BLK), lambda b, h, i, s: (b, 0, i)),
                pl.BlockSpec((1, 8, BLK), lambda b, h, i, s: (b, 0, i)),
                pl.BlockSpec((1, nb, 1, LANES), lambda b, h, i, s: (b, 0, 0, 0)),
            ],
            out_specs=pl.BlockSpec((1, BLK, n_h * HEAD_DIM), lambda b, h, i, s: (b, i, h)),
            scratch_shapes=[
                pltpu.VMEM((n_h, 1, BLK), F32),
                pltpu.VMEM((n_h, V_ROWS, BLK), F32),
            ],
        ),
        out_shape=jax.ShapeDtypeStruct((B, S, n_grp * n_h * HEAD_DIM), BF16),
        compiler_params=_params(("arbitrary", "arbitrary", "arbitrary")),
        name="fox",
    )(fb, qt, feat, kf, vt, gt, qn, kmsq)


def _sb_kernel(qt_ref, k_ref, vt_ref, tri_ref, o_ref, r_ref, acc_ref):
    qi = pl.program_id(2)
    bq = qt_ref.shape[2]
    n_h = acc_ref.shape[0]
    heads = range(n_h)
    row = lax.broadcasted_iota(jnp.int32, (BLK, bq), 0)
    col = lax.broadcasted_iota(jnp.int32, (BLK, bq), 1)
    strict = row < col
    top = lax.broadcasted_iota(jnp.int32, (LANES, bq), 0) < HEAD_DIM
    zero = jnp.zeros((LANES, bq), BF16)
    qh = []
    for h in heads:
        pair = qt_ref[0, (h // 2) * LANES:(h // 2 + 1) * LANES, :]
        qh.append(jnp.where(top, pair, zero) if h % 2 == 0 else jnp.where(top, zero, pair))
    tri = tri_ref[...]

    def kpair(k0, h):
        return k_ref[0, pl.ds(k0, BLK), (h // 2) * LANES:(h // 2 + 1) * LANES]

    def stick_logs(z, mask=None):
        nz = -z
        l1 = jnp.minimum(nz, 0.0) - jnp.log2(1.0 + jnp.exp2(jnp.minimum(z, nz)))
        lb = z + l1
        if mask is not None:
            l1 = jnp.where(mask, l1, 0.0)
        return lb, l1.astype(BF16), l1[0:1, :]

    def suffix(l1b):
        return jnp.dot(tri, l1b, preferred_element_type=F32)

    jp = jnp.maximum(qi - 1, 0)
    kp0 = pl.multiple_of(jp * BLK, BLK)
    kd0 = pl.multiple_of(qi * BLK, BLK)
    has_prev = qi > 0
    cp = jnp.where(has_prev, 0.0, NEG_INF)
    z2 = [jnp.dot(jnp.concatenate([kpair(kp0, h), kpair(kd0, h)], axis=0), qh[h],
                  preferred_element_type=F32) for h in heads]
    lp = [stick_logs(z2[h][0:BLK]) for h in heads]
    ld = [stick_logs(z2[h][BLK:2 * BLK], strict) for h in heads]
    sfp = [suffix(lp[h][1]) for h in heads]
    sfd = [suffix(ld[h][1]) for h in heads]
    a_first = []
    for h in heads:
        tot_d = sfd[h][0:1, :] + ld[h][2]
        tot_p = sfp[h][0:1, :] + lp[h][2]
        a_d = jnp.where(strict, jnp.exp2(ld[h][0] + sfd[h]), 0.0)
        a_p = jnp.exp2(lp[h][0] + sfp[h] + (tot_d + cp))
        a_first.append(jnp.concatenate([a_p, a_d], axis=0).astype(BF16))
        r_ref[h] = tot_d + jnp.where(has_prev, tot_p, 0.0)
    for h in heads:
        vt2 = jnp.concatenate([vt_ref[0, h, jp], vt_ref[0, h, qi]], axis=1)
        acc_ref[h] = jnp.dot(vt2, a_first[h], preferred_element_type=F32)

    def rmax():
        out = jnp.max(r_ref[0])
        for h in heads[1:]:
            out = jnp.maximum(out, jnp.max(r_ref[h]))
        return out

    def block(j):
        k0 = pl.multiple_of(j * BLK, BLK)
        z = [jnp.dot(kpair(k0, h), qh[h], preferred_element_type=F32) for h in heads]
        lg = [stick_logs(z[h]) for h in heads]
        sfx = [suffix(lg[h][1]) for h in heads]
        a = []
        for h in heads:
            r_old = r_ref[h]
            a.append(jnp.exp2(lg[h][0] + sfx[h] + r_old).astype(BF16))
            r_ref[h] = r_old + (sfx[h][0:1, :] + lg[h][2])
        for h in heads:
            acc_ref[h] = acc_ref[h] + jnp.dot(vt_ref[0, h, j], a[h], preferred_element_type=F32)
        return rmax()

    def body(carry):
        j, _ = carry
        return j - 1, block(j)

    lax.while_loop(lambda c: (c[0] >= 0) & (c[1] > -SKIP_LOG2), body, (qi - 2, rmax()))

    o_ref[0] = jnp.concatenate([acc_ref[h] for h in heads], axis=0).T.astype(o_ref.dtype)


def _sb(qt, ks, vt, tri):
    B, _, S = qt.shape
    n_h = SB_HEADS
    n_grp = qt.shape[1] // (n_h * HEAD_DIM)
    nb = S // BLK
    return pl.pallas_call(
        _sb_kernel,
        grid=(B, n_grp, nb),
        in_specs=[
            pl.BlockSpec((1, n_h * HEAD_DIM, BLK), lambda b, h, i: (b, h, i)),
            pl.BlockSpec((1, S, n_h * HEAD_DIM), lambda b, h, i: (b, 0, h)),
            pl.BlockSpec((1, n_h, nb, HEAD_DIM, BLK), lambda b, h, i: (b, h, 0, 0, 0)),
            pl.BlockSpec((BLK, BLK), lambda b, h, i: (0, 0)),
        ],
        out_specs=pl.BlockSpec((1, BLK, n_h * HEAD_DIM), lambda b, h, i: (b, i, h)),
        out_shape=jax.ShapeDtypeStruct((B, S, n_grp * n_h * HEAD_DIM), BF16),
        scratch_shapes=[
            pltpu.VMEM((n_h, 1, BLK), F32),
            pltpu.VMEM((n_h, HEAD_DIM, BLK), F32),
        ],
        compiler_params=_params(("arbitrary", "arbitrary", "arbitrary")),
        name="sb",
    )(qt, ks, vt, tri)


def _pre1_kernel(x_ref, g_ref, wt_ref, pos_ref, inv_ref, qt_ref, k_ref, vt_ref, *, n_q, n_kv):
    tm = x_ref.shape[1]
    half = HEAD_DIM // 2
    hb = _rms(x_ref[0], g_ref[...]).astype(BF16)
    ang = inv_ref[...] * pos_ref[0].astype(F32)
    cos = jnp.cos(ang)
    sin = jnp.sin(ang)

    def proj_t(r0, rows):
        return lax.dot_general(wt_ref[r0:r0 + rows, :], hb, _NT, preferred_element_type=F32)

    def rope_t(x):
        x1, x2 = x[0:half], x[half:HEAD_DIM]
        return jnp.concatenate([x1 * cos - x2 * sin, x2 * cos + x1 * sin], axis=0)

    qw = n_q * HEAD_DIM
    for c in range(n_q // 4):
        qt = proj_t(c * 4 * HEAD_DIM, 4 * HEAD_DIM)
        for a in range(4):
            hq = c * 4 + a
            qt_ref[0, hq * HEAD_DIM:(hq + 1) * HEAD_DIM, :] = rope_t(
                qt[a * HEAD_DIM:(a + 1) * HEAD_DIM]).astype(BF16)
    kt = proj_t(qw, n_kv * HEAD_DIM)
    zpad = jnp.zeros((LANES - HEAD_DIM, tm), F32)
    for g in range(n_kv):
        kg = jnp.concatenate([rope_t(kt[g * HEAD_DIM:(g + 1) * HEAD_DIM]), zpad], axis=0)
        k_ref[0, :, g * LANES:(g + 1) * LANES] = kg.T.astype(BF16)
    vt = proj_t(qw + n_kv * HEAD_DIM, n_kv * HEAD_DIM).astype(BF16)
    pad_row = lax.broadcasted_iota(jnp.int32, (V_ROWS - HEAD_DIM, tm), 0)
    ones_pad = jnp.where(pad_row == 0, 1.0, 0.0).astype(BF16)
    for g in range(n_kv):
        vt_ref[0, g * V_ROWS:g * V_ROWS + HEAD_DIM, :] = vt[g * HEAD_DIM:(g + 1) * HEAD_DIM]
        vt_ref[0, g * V_ROWS + HEAD_DIM:(g + 1) * V_ROWS, :] = ones_pad


def _pre1(h, g, wt, pos, inv, tm, n_q, n_kv):
    B, S, D = h.shape
    return pl.pallas_call(
        functools.partial(_pre1_kernel, n_q=n_q, n_kv=n_kv),
        grid=(B, S // tm),
        in_specs=[
            pl.BlockSpec((1, tm, D), lambda b, i: (b, i, 0)),
            pl.BlockSpec((1, D), lambda b, i: (0, 0)),
            pl.BlockSpec(wt.shape, lambda b, i: (0, 0)),
            pl.BlockSpec((1, 1, tm), lambda b, i: (b, 0, i)),
            pl.BlockSpec(inv.shape, lambda b, i: (0, 0)),
        ],
        out_specs=[
            pl.BlockSpec((1, n_q * HEAD_DIM, tm), lambda b, i: (b, 0, i)),
            pl.BlockSpec((1, tm, n_kv * LANES), lambda b, i: (b, i, 0)),
            pl.BlockSpec((1, n_kv * V_ROWS, tm), lambda b, i: (b, 0, i)),
        ],
        out_shape=[
            jax.ShapeDtypeStruct((B, n_q * HEAD_DIM, S), BF16),
            jax.ShapeDtypeStruct((B, S, n_kv * LANES), BF16),
            jax.ShapeDtypeStruct((B, n_kv * V_ROWS, S), BF16),
        ],
        compiler_params=_params(("arbitrary", "arbitrary")),
        name="pre1",
    )(h, g, wt, pos, inv)


def _swa_kernel(sink_ref, qt_ref, kp_ref, ko_ref, vp_ref, vo_ref, o_ref, *, n_kv, group):
    i = pl.program_id(1)
    W = WINDOW
    n_sub = qt_ref.shape[2] // W
    r = lax.broadcasted_iota(jnp.int32, (2 * W, W), 0)
    c = lax.broadcasted_iota(jnp.int32, (2 * W, W), 1)
    rel = c + W - r
    band = (rel >= 0) & (rel < W)
    valid = [jnp.concatenate([band & ((r >= W) | (i > 0)) if u == 0 else band] * group, axis=1)
             for u in range(n_sub)]
    seg = lax.broadcasted_iota(jnp.int32, (1, group * W), 1) // W
    zpad = jnp.zeros((LANES - HEAD_DIM, group * W), BF16)
    chains = [(u, g) for u in range(n_sub) for g in range(n_kv)]

    def keys(u, g):
        ls = slice(g * LANES, (g + 1) * LANES)
        if u == 0:
            return jnp.concatenate([kp_ref[0, :, ls], ko_ref[0, 0:W, ls]], axis=0)
        return ko_ref[0, (u - 1) * W:(u + 1) * W, ls]

    def values(u, g):
        rs = slice(g * V_ROWS, (g + 1) * V_ROWS)
        if u == 0:
            return jnp.concatenate([vp_ref[0, rs, :], vo_ref[0, rs, 0:W]], axis=1)
        return vo_ref[0, rs, (u - 1) * W:(u + 1) * W]

    st = []
    for u, g in chains:
        qg = jnp.concatenate(
            [qt_ref[0, (g * group + a) * HEAD_DIM:(g * group + a + 1) * HEAD_DIM, u * W:(u + 1) * W]
             for a in range(group)], axis=1)
        st.append(jnp.dot(keys(u, g), jnp.concatenate([qg, zpad], axis=0),
                          preferred_element_type=F32))
    p, sink_term = [], []
    for n, (u, g) in enumerate(chains):
        sg = jnp.where(valid[u], st[n], NEG_INF)
        sink = jnp.zeros((1, group * W), F32)
        for a in range(group):
            sink = jnp.where(seg == a, sink_ref[g * group + a] * LOG2E, sink)
        m = jnp.maximum(jnp.max(sg, axis=0, keepdims=True), sink)
        p.append(jnp.exp2(sg - m).astype(BF16))
        sink_term.append(jnp.exp2(sink - m))
    acc = [jnp.dot(values(u, g), p[n], preferred_element_type=F32)
           for n, (u, g) in enumerate(chains)]
    for n, (u, g) in enumerate(chains):
        o = acc[n][0:HEAD_DIM] / (acc[n][HEAD_DIM:HEAD_DIM + 1] + sink_term[n])
        for a in range(0, group, 2):
            pair = jnp.concatenate([o[:, a * W:(a + 1) * W], o[:, (a + 1) * W:(a + 2) * W]], axis=0)
            l0 = (g * group + a) * HEAD_DIM
            o_ref[0, u * W:(u + 1) * W, l0:l0 + 2 * HEAD_DIM] = pair.T.astype(o_ref.dtype)


def _swa(qt, kpad, vt, sinks, n_q, n_kv):
    B, _, S = qt.shape
    W = WINDOW
    n_sub = SWA_SUB
    prev = lambda i: jnp.maximum(n_sub * i - 1, 0)
    return pl.pallas_call(
        functools.partial(_swa_kernel, n_kv=n_kv, group=n_q // n_kv),
        grid_spec=pltpu.PrefetchScalarGridSpec(
            num_scalar_prefetch=1,
            grid=(B, S // (n_sub * W)),
            in_specs=[
                pl.BlockSpec((1, n_q * HEAD_DIM, n_sub * W), lambda b, i, s: (b, 0, i)),
                pl.BlockSpec((1, W, n_kv * LANES), lambda b, i, s: (b, prev(i), 0)),
                pl.BlockSpec((1, n_sub * W, n_kv * LANES), lambda b, i, s: (b, i, 0)),
                pl.BlockSpec((1, n_kv * V_ROWS, W), lambda b, i, s: (b, 0, prev(i))),
                pl.BlockSpec((1, n_kv * V_ROWS, n_sub * W), lambda b, i, s: (b, 0, i)),
            ],
            out_specs=pl.BlockSpec((1, n_sub * W, n_q * HEAD_DIM), lambda b, i, s: (b, i, 0)),
        ),
        out_shape=jax.ShapeDtypeStruct((B, S, n_q * HEAD_DIM), BF16),
        compiler_params=_params(("arbitrary", "arbitrary")),
        name="swa",
    )(sinks, qt, kpad, kpad, vt, vt)


def _post_kernel(*refs, n_mix, final_norm):
    h_ref = refs[0]
    mix_refs = refs[1:1 + n_mix]
    (p_ref, wo_ref, gf_ref, wg_ref, wu_ref, wd_ref, gp_ref, wpg_ref, wpp_ref, gfin_ref,
     o_ref) = refs[1 + n_mix:]
    h = h_ref[...]
    off = 0
    for m_ref in mix_refs:
        w = m_ref.shape[1]
        h = h + jnp.dot(m_ref[...], wo_ref[off:off + w, :], preferred_element_type=F32)
        off += w
    hb = _rms(h, gf_ref[...]).astype(BF16)
    g = jnp.dot(hb, wg_ref[...], preferred_element_type=F32)
    u = jnp.dot(hb, wu_ref[...], preferred_element_type=F32)
    act = (g * jax.nn.sigmoid(g) * u).astype(BF16)
    h = h + jnp.dot(act, wd_ref[...], preferred_element_type=F32)
    gate = jax.nn.sigmoid(jnp.dot(_rms(h, gp_ref[...]).astype(BF16), wpg_ref[...],
                                  preferred_element_type=F32))
    h = h + gate * jnp.dot(p_ref[...].astype(BF16), wpp_ref[...], preferred_element_type=F32)
    if final_norm:
        h = _rms(h, gfin_ref[...])
    o_ref[...] = h


def _post(h, mixes, p_all, layer, wo, gf, wg, wu, wd, gp, wpg, wpp, gfin, tm, final_norm):
    T, D = h.shape
    row = lambda w: pl.BlockSpec((tm, w), lambda i: (i, 0))
    full = lambda a: pl.BlockSpec(a.shape, lambda i: (0, 0))
    lay = lambda a: pl.BlockSpec((None,) + a.shape[1:], lambda i: (layer, 0, 0),
                                 pipeline_mode=pl.Buffered(1))
    return pl.pallas_call(
        functools.partial(_post_kernel, n_mix=len(mixes), final_norm=final_norm),
        grid=(T // tm,),
        in_specs=[row(D)] + [row(m.shape[1]) for m in mixes]
        + [pl.BlockSpec((None, tm, p_all.shape[2]), lambda i: (layer, i, 0))]
        + [full(wo)] + [lay(a) for a in (gf, wg, wu, wd, gp, wpg, wpp)] + [full(gfin)],
        out_specs=row(D),
        out_shape=jax.ShapeDtypeStruct((T, D), F32),
        compiler_params=_params(("arbitrary",)),
        name="post",
    )(h, *mixes, p_all, wo, gf, wg, wu, wd, gp, wpg, wpp, gfin)


def _layer0_weights(w_in, b_f, n_fox):
    fw = n_fox * HEAD_DIM
    scale = HEAD_DIM ** -0.5 * LOG2E
    D = w_in.shape[0]
    qa, ka, va, qs, ks, vs = (w_in[:, i * fw:(i + 1) * fw] for i in range(6))
    wt = jnp.concatenate([qa * scale, va, qs * scale, vs], axis=1).T.astype(BF16)
    wk = jnp.concatenate([ka, ks], axis=1).astype(BF16)
    gate_pad = ((0, 0), (0, LANES - 3 * n_fox))
    wf = jnp.pad(jnp.tile(w_in[:, 6 * fw:], (1, 3)), gate_pad).astype(BF16)
    bf = jnp.pad(jnp.tile(b_f.reshape(1, n_fox), (1, 3)), gate_pad)
    heads = jnp.arange(n_fox)
    pk = jnp.zeros((LANES, n_fox // 2 * LANES), F32)
    aq = jnp.zeros((n_fox * FEAT_ROWS, LANES), F32)
    for piece in range(3):
        pk = pk.at[piece * n_fox + heads,
                   heads // 2 * LANES + heads % 2 * FEAT_ROWS + 3 + piece].set(-1.0)
        aq = aq.at[heads * FEAT_ROWS + piece, piece * n_fox + heads].set(1.0)
    return wt, wk, wf, bf, pk.astype(BF16), aq.astype(BF16)


def kernel(x, p, positions, norm_mix, norm_ffn, norm_ple, norm_final, ev_w_in, ev_b_f, ev_w_out,
           od_w_in, od_sinks, od_w_out, ffn_w_gate, ffn_w_up, ffn_w_down, ple_w_proj, ple_w_gate):
    B, S, D = x.shape
    T = B * S
    n_heads = D // HEAD_DIM
    n_fox = n_heads // 2
    fox_w = n_fox * HEAD_DIM
    n_q, n_kv = n_heads, 4
    assert S % BLK == 0 and n_fox == 8
    row = lambda a: a.reshape(1, -1)

    wt, wk, wf, bf, pk, aq = _layer0_weights(ev_w_in[0], ev_b_f[0], n_fox)
    (qtf, qts, vtf, vts, kf, ks, feat, gt, qn, fb, kmsq) = _pre0(
        x, row(norm_mix[0]), wt, wk, wf, bf, pk, aq, n_fox)
    fb_heads = fb[:, :, 0, :n_fox].transpose(0, 2, 1)
    o_fox = _fox(fb_heads, qtf, feat, kf, vtf, gt, qn, kmsq)
    tri = (jnp.arange(BLK)[None, :] > jnp.arange(BLK)[:, None]).astype(BF16)
    o_sb = _sb(qts, ks, vts, tri)

    tm = min(512, T)
    depth = norm_ffn.shape[0]
    p_all = p.reshape(depth, T, -1)
    stacked = (norm_ffn.reshape(depth, 1, D), ffn_w_gate.astype(BF16), ffn_w_up.astype(BF16),
               ffn_w_down.astype(BF16), norm_ple.reshape(depth, 1, D), ple_w_gate.astype(BF16),
               ple_w_proj.astype(BF16))
    h = _post(x.reshape(T, D), [o_fox.reshape(T, fox_w), o_sb.reshape(T, fox_w)], p_all, 0,
              ev_w_out[0].astype(BF16), *stacked, row(norm_final), tm, final_norm=False)

    qw = n_q * HEAD_DIM
    kw = n_kv * HEAD_DIM
    col_scale1 = jnp.ones((qw + 2 * kw,), F32).at[:qw].set(HEAD_DIM ** -0.5 * LOG2E)
    w1t = (od_w_in[0] * col_scale1).T.astype(BF16)
    half = HEAD_DIM // 2
    inv = (ROPE_THETA ** (-jnp.arange(half, dtype=F32) / half)).reshape(half, 1)
    qt1, k1, vt1 = _pre1(h.reshape(B, S, D), row(norm_mix[1]), w1t, positions.reshape(B, 1, S),
                         inv, min(tm, S), n_q, n_kv)
    o_swa = _swa(qt1, k1, vt1, od_sinks[0], n_q, n_kv)
    out = _post(h, [o_swa.reshape(T, qw)], p_all, 1, od_w_out[0].astype(BF16), *stacked,
                row(norm_final), tm, final_norm=True)
    return out.reshape(B, S, D)
```

```python
import functools

import jax
import jax.numpy as jnp
from jax import lax
from jax.experimental import pallas as pl
from jax.experimental.pallas import tpu as pltpu

F32 = jnp.float32
BF16 = jnp.bfloat16

HEAD_DIM = 64
LANES = 128
BLK = 256
V_ROWS = 80
FEAT_ROWS = 16
WINDOW = 128
ROPE_THETA = 10000.0
EPS = 1e-6
NEG_INF = -1e30
LOG2E = 1.4426950408889634
SKIP_LOG2 = 60.0 * LOG2E
FOX_HEADS = 8
SWA_SUB = 4
VMEM_LIMIT = 56 * 1024 * 1024

_NT = (((1,), (1,)), ((), ()))


def _params(sem):
    return pltpu.CompilerParams(dimension_semantics=sem, vmem_limit_bytes=VMEM_LIMIT)


def _rms(x, g):
    return x * lax.rsqrt(jnp.mean(x * x, axis=-1, keepdims=True) + EPS) * g


def _log_sigmoid(x):
    return jnp.minimum(x, 0.0) - jnp.log(1.0 + jnp.exp(-jnp.abs(x)))


def _split3(x):
    a = x.astype(BF16)
    r = x - a.astype(F32)
    b = r.astype(BF16)
    c = (r - b.astype(F32)).astype(BF16)
    return a, b, c


def _pre0_kernel(x_ref, g_ref, wt_ref, wk_ref, wf_ref, bf_ref, pk_ref, aq_ref, tri_ref,
                 qtf_ref, vtf_ref, kf_ref, feat_ref, gt_ref, qn_ref, fb_ref, kmsq_ref, osb_ref,
                 carry_ref, qts_s, ks_s, vts_s, r_ref, acc_ref, *, n_fox):
    tm = x_ref.shape[1]
    fw = n_fox * HEAD_DIM
    i = pl.program_id(1)
    nb = pl.num_programs(1) - 1
    extra = i == nb
    t = jnp.minimum(i, nb - 1)
    qb = jnp.maximum(i - 1, 0)
    heads = range(n_fox)

    @pl.when(i == 0)
    def _():
        carry_ref[...] = jnp.zeros_like(carry_ref)
        qts_s[...] = jnp.zeros_like(qts_s)
        ks_s[0:tm, :] = jnp.zeros((tm, fw), BF16)
        vts_s[:, 0] = jnp.zeros((n_fox, HEAD_DIM, tm), BF16)

    rowi = lax.broadcasted_iota(jnp.int32, (BLK, tm), 0)
    coli = lax.broadcasted_iota(jnp.int32, (BLK, tm), 1)
    strict = rowi < coli
    top = lax.broadcasted_iota(jnp.int32, (LANES, tm), 0) < HEAD_DIM
    zero = jnp.zeros((LANES, tm), BF16)
    qh = []
    for h in heads:
        pair = qts_s[(h // 2) * LANES:(h // 2 + 1) * LANES, :]
        qh.append(jnp.where(top, pair, zero) if h % 2 == 0 else jnp.where(top, zero, pair))
    tri_sb = tri_ref[...]

    def kpair(k0, h):
        return ks_s[pl.ds(k0, BLK), (h // 2) * LANES:(h // 2 + 1) * LANES]

    def stick_logs(z, mask=None):
        nz = -z
        l1 = jnp.minimum(nz, 0.0) - jnp.log2(1.0 + jnp.exp2(jnp.minimum(z, nz)))
        lb = z + l1
        if mask is not None:
            l1 = jnp.where(mask, l1, 0.0)
        return lb, l1.astype(BF16), l1[0:1, :]

    def suffix(l1b):
        return jnp.dot(tri_sb, l1b, preferred_element_type=F32)

    def tproj(c, half):
        r0 = c * fw + half * (fw // 2)
        return lax.dot_general(wt_ref[r0:r0 + fw // 2, :], hb, _NT,
                               preferred_element_type=F32).astype(BF16)

    jp = jnp.maximum(qb - 1, 0)
    kp0 = pl.multiple_of(jp * BLK, BLK)
    kd0 = pl.multiple_of(qb * BLK, BLK)
    has_prev = qb > 0
    cp = jnp.where(has_prev, 0.0, NEG_INF)
    z2 = [jnp.dot(jnp.concatenate([kpair(kp0, h), kpair(kd0, h)], axis=0), qh[h],
                  preferred_element_type=F32) for h in heads]

    hb = _rms(x_ref[0], g_ref[...]).astype(BF16)
    gate = jnp.dot(hb, wf_ref[...], preferred_element_type=F32) + bf_ref[...]
    lf = _log_sigmoid(gate) * LOG2E

    lp, ld, tp = [], [], []
    for h in heads:
        lp.append(stick_logs(z2[h][0:BLK]))
        ld.append(stick_logs(z2[h][BLK:2 * BLK], strict))
        tp.append(tproj(h // 2, h % 2))
    qtf = jnp.concatenate(tp[0:2], axis=0)
    vtf = jnp.concatenate(tp[2:4], axis=0)
    qts_new = jnp.concatenate(tp[4:6], axis=0)
    vts = jnp.concatenate(tp[6:8], axis=0)

    sfp = [suffix(lp[h][1]) for h in heads]
    sfd = [suffix(ld[h][1]) for h in heads]

    row = lax.broadcasted_iota(jnp.int32, (tm, tm), 0)
    col = lax.broadcasted_iota(jnp.int32, (tm, tm), 1)
    tri = jnp.where(row >= col, 1.0, 0.0).astype(BF16)
    G = jnp.zeros((tm, LANES), F32)
    for piece in _split3(lf):
        G = G + jnp.dot(tri, piece, preferred_element_type=F32)
    base = jnp.where(extra, carry_ref[1], carry_ref[0])
    fb_ref[0, 0] = base
    carry_ref[1] = base
    carry_ref[0] = base + G[tm - 1:tm, :]

    g_hi, g_mid, g_lo = _split3(G)
    lane_t = lax.broadcasted_iota(jnp.int32, (tm, LANES), 1)
    gp = jnp.where(lane_t < n_fox, g_hi, jnp.where(lane_t < 2 * n_fox, g_mid, g_lo))
    kfeat = jnp.dot(gp, pk_ref[...], preferred_element_type=F32)

    GT = G.T
    gt_ref[0] = GT[0:8, :]
    t_hi, t_mid, t_lo = _split3(GT)
    row_t = lax.broadcasted_iota(jnp.int32, (LANES, tm), 0)
    gpt = jnp.where(row_t < n_fox, t_hi, jnp.where(row_t < 2 * n_fox, t_mid, t_lo))
    qfeat = jnp.dot(aq_ref[...], gpt, preferred_element_type=F32)
    frow = lax.broadcasted_iota(jnp.int32, (n_fox * FEAT_ROWS, 1), 0) % FEAT_ROWS
    qones = jnp.where((frow >= 3) & (frow < 6), 1.0, 0.0)
    feat_ref[0] = (qfeat + qones).astype(BF16)

    a_first, kk = [], []
    nk = 2 * wk_ref.shape[1] // n_fox
    for h in heads:
        tot_d = sfd[h][0:1, :] + ld[h][2]
        tot_p = sfp[h][0:1, :] + lp[h][2]
        a_d = jnp.where(strict, jnp.exp2(ld[h][0] + sfd[h]), 0.0)
        a_p = jnp.exp2(lp[h][0] + sfp[h] + (tot_d + cp))
        a_first.append(jnp.concatenate([a_p, a_d], axis=0).astype(BF16))
        r_ref[h] = tot_d + jnp.where(has_prev, tot_p, 0.0)
        if h % 2 == 0:
            kk.append(jnp.dot(hb, wk_ref[:, (h // 2) * nk:(h // 2 + 1) * nk],
                              preferred_element_type=F32))
    kk = jnp.concatenate(kk, axis=1)
    for h in heads:
        vt2 = jnp.concatenate([vts_s[h, jp], vts_s[h, qb]], axis=1)
        acc_ref[h] = jnp.dot(vt2, a_first[h], preferred_element_type=F32)

    qtf_ref[0] = qtf
    lane_row = lax.broadcasted_iota(jnp.int32, (V_ROWS - HEAD_DIM, tm), 0)
    ones_pad = jnp.where(lane_row == 0, 1.0, 0.0).astype(BF16)
    for h in heads:
        vtf_ref[0, h, 0, 0:HEAD_DIM, :] = vtf[h * HEAD_DIM:(h + 1) * HEAD_DIM, :]
        vtf_ref[0, h, 0, HEAD_DIM:V_ROWS, :] = ones_pad
    qts_s[...] = qts_new
    for h in heads:
        vts_s[h, t] = vts[h * HEAD_DIM:(h + 1) * HEAD_DIM, :]

    q32 = qtf.astype(F32)
    qn_rows = [jnp.sqrt(jnp.sum(jnp.square(q32[h * HEAD_DIM:(h + 1) * HEAD_DIM, :]),
                                axis=0, keepdims=True)) for h in heads]
    qn_ref[0] = jnp.concatenate(qn_rows, axis=0)

    ks_s[pl.ds(pl.multiple_of(t * tm, tm), tm), :] = kk[:, fw:].astype(BF16)
    kfox = kk[:, :fw].astype(BF16)
    k32 = kfox.astype(F32)
    lane = lax.broadcasted_iota(jnp.int32, (1, LANES), 1)
    first_head = lane < HEAD_DIM
    kmsq = jnp.zeros((1, LANES), F32)
    for pr in range(n_fox // 2):
        sq = jnp.square(k32[:, pr * LANES:(pr + 1) * LANES])
        for e in range(2):
            mine = first_head if e == 0 else jnp.logical_not(first_head)
            ss = jnp.sum(jnp.where(mine, sq, 0.0), axis=-1, keepdims=True)
            kmsq = jnp.where(lane == 2 * pr + e, jnp.max(ss, axis=0, keepdims=True), kmsq)
    kmsq_ref[0, 0] = kmsq

    kones = jnp.where((lane % FEAT_ROWS < 3) & (lane < 2 * FEAT_ROWS), 1.0, 0.0)
    for pr in range(n_fox // 2):
        kf_ref[0, :, 2 * pr * LANES:(2 * pr + 1) * LANES] = kfox[:, pr * LANES:(pr + 1) * LANES]
        kf_ref[0, :, (2 * pr + 1) * LANES:(2 * pr + 2) * LANES] = (
            kfeat[:, pr * LANES:(pr + 1) * LANES] + kones).astype(BF16)

    def rmax():
        out = jnp.max(r_ref[0])
        for h in heads[1:]:
            out = jnp.maximum(out, jnp.max(r_ref[h]))
        return out

    def block(j):
        k0 = pl.multiple_of(j * BLK, BLK)
        z = [jnp.dot(kpair(k0, h), qh[h], preferred_element_type=F32) for h in heads]
        lg = [stick_logs(z[h]) for h in heads]
        sfx = [suffix(lg[h][1]) for h in heads]
        a = []
        for h in heads:
            r_old = r_ref[h]
            a.append(jnp.exp2(lg[h][0] + sfx[h] + r_old).astype(BF16))
            r_ref[h] = r_old + (sfx[h][0:1, :] + lg[h][2])
        for h in heads:
            acc_ref[h] = acc_ref[h] + jnp.dot(vts_s[h, j], a[h], preferred_element_type=F32)
        return rmax()

    def body(carry):
        j, _ = carry
        return j - 1, block(j)

    lax.while_loop(lambda c: (c[0] >= 0) & (c[1] > -SKIP_LOG2), body, (qb - 2, rmax()))

    osb_ref[0] = jnp.concatenate([acc_ref[h] for h in heads], axis=0).T.astype(osb_ref.dtype)


def _pre0(x, g, wt, wk, wf, bf, pk, aq, tri, n_fox):
    B, S, D = x.shape
    tm = BLK
    nb = S // tm
    fw = n_fox * HEAD_DIM
    const = lambda a: pl.BlockSpec(a.shape, lambda b, s: (0,) * a.ndim)
    tile = lambda s: jnp.minimum(s, nb - 1)
    tok_lane = lambda rows: pl.BlockSpec((1, rows, tm), lambda b, s: (b, 0, tile(s)))
    return pl.pallas_call(
        functools.partial(_pre0_kernel, n_fox=n_fox),
        grid=(B, nb + 1),
        in_specs=[pl.BlockSpec((1, tm, D), lambda b, s: (b, tile(s), 0))]
        + [const(a) for a in (g, wt, wk, wf, bf, pk, aq, tri)],
        out_specs=[
            tok_lane(fw),
            pl.BlockSpec((1, n_fox, 1, V_ROWS, tm), lambda b, s: (b, 0, tile(s), 0, 0)),
            pl.BlockSpec((1, tm, n_fox * LANES), lambda b, s: (b, tile(s), 0)),
            tok_lane(n_fox * FEAT_ROWS),
            tok_lane(8),
            tok_lane(8),
            pl.BlockSpec((1, 1, 1, LANES), lambda b, s: (b, tile(s), 0, 0)),
            pl.BlockSpec((1, 1, 1, LANES), lambda b, s: (b, tile(s), 0, 0)),
            pl.BlockSpec((1, tm, fw), lambda b, s: (b, jnp.maximum(s - 1, 0), 0)),
        ],
        out_shape=[
            jax.ShapeDtypeStruct((B, fw, S), BF16),
            jax.ShapeDtypeStruct((B, n_fox, nb, V_ROWS, tm), BF16),
            jax.ShapeDtypeStruct((B, S, n_fox * LANES), BF16),
            jax.ShapeDtypeStruct((B, n_fox * FEAT_ROWS, S), BF16),
            jax.ShapeDtypeStruct((B, 8, S), F32),
            jax.ShapeDtypeStruct((B, 8, S), F32),
            jax.ShapeDtypeStruct((B, nb, 1, LANES), F32),
            jax.ShapeDtypeStruct((B, nb, 1, LANES), F32),
            jax.ShapeDtypeStruct((B, S, fw), BF16),
        ],
        scratch_shapes=[
            pltpu.VMEM((2, 1, LANES), F32),
            pltpu.VMEM((fw, tm), BF16),
            pltpu.VMEM((S, fw), BF16),
            pltpu.VMEM((n_fox, nb, HEAD_DIM, tm), BF16),
            pltpu.VMEM((n_fox, 1, tm), F32),
            pltpu.VMEM((n_fox, HEAD_DIM, tm), F32),
        ],
        compiler_params=_params(("arbitrary", "arbitrary")),
        name="pre0",
    )(x, g, wt, wk, wf, bf, pk, aq, tri)


def _fox_kernel(fb_ref, qt_ref, feat_ref, k_ref, vt_ref, gt_ref, qn_ref, kmsq_ref,
                o_ref, m_ref, acc_ref):
    b = pl.program_id(0)
    hp = pl.program_id(1)
    qi = pl.program_id(2)
    bq = qt_ref.shape[2]
    n_h = acc_ref.shape[0]
    heads = range(n_h)
    hg = [n_h * hp + h for h in heads]
    row = lax.broadcasted_iota(jnp.int32, (BLK, bq), 0)
    col = lax.broadcasted_iota(jnp.int32, (BLK, bq), 1)
    causal = row <= col
    kmax_sq = jnp.max(kmsq_ref[0], axis=0)
    lane = lax.broadcasted_iota(jnp.int32, (1, LANES), 1)

    def zeros(rows):
        return jnp.zeros((rows, bq), BF16)

    qaug = []
    for h in heads:
        q = qt_ref[0, h * HEAD_DIM:(h + 1) * HEAD_DIM, :]
        f = feat_ref[0, h * FEAT_ROWS:(h + 1) * FEAT_ROWS, :]
        parts = [q, zeros(HEAD_DIM), f, zeros(FEAT_ROWS)] if h % 2 == 0 else \
                [zeros(HEAD_DIM), q, zeros(FEAT_ROWS), f]
        qaug.append(jnp.concatenate(parts + [zeros(LANES - 2 * FEAT_ROWS)], axis=0))

    def kpair(k0, h):
        return k_ref[0, pl.ds(k0, BLK), (h // 2) * 2 * LANES:(h // 2 + 1) * 2 * LANES]

    jp = jnp.maximum(qi - 1, 0)
    kp0 = pl.multiple_of(jp * BLK, BLK)
    kd0 = pl.multiple_of(qi * BLK, BLK)
    st = [jnp.dot(jnp.concatenate([kpair(kp0, h), kpair(kd0, h)], axis=0),
                  qaug[h], preferred_element_type=F32) for h in heads]
    p_first = []
    for h in heads:
        cp = jnp.where(qi > 0, fb_ref[b, hg[h], qi] - fb_ref[b, hg[h], jp], NEG_INF)
        st_p = st[h][0:BLK]
        st_d = jnp.where(causal, st[h][BLK:2 * BLK], NEG_INF)
        m = jnp.maximum(jnp.max(st_d, axis=0, keepdims=True),
                        jnp.max(st_p, axis=0, keepdims=True) + cp)
        m_ref[h] = m
        p_first.append(jnp.concatenate([jnp.exp2(st_p - (m - cp)), jnp.exp2(st_d - m)],
                                       axis=0).astype(BF16))
    for h in heads:
        vt2 = jnp.concatenate([vt_ref[0, h, jp], vt_ref[0, h, qi]], axis=1)
        acc_ref[h] = jnp.dot(vt2, p_first[h], preferred_element_type=F32)

    def block(j):
        k0 = pl.multiple_of(j * BLK, BLK)
        st = [jnp.dot(kpair(k0, h), qaug[h], preferred_element_type=F32)
              for h in heads]
        p, alpha = [], []
        for h in heads:
            c = fb_ref[b, hg[h], qi] - fb_ref[b, hg[h], j]
            m_old = m_ref[h]
            m_new = jnp.maximum(m_old, jnp.max(st[h], axis=0, keepdims=True) + c)
            p.append(jnp.exp2(st[h] - (m_new - c)).astype(BF16))
            alpha.append(jnp.exp2(m_old - m_new))
            m_ref[h] = m_new
        for h in heads:
            acc_ref[h] = alpha[h] * acc_ref[h] + jnp.dot(vt_ref[0, h, j], p[h],
                                                         preferred_element_type=F32)

    th = []
    for h in heads:
        kmax = jnp.sqrt(jnp.sum(jnp.where(lane == hg[h], kmax_sq, 0.0), axis=-1, keepdims=True))
        bound = qn_ref[0, pl.ds(hg[h], 1), :] * kmax + gt_ref[0, pl.ds(hg[h], 1), :] - m_ref[h]
        th.append(jnp.max(bound) + SKIP_LOG2)

    def needed(j):
        jn = jnp.clip(j + 1, 0, qi)
        need = fb_ref[b, hg[0], jn] - fb_ref[b, hg[0], qi] <= th[0]
        for h in heads[1:]:
            need = need | (fb_ref[b, hg[h], jn] - fb_ref[b, hg[h], qi] <= th[h])
        return (j >= 0) & need

    def body(j):
        block(j)
        return j - 1

    lax.while_loop(needed, body, qi - 2)

    out_t = jnp.concatenate(
        [acc_ref[h, 0:HEAD_DIM, :] / acc_ref[h, HEAD_DIM:HEAD_DIM + 1, :] for h in heads], axis=0)
    o_ref[0] = out_t.T.astype(o_ref.dtype)


def _fox(fb, qt, feat, kf, vt, gt, qn, kmsq):
    B, _, S = qt.shape
    n_h = FOX_HEADS
    n_grp = qt.shape[1] // (n_h * HEAD_DIM)
    nb = S // BLK
    return pl.pallas_call(
        _fox_kernel,
        grid_spec=pltpu.PrefetchScalarGridSpec(
            num_scalar_prefetch=1,
            grid=(B, n_grp, nb),
            in_specs=[
                pl.BlockSpec((1, n_h * HEAD_DIM, BLK), lambda b, h, i, s: (b, h, i)),
                pl.BlockSpec((1, n_h * FEAT_ROWS, BLK), lambda b, h, i, s: (b, h, i)),
                pl.BlockSpec((1, S, n_h * LANES), lambda b, h, i, s: (b, 0, h),
                             pipeline_mode=pl.Buffered(1)),
                pl.BlockSpec((1, n_h, nb, V_ROWS, BLK), lambda b, h, i, s: (b, h, 0, 0, 0),
                             pipeline_mode=pl.Buffered(1)),
                pl.BlockSpec((1, 8, BLK), lambda b, h, i, s: (b, 0, i)),
                pl.BlockSpec((1, 8, BLK), lambda b, h, i, s: (b, 0, i)),
                pl.BlockSpec((1, nb, 1, LANES), lambda b, h, i, s: (b, 0, 0, 0)),
            ],
            out_specs=pl.BlockSpec((1, BLK, n_h * HEAD_DIM), lambda b, h, i, s: (b, i, h)),
            scratch_shapes=[
                pltpu.VMEM((n_h, 1, BLK), F32),
                pltpu.VMEM((n_h, V_ROWS, BLK), F32),
            ],
        ),
        out_shape=jax.ShapeDtypeStruct((B, S, n_grp * n_h * HEAD_DIM), BF16),
        compiler_params=_params(("arbitrary", "arbitrary", "arbitrary")),
        name="fox",
    )(fb, qt, feat, kf, vt, gt, qn, kmsq)


def _pre1_kernel(x_ref, g_ref, wt_ref, pos_ref, inv_ref, qt_ref, k_ref, vt_ref, *, n_q, n_kv):
    tm = x_ref.shape[1]
    half = HEAD_DIM // 2
    hb = _rms(x_ref[0], g_ref[...]).astype(BF16)
    ang = inv_ref[...] * pos_ref[0].astype(F32)
    cos = jnp.cos(ang)
    sin = jnp.sin(ang)

    def proj_t(r0, rows):
        return lax.dot_general(wt_ref[r0:r0 + rows, :], hb, _NT, preferred_element_type=F32)

    def rope_t(x):
        x1, x2 = x[0:half], x[half:HEAD_DIM]
        return jnp.concatenate([x1 * cos - x2 * sin, x2 * cos + x1 * sin], axis=0)

    qw = n_q * HEAD_DIM
    for c in range(n_q // 4):
        qt = proj_t(c * 4 * HEAD_DIM, 4 * HEAD_DIM)
        for a in range(4):
            hq = c * 4 + a
            qt_ref[0, hq * HEAD_DIM:(hq + 1) * HEAD_DIM, :] = rope_t(
                qt[a * HEAD_DIM:(a + 1) * HEAD_DIM]).astype(BF16)
    kt = proj_t(qw, n_kv * HEAD_DIM)
    zpad = jnp.zeros((LANES - HEAD_DIM, tm), F32)
    for g in range(n_kv):
        kg = jnp.concatenate([rope_t(kt[g * HEAD_DIM:(g + 1) * HEAD_DIM]), zpad], axis=0)
        k_ref[0, :, g * LANES:(g + 1) * LANES] = kg.T.astype(BF16)
    vt = proj_t(qw + n_kv * HEAD_DIM, n_kv * HEAD_DIM).astype(BF16)
    pad_row = lax.broadcasted_iota(jnp.int32, (V_ROWS - HEAD_DIM, tm), 0)
    ones_pad = jnp.where(pad_row == 0, 1.0, 0.0).astype(BF16)
    for g in range(n_kv):
        vt_ref[0, g * V_ROWS:g * V_ROWS + HEAD_DIM, :] = vt[g * HEAD_DIM:(g + 1) * HEAD_DIM]
        vt_ref[0, g * V_ROWS + HEAD_DIM:(g + 1) * V_ROWS, :] = ones_pad


def _pre1(h, g, wt, pos, inv, tm, n_q, n_kv):
    B, S, D = h.shape
    return pl.pallas_call(
        functools.partial(_pre1_kernel, n_q=n_q, n_kv=n_kv),
        grid=(B, S // tm),
        in_specs=[
            pl.BlockSpec((1, tm, D), lambda b, i: (b, i, 0)),
            pl.BlockSpec((1, D), lambda b, i: (0, 0)),
            pl.BlockSpec(wt.shape, lambda b, i: (0, 0)),
            pl.BlockSpec((1, 1, tm), lambda b, i: (b, 0, i)),
            pl.BlockSpec(inv.shape, lambda b, i: (0, 0)),
        ],
        out_specs=[
            pl.BlockSpec((1, n_q * HEAD_DIM, tm), lambda b, i: (b, 0, i)),
            pl.BlockSpec((1, tm, n_kv * LANES), lambda b, i: (b, i, 0)),
            pl.BlockSpec((1, n_kv * V_ROWS, tm), lambda b, i: (b, 0, i)),
        ],
        out_shape=[
            jax.ShapeDtypeStruct((B, n_q * HEAD_DIM, S), BF16),
            jax.ShapeDtypeStruct((B, S, n_kv * LANES), BF16),
            jax.ShapeDtypeStruct((B, n_kv * V_ROWS, S), BF16),
        ],
        compiler_params=_params(("arbitrary", "arbitrary")),
        name="pre1",
    )(h, g, wt, pos, inv)


def _swa_kernel(sink_ref, qt_ref, kp_ref, ko_ref, vp_ref, vo_ref, o_ref, *, n_kv, group):
    i = pl.program_id(1)
    W = WINDOW
    n_sub = qt_ref.shape[2] // W
    r = lax.broadcasted_iota(jnp.int32, (2 * W, W), 0)
    c = lax.broadcasted_iota(jnp.int32, (2 * W, W), 1)
    rel = c + W - r
    band = (rel >= 0) & (rel < W)
    valid = [jnp.concatenate([band & ((r >= W) | (i > 0)) if u == 0 else band] * group, axis=1)
             for u in range(n_sub)]
    seg = lax.broadcasted_iota(jnp.int32, (1, group * W), 1) // W
    zpad = jnp.zeros((LANES - HEAD_DIM, group * W), BF16)
    chains = [(u, g) for u in range(n_sub) for g in range(n_kv)]

    def keys(u, g):
        ls = slice(g * LANES, (g + 1) * LANES)
        if u == 0:
            return jnp.concatenate([kp_ref[0, :, ls], ko_ref[0, 0:W, ls]], axis=0)
        return ko_ref[0, (u - 1) * W:(u + 1) * W, ls]

    def values(u, g):
        rs = slice(g * V_ROWS, (g + 1) * V_ROWS)
        if u == 0:
            return jnp.concatenate([vp_ref[0, rs, :], vo_ref[0, rs, 0:W]], axis=1)
        return vo_ref[0, rs, (u - 1) * W:(u + 1) * W]

    st = []
    for u, g in chains:
        qg = jnp.concatenate(
            [qt_ref[0, (g * group + a) * HEAD_DIM:(g * group + a + 1) * HEAD_DIM, u * W:(u + 1) * W]
             for a in range(group)], axis=1)
        st.append(jnp.dot(keys(u, g), jnp.concatenate([qg, zpad], axis=0),
                          preferred_element_type=F32))
    p, sink_term = [], []
    for n, (u, g) in enumerate(chains):
        sg = jnp.where(valid[u], st[n], NEG_INF)
        sink = jnp.zeros((1, group * W), F32)
        for a in range(group):
            sink = jnp.where(seg == a, sink_ref[g * group + a] * LOG2E, sink)
        m = jnp.maximum(jnp.max(sg, axis=0, keepdims=True), sink)
        p.append(jnp.exp2(sg - m).astype(BF16))
        sink_term.append(jnp.exp2(sink - m))
    acc = [jnp.dot(values(u, g), p[n], preferred_element_type=F32)
           for n, (u, g) in enumerate(chains)]
    for n, (u, g) in enumerate(chains):
        o = acc[n][0:HEAD_DIM] / (acc[n][HEAD_DIM:HEAD_DIM + 1] + sink_term[n])
        for a in range(0, group, 2):
            pair = jnp.concatenate([o[:, a * W:(a + 1) * W], o[:, (a + 1) * W:(a + 2) * W]], axis=0)
            l0 = (g * group + a) * HEAD_DIM
            o_ref[0, u * W:(u + 1) * W, l0:l0 + 2 * HEAD_DIM] = pair.T.astype(o_ref.dtype)


def _swa(qt, kpad, vt, sinks, n_q, n_kv):
    B, _, S = qt.shape
    W = WINDOW
    n_sub = SWA_SUB
    prev = lambda i: jnp.maximum(n_sub * i - 1, 0)
    return pl.pallas_call(
        functools.partial(_swa_kernel, n_kv=n_kv, group=n_q // n_kv),
        grid_spec=pltpu.PrefetchScalarGridSpec(
            num_scalar_prefetch=1,
            grid=(B, S // (n_sub * W)),
            in_specs=[
                pl.BlockSpec((1, n_q * HEAD_DIM, n_sub * W), lambda b, i, s: (b, 0, i)),
                pl.BlockSpec((1, W, n_kv * LANES), lambda b, i, s: (b, prev(i), 0)),
                pl.BlockSpec((1, n_sub * W, n_kv * LANES), lambda b, i, s: (b, i, 0)),
                pl.BlockSpec((1, n_kv * V_ROWS, W), lambda b, i, s: (b, 0, prev(i))),
                pl.BlockSpec((1, n_kv * V_ROWS, n_sub * W), lambda b, i, s: (b, 0, i)),
            ],
            out_specs=pl.BlockSpec((1, n_sub * W, n_q * HEAD_DIM), lambda b, i, s: (b, i, 0)),
        ),
        out_shape=jax.ShapeDtypeStruct((B, S, n_q * HEAD_DIM), BF16),
        compiler_params=_params(("arbitrary", "arbitrary")),
        name="swa",
    )(sinks, qt, kpad, kpad, vt, vt)


def _post_kernel(*refs, n_mix, final_norm):
    h_ref = refs[0]
    mix_refs = refs[1:1 + n_mix]
    (p_ref, wo_ref, gf_ref, wg_ref, wu_ref, wd_ref, gp_ref, wpg_ref, wpp_ref, gfin_ref,
     o_ref) = refs[1 + n_mix:]
    h = h_ref[...]
    off = 0
    for m_ref in mix_refs:
        w = m_ref.shape[1]
        h = h + jnp.dot(m_ref[...], wo_ref[off:off + w, :], preferred_element_type=F32)
        off += w
    hb = _rms(h, gf_ref[...]).astype(BF16)
    g = jnp.dot(hb, wg_ref[...], preferred_element_type=F32)
    u = jnp.dot(hb, wu_ref[...], preferred_element_type=F32)
    act = (g * jax.nn.sigmoid(g) * u).astype(BF16)
    h = h + jnp.dot(act, wd_ref[...], preferred_element_type=F32)
    gate = jax.nn.sigmoid(jnp.dot(_rms(h, gp_ref[...]).astype(BF16), wpg_ref[...],
                                  preferred_element_type=F32))
    h = h + gate * jnp.dot(p_ref[...].astype(BF16), wpp_ref[...], preferred_element_type=F32)
    if final_norm:
        h = _rms(h, gfin_ref[...])
    o_ref[...] = h


def _post(h, mixes, p_all, layer, wo, gf, wg, wu, wd, gp, wpg, wpp, gfin, tm, final_norm):
    T, D = h.shape
    row = lambda w: pl.BlockSpec((tm, w), lambda i: (i, 0))
    full = lambda a: pl.BlockSpec(a.shape, lambda i: (0, 0))
    lay = lambda a: pl.BlockSpec((None,) + a.shape[1:], lambda i: (layer, 0, 0),
                                 pipeline_mode=pl.Buffered(1))
    return pl.pallas_call(
        functools.partial(_post_kernel, n_mix=len(mixes), final_norm=final_norm),
        grid=(T // tm,),
        in_specs=[row(D)] + [row(m.shape[1]) for m in mixes]
        + [pl.BlockSpec((None, tm, p_all.shape[2]), lambda i: (layer, i, 0))]
        + [full(wo)] + [lay(a) for a in (gf, wg, wu, wd, gp, wpg, wpp)] + [full(gfin)],
        out_specs=row(D),
        out_shape=jax.ShapeDtypeStruct((T, D), F32),
        compiler_params=_params(("arbitrary",)),
        name="post",
    )(h, *mixes, p_all, wo, gf, wg, wu, wd, gp, wpg, wpp, gfin)


def _layer0_weights(w_in, b_f, n_fox):
    fw = n_fox * HEAD_DIM
    scale = HEAD_DIM ** -0.5 * LOG2E
    D = w_in.shape[0]
    qa, ka, va, qs, ks, vs = (w_in[:, i * fw:(i + 1) * fw] for i in range(6))
    wt = jnp.concatenate([qa * scale, va, qs * scale, vs], axis=1).T.astype(BF16)
    wk = jnp.concatenate([ka, ks], axis=1).astype(BF16)
    gate_pad = ((0, 0), (0, LANES - 3 * n_fox))
    wf = jnp.pad(jnp.tile(w_in[:, 6 * fw:], (1, 3)), gate_pad).astype(BF16)
    bf = jnp.pad(jnp.tile(b_f.reshape(1, n_fox), (1, 3)), gate_pad)
    heads = jnp.arange(n_fox)
    pk = jnp.zeros((LANES, n_fox // 2 * LANES), F32)
    aq = jnp.zeros((n_fox * FEAT_ROWS, LANES), F32)
    for piece in range(3):
        pk = pk.at[piece * n_fox + heads,
                   heads // 2 * LANES + heads % 2 * FEAT_ROWS + 3 + piece].set(-1.0)
        aq = aq.at[heads * FEAT_ROWS + piece, piece * n_fox + heads].set(1.0)
    return wt, wk, wf, bf, pk.astype(BF16), aq.astype(BF16)


def kernel(x, p, positions, norm_mix, norm_ffn, norm_ple, norm_final, ev_w_in, ev_b_f, ev_w_out,
           od_w_in, od_sinks, od_w_out, ffn_w_gate, ffn_w_up, ffn_w_down, ple_w_proj, ple_w_gate):
    B, S, D = x.shape
    T = B * S
    n_heads = D // HEAD_DIM
    n_fox = n_heads // 2
    fox_w = n_fox * HEAD_DIM
    n_q, n_kv = n_heads, 4
    assert S % BLK == 0 and n_fox == 8
    row = lambda a: a.reshape(1, -1)

    wt, wk, wf, bf, pk, aq = _layer0_weights(ev_w_in[0], ev_b_f[0], n_fox)
    tri = (jnp.arange(BLK)[None, :] > jnp.arange(BLK)[:, None]).astype(BF16)
    (qtf, vtf, kf, feat, gt, qn, fb, kmsq, o_sb) = _pre0(
        x, row(norm_mix[0]), wt, wk, wf, bf, pk, aq, tri, n_fox)
    fb_heads = fb[:, :, 0, :n_fox].transpose(0, 2, 1)
    o_fox = _fox(fb_heads, qtf, feat, kf, vtf, gt, qn, kmsq)

    tm = min(512, T)
    depth = norm_ffn.shape[0]
    p_all = p.reshape(depth, T, -1)
    stacked = (norm_ffn.reshape(depth, 1, D), ffn_w_gate.astype(BF16), ffn_w_up.astype(BF16),
               ffn_w_down.astype(BF16), norm_ple.reshape(depth, 1, D), ple_w_gate.astype(BF16),
               ple_w_proj.astype(BF16))
    h = _post(x.reshape(T, D), [o_fox.reshape(T, fox_w), o_sb.reshape(T, fox_w)], p_all, 0,
              ev_w_out[0].astype(BF16), *stacked, row(norm_final), tm, final_norm=False)

    qw = n_q * HEAD_DIM
    kw = n_kv * HEAD_DIM
    col_scale1 = jnp.ones((qw + 2 * kw,), F32).at[:qw].set(HEAD_DIM ** -0.5 * LOG2E)
    w1t = (od_w_in[0] * col_scale1).T.astype(BF16)
    half = HEAD_DIM // 2
    inv = (ROPE_THETA ** (-jnp.arange(half, dtype=F32) / half)).reshape(half, 1)
    qt1, k1, vt1 = _pre1(h.reshape(B, S, D), row(norm_mix[1]), w1t, positions.reshape(B, 1, S),
                         inv, min(tm, S), n_q, n_kv)
    o_swa = _swa(qt1, k1, vt1, od_sinks[0], n_q, n_kv)
    out = _post(h, [o_swa.reshape(T, qw)], p_all, 1, od_w_out[0].astype(BF16), *stacked,
                row(norm_final), tm, final_norm=True)
    return out.reshape(B, S, D)
```

```python
import functools

import jax
import jax.numpy as jnp
from jax import lax
from jax.experimental import pallas as pl
from jax.experimental.pallas import tpu as pltpu

F32 = jnp.float32
BF16 = jnp.bfloat16

HEAD_DIM = 64
LANES = 128
BLK = 256
V_ROWS = 80
FEAT_ROWS = 16
WINDOW = 128
ROPE_THETA = 10000.0
EPS = 1e-6
NEG_INF = -1e30
LOG2E = 1.4426950408889634
SKIP_LOG2 = 60.0 * LOG2E
FOX_HEADS = 8
SWA_SUB = 4
VMEM_LIMIT = 56 * 1024 * 1024

_NT = (((1,), (1,)), ((), ()))


def _params(sem):
    return pltpu.CompilerParams(dimension_semantics=sem, vmem_limit_bytes=VMEM_LIMIT)


def _rms(x, g):
    return x * lax.rsqrt(jnp.mean(x * x, axis=-1, keepdims=True) + EPS) * g


def _log_sigmoid(x):
    return jnp.minimum(x, 0.0) - jnp.log(1.0 + jnp.exp(-jnp.abs(x)))


def _split3(x):
    a = x.astype(BF16)
    r = x - a.astype(F32)
    b = r.astype(BF16)
    c = (r - b.astype(F32)).astype(BF16)
    return a, b, c


def _pre0_kernel(x_ref, g_ref, wt_ref, wk_ref, wf_ref, bf_ref, pk_ref, aq_ref, tri_ref,
                 qtf_ref, vtf_ref, kf_ref, feat_ref, gt_ref, qn_ref, fb_ref, kmsq_ref, osb_ref,
                 carry_ref, qts_s, ks_s, vts_s, r_ref, acc_ref, *, n_fox):
    tm = x_ref.shape[1]
    fw = n_fox * HEAD_DIM
    i = pl.program_id(1)
    nb = pl.num_programs(1) - 1
    extra = i == nb
    t = jnp.minimum(i, nb - 1)
    qb = jnp.maximum(i - 1, 0)
    heads = range(n_fox)

    @pl.when(i == 0)
    def _():
        carry_ref[...] = jnp.zeros_like(carry_ref)
        qts_s[...] = jnp.zeros_like(qts_s)
        ks_s[0:tm, :] = jnp.zeros((tm, fw), BF16)
        vts_s[:, 0] = jnp.zeros((n_fox, HEAD_DIM, tm), BF16)

    rowi = lax.broadcasted_iota(jnp.int32, (BLK, tm), 0)
    coli = lax.broadcasted_iota(jnp.int32, (BLK, tm), 1)
    strict = rowi < coli
    top = lax.broadcasted_iota(jnp.int32, (LANES, tm), 0) < HEAD_DIM
    zero = jnp.zeros((LANES, tm), BF16)
    qh = []
    for h in heads:
        pair = qts_s[(h // 2) * LANES:(h // 2 + 1) * LANES, :]
        qh.append(jnp.where(top, pair, zero) if h % 2 == 0 else jnp.where(top, zero, pair))
    tri_sb = tri_ref[...]

    def kpair(k0, h):
        return ks_s[pl.ds(k0, BLK), (h // 2) * LANES:(h // 2 + 1) * LANES]

    def stick_logs(z, mask=None):
        nz = -z
        l1 = jnp.minimum(nz, 0.0) - jnp.log2(1.0 + jnp.exp2(jnp.minimum(z, nz)))
        lb = z + l1
        if mask is not None:
            l1 = jnp.where(mask, l1, 0.0)
        return lb, l1.astype(BF16), l1[0:1, :]

    def suffix(l1b):
        return jnp.dot(tri_sb, l1b, preferred_element_type=F32)

    def tproj(c, half):
        r0 = c * fw + half * (fw // 2)
        return lax.dot_general(wt_ref[r0:r0 + fw // 2, :], hb, _NT,
                               preferred_element_type=F32).astype(BF16)

    jp = jnp.maximum(qb - 1, 0)
    kp0 = pl.multiple_of(jp * BLK, BLK)
    kd0 = pl.multiple_of(qb * BLK, BLK)
    has_prev = qb > 0
    cp = jnp.where(has_prev, 0.0, NEG_INF)

    hb = _rms(x_ref[0], g_ref[...]).astype(BF16)
    gate = jnp.dot(hb, wf_ref[...], preferred_element_type=F32) + bf_ref[...]
    lf = _log_sigmoid(gate) * LOG2E

    lp, ld, tp, sfp, sfd = [], [], [], [], []
    for h in heads:
        z2 = jnp.dot(jnp.concatenate([kpair(kp0, h), kpair(kd0, h)], axis=0), qh[h],
                     preferred_element_type=F32)
        lp.append(stick_logs(z2[0:BLK]))
        ld.append(stick_logs(z2[BLK:2 * BLK], strict))
        tp.append(tproj(h // 2, h % 2))
        if h >= 1:
            sfp.append(suffix(lp[h - 1][1]))
            sfd.append(suffix(ld[h - 1][1]))
    sfp.append(suffix(lp[n_fox - 1][1]))
    sfd.append(suffix(ld[n_fox - 1][1]))
    qtf = jnp.concatenate(tp[0:2], axis=0)
    vtf = jnp.concatenate(tp[2:4], axis=0)
    qts_new = jnp.concatenate(tp[4:6], axis=0)
    vts = jnp.concatenate(tp[6:8], axis=0)

    row = lax.broadcasted_iota(jnp.int32, (tm, tm), 0)
    col = lax.broadcasted_iota(jnp.int32, (tm, tm), 1)
    tri = jnp.where(row >= col, 1.0, 0.0).astype(BF16)
    G = jnp.zeros((tm, LANES), F32)
    for piece in _split3(lf):
        G = G + jnp.dot(tri, piece, preferred_element_type=F32)
    base = jnp.where(extra, carry_ref[1], carry_ref[0])
    fb_ref[0, 0] = base
    carry_ref[1] = base
    carry_ref[0] = base + G[tm - 1:tm, :]

    g_hi, g_mid, g_lo = _split3(G)
    lane_t = lax.broadcasted_iota(jnp.int32, (tm, LANES), 1)
    gp = jnp.where(lane_t < n_fox, g_hi, jnp.where(lane_t < 2 * n_fox, g_mid, g_lo))
    kfeat = jnp.dot(gp, pk_ref[...], preferred_element_type=F32)

    GT = G.T
    gt_ref[0] = GT[0:8, :]
    t_hi, t_mid, t_lo = _split3(GT)
    row_t = lax.broadcasted_iota(jnp.int32, (LANES, tm), 0)
    gpt = jnp.where(row_t < n_fox, t_hi, jnp.where(row_t < 2 * n_fox, t_mid, t_lo))
    qfeat = jnp.dot(aq_ref[...], gpt, preferred_element_type=F32)
    frow = lax.broadcasted_iota(jnp.int32, (n_fox * FEAT_ROWS, 1), 0) % FEAT_ROWS
    qones = jnp.where((frow >= 3) & (frow < 6), 1.0, 0.0)
    feat_ref[0] = (qfeat + qones).astype(BF16)

    a_first, kk = [], []
    nk = 2 * wk_ref.shape[1] // n_fox
    for h in heads:
        tot_d = sfd[h][0:1, :] + ld[h][2]
        tot_p = sfp[h][0:1, :] + lp[h][2]
        a_d = jnp.where(strict, jnp.exp2(ld[h][0] + sfd[h]), 0.0)
        a_p = jnp.exp2(lp[h][0] + sfp[h] + (tot_d + cp))
        a_first.append(jnp.concatenate([a_p, a_d], axis=0).astype(BF16))
        r_ref[h] = tot_d + jnp.where(has_prev, tot_p, 0.0)
        if h % 2 == 0:
            kk.append(jnp.dot(hb, wk_ref[:, (h // 2) * nk:(h // 2 + 1) * nk],
                              preferred_element_type=F32))
    kk = jnp.concatenate(kk, axis=1)
    for h in heads:
        vt2 = jnp.concatenate([vts_s[h, jp], vts_s[h, qb]], axis=1)
        acc_ref[h] = jnp.dot(vt2, a_first[h], preferred_element_type=F32)

    qtf_ref[0] = qtf
    lane_row = lax.broadcasted_iota(jnp.int32, (V_ROWS - HEAD_DIM, tm), 0)
    ones_pad = jnp.where(lane_row == 0, 1.0, 0.0).astype(BF16)
    for h in heads:
        vtf_ref[0, h, 0, 0:HEAD_DIM, :] = vtf[h * HEAD_DIM:(h + 1) * HEAD_DIM, :]
        vtf_ref[0, h, 0, HEAD_DIM:V_ROWS, :] = ones_pad
    qts_s[...] = qts_new
    for h in heads:
        vts_s[h, t] = vts[h * HEAD_DIM:(h + 1) * HEAD_DIM, :]

    q32 = qtf.astype(F32)
    qn_rows = [jnp.sqrt(jnp.sum(jnp.square(q32[h * HEAD_DIM:(h + 1) * HEAD_DIM, :]),
                                axis=0, keepdims=True)) for h in heads]
    qn_ref[0] = jnp.concatenate(qn_rows, axis=0)

    ks_s[pl.ds(pl.multiple_of(t * tm, tm), tm), :] = kk[:, fw:].astype(BF16)
    kfox = kk[:, :fw].astype(BF16)
    k32 = kfox.astype(F32)
    lane = lax.broadcasted_iota(jnp.int32, (1, LANES), 1)
    first_head = lane < HEAD_DIM
    kmsq = jnp.zeros((1, LANES), F32)
    for pr in range(n_fox // 2):
        sq = jnp.square(k32[:, pr * LANES:(pr + 1) * LANES])
        for e in range(2):
            mine = first_head if e == 0 else jnp.logical_not(first_head)
            ss = jnp.sum(jnp.where(mine, sq, 0.0), axis=-1, keepdims=True)
            kmsq = jnp.where(lane == 2 * pr + e, jnp.max(ss, axis=0, keepdims=True), kmsq)
    kmsq_ref[0, 0] = kmsq

    kones = jnp.where((lane % FEAT_ROWS < 3) & (lane < 2 * FEAT_ROWS), 1.0, 0.0)
    for pr in range(n_fox // 2):
        kf_ref[0, :, 2 * pr * LANES:(2 * pr + 1) * LANES] = kfox[:, pr * LANES:(pr + 1) * LANES]
        kf_ref[0, :, (2 * pr + 1) * LANES:(2 * pr + 2) * LANES] = (
            kfeat[:, pr * LANES:(pr + 1) * LANES] + kones).astype(BF16)

    def rmax():
        out = jnp.max(r_ref[0])
        for h in heads[1:]:
            out = jnp.maximum(out, jnp.max(r_ref[h]))
        return out

    def block(j):
        k0 = pl.multiple_of(j * BLK, BLK)
        z = [jnp.dot(kpair(k0, h), qh[h], preferred_element_type=F32) for h in heads]
        lg = [stick_logs(z[h]) for h in heads]
        sfx = [suffix(lg[h][1]) for h in heads]
        a = []
        for h in heads:
            r_old = r_ref[h]
            a.append(jnp.exp2(lg[h][0] + sfx[h] + r_old).astype(BF16))
            r_ref[h] = r_old + (sfx[h][0:1, :] + lg[h][2])
        for h in heads:
            acc_ref[h] = acc_ref[h] + jnp.dot(vts_s[h, j], a[h], preferred_element_type=F32)
        return rmax()

    def body(carry):
        j, _ = carry
        return j - 1, block(j)

    lax.while_loop(lambda c: (c[0] >= 0) & (c[1] > -SKIP_LOG2), body, (qb - 2, rmax()))

    osb_ref[0] = jnp.concatenate([acc_ref[h] for h in heads], axis=0).T.astype(osb_ref.dtype)


def _pre0(x, g, wt, wk, wf, bf, pk, aq, tri, n_fox):
    B, S, D = x.shape
    tm = BLK
    nb = S // tm
    fw = n_fox * HEAD_DIM
    const = lambda a: pl.BlockSpec(a.shape, lambda b, s: (0,) * a.ndim)
    tile = lambda s: jnp.minimum(s, nb - 1)
    tok_lane = lambda rows: pl.BlockSpec((1, rows, tm), lambda b, s: (b, 0, tile(s)))
    return pl.pallas_call(
        functools.partial(_pre0_kernel, n_fox=n_fox),
        grid=(B, nb + 1),
        in_specs=[pl.BlockSpec((1, tm, D), lambda b, s: (b, tile(s), 0))]
        + [const(a) for a in (g, wt, wk, wf, bf, pk, aq, tri)],
        out_specs=[
            tok_lane(fw),
            pl.BlockSpec((1, n_fox, 1, V_ROWS, tm), lambda b, s: (b, 0, tile(s), 0, 0)),
            pl.BlockSpec((1, tm, n_fox * LANES), lambda b, s: (b, tile(s), 0)),
            tok_lane(n_fox * FEAT_ROWS),
            tok_lane(8),
            tok_lane(8),
            pl.BlockSpec((1, 1, 1, LANES), lambda b, s: (b, tile(s), 0, 0)),
            pl.BlockSpec((1, 1, 1, LANES), lambda b, s: (b, tile(s), 0, 0)),
            pl.BlockSpec((1, tm, fw), lambda b, s: (b, jnp.maximum(s - 1, 0), 0)),
        ],
        out_shape=[
            jax.ShapeDtypeStruct((B, fw, S), BF16),
            jax.ShapeDtypeStruct((B, n_fox, nb, V_ROWS, tm), BF16),
            jax.ShapeDtypeStruct((B, S, n_fox * LANES), BF16),
            jax.ShapeDtypeStruct((B, n_fox * FEAT_ROWS, S), BF16),
            jax.ShapeDtypeStruct((B, 8, S), F32),
            jax.ShapeDtypeStruct((B, 8, S), F32),
            jax.ShapeDtypeStruct((B, nb, 1, LANES), F32),
            jax.ShapeDtypeStruct((B, nb, 1, LANES), F32),
            jax.ShapeDtypeStruct((B, S, fw), BF16),
        ],
        scratch_shapes=[
            pltpu.VMEM((2, 1, LANES), F32),
            pltpu.VMEM((fw, tm), BF16),
            pltpu.VMEM((S, fw), BF16),
            pltpu.VMEM((n_fox, nb, HEAD_DIM, tm), BF16),
            pltpu.VMEM((n_fox, 1, tm), F32),
            pltpu.VMEM((n_fox, HEAD_DIM, tm), F32),
        ],
        compiler_params=_params(("arbitrary", "arbitrary")),
        name="pre0",
    )(x, g, wt, wk, wf, bf, pk, aq, tri)


def _fox_kernel(fb_ref, qt_ref, feat_ref, k_ref, vt_ref, gt_ref, qn_ref, kmsq_ref,
                o_ref, m_ref, acc_ref):
    b = pl.program_id(0)
    hp = pl.program_id(1)
    qi = pl.program_id(2)
    bq = qt_ref.shape[2]
    n_h = acc_ref.shape[0]
    heads = range(n_h)
    hg = [n_h * hp + h for h in heads]
    row = lax.broadcasted_iota(jnp.int32, (BLK, bq), 0)
    col = lax.broadcasted_iota(jnp.int32, (BLK, bq), 1)
    causal = row <= col
    kmax_sq = jnp.max(kmsq_ref[0], axis=0)
    lane = lax.broadcasted_iota(jnp.int32, (1, LANES), 1)

    def zeros(rows):
        return jnp.zeros((rows, bq), BF16)

    qaug = []
    for h in heads:
        q = qt_ref[0, h * HEAD_DIM:(h + 1) * HEAD_DIM, :]
        f = feat_ref[0, h * FEAT_ROWS:(h + 1) * FEAT_ROWS, :]
        parts = [q, zeros(HEAD_DIM), f, zeros(FEAT_ROWS)] if h % 2 == 0 else \
                [zeros(HEAD_DIM), q, zeros(FEAT_ROWS), f]
        qaug.append(jnp.concatenate(parts + [zeros(LANES - 2 * FEAT_ROWS)], axis=0))

    def kpair(k0, h):
        return k_ref[0, pl.ds(k0, BLK), (h // 2) * 2 * LANES:(h // 2 + 1) * 2 * LANES]

    jp = jnp.maximum(qi - 1, 0)
    kp0 = pl.multiple_of(jp * BLK, BLK)
    kd0 = pl.multiple_of(qi * BLK, BLK)
    st = [jnp.dot(jnp.concatenate([kpair(kp0, h), kpair(kd0, h)], axis=0),
                  qaug[h], preferred_element_type=F32) for h in heads]
    p_first = []
    for h in heads:
        cp = jnp.where(qi > 0, fb_ref[b, hg[h], qi] - fb_ref[b, hg[h], jp], NEG_INF)
        st_p = st[h][0:BLK]
        st_d = jnp.where(causal, st[h][BLK:2 * BLK], NEG_INF)
        m = jnp.maximum(jnp.max(st_d, axis=0, keepdims=True),
                        jnp.max(st_p, axis=0, keepdims=True) + cp)
        m_ref[h] = m
        p_first.append(jnp.concatenate([jnp.exp2(st_p - (m - cp)), jnp.exp2(st_d - m)],
                                       axis=0).astype(BF16))
    for h in heads:
        vt2 = jnp.concatenate([vt_ref[0, h, jp], vt_ref[0, h, qi]], axis=1)
        acc_ref[h] = jnp.dot(vt2, p_first[h], preferred_element_type=F32)

    def block(j):
        k0 = pl.multiple_of(j * BLK, BLK)
        st = [jnp.dot(kpair(k0, h), qaug[h], preferred_element_type=F32)
              for h in heads]
        p, alpha = [], []
        for h in heads:
            c = fb_ref[b, hg[h], qi] - fb_ref[b, hg[h], j]
            m_old = m_ref[h]
            m_new = jnp.maximum(m_old, jnp.max(st[h], axis=0, keepdims=True) + c)
            p.append(jnp.exp2(st[h] - (m_new - c)).astype(BF16))
            alpha.append(jnp.exp2(m_old - m_new))
            m_ref[h] = m_new
        for h in heads:
            acc_ref[h] = alpha[h] * acc_ref[h] + jnp.dot(vt_ref[0, h, j], p[h],
                                                         preferred_element_type=F32)

    th = []
    for h in heads:
        kmax = jnp.sqrt(jnp.sum(jnp.where(lane == hg[h], kmax_sq, 0.0), axis=-1, keepdims=True))
        bound = qn_ref[0, pl.ds(hg[h], 1), :] * kmax + gt_ref[0, pl.ds(hg[h], 1), :] - m_ref[h]
        th.append(jnp.max(bound) + SKIP_LOG2)

    def needed(j):
        jn = jnp.clip(j + 1, 0, qi)
        need = fb_ref[b, hg[0], jn] - fb_ref[b, hg[0], qi] <= th[0]
        for h in heads[1:]:
            need = need | (fb_ref[b, hg[h], jn] - fb_ref[b, hg[h], qi] <= th[h])
        return (j >= 0) & need

    def body(j):
        block(j)
        return j - 1

    lax.while_loop(needed, body, qi - 2)

    out_t = jnp.concatenate(
        [acc_ref[h, 0:HEAD_DIM, :] / acc_ref[h, HEAD_DIM:HEAD_DIM + 1, :] for h in heads], axis=0)
    o_ref[0] = out_t.T.astype(o_ref.dtype)


def _fox(fb, qt, feat, kf, vt, gt, qn, kmsq):
    B, _, S = qt.shape
    n_h = FOX_HEADS
    n_grp = qt.shape[1] // (n_h * HEAD_DIM)
    nb = S // BLK
    return pl.pallas_call(
        _fox_kernel,
        grid_spec=pltpu.PrefetchScalarGridSpec(
            num_scalar_prefetch=1,
            grid=(B, n_grp, nb),
            in_specs=[
                pl.BlockSpec((1, n_h * HEAD_DIM, BLK), lambda b, h, i, s: (b, h, i)),
                pl.BlockSpec((1, n_h * FEAT_ROWS, BLK), lambda b, h, i, s: (b, h, i)),
                pl.BlockSpec((1, S, n_h * LANES), lambda b, h, i, s: (b, 0, h),
                             pipeline_mode=pl.Buffered(1)),
                pl.BlockSpec((1, n_h, nb, V_ROWS, BLK), lambda b, h, i, s: (b, h, 0, 0, 0),
                             pipeline_mode=pl.Buffered(1)),
                pl.BlockSpec((1, 8, BLK), lambda b, h, i, s: (b, 0, i)),
                pl.BlockSpec((1, 8, BLK), lambda b, h, i, s: (b, 0, i)),
                pl.BlockSpec((1, nb, 1, LANES), lambda b, h, i, s: (b, 0, 0, 0)),
            ],
            out_specs=pl.BlockSpec((1, BLK, n_h * HEAD_DIM), lambda b, h, i, s: (b, i, h)),
            scratch_shapes=[
                pltpu.VMEM((n_h, 1, BLK), F32),
                pltpu.VMEM((n_h, V_ROWS, BLK), F32),
            ],
        ),
        out_shape=jax.ShapeDtypeStruct((B, S, n_grp * n_h * HEAD_DIM), BF16),
        compiler_params=_params(("arbitrary", "arbitrary", "arbitrary")),
        name="fox",
    )(fb, qt, feat, kf, vt, gt, qn, kmsq)


def _pre1_kernel(x_ref, g_ref, wt_ref, pos_ref, inv_ref, qt_ref, k_ref, vt_ref, *, n_q, n_kv):
    tm = x_ref.shape[1]
    half = HEAD_DIM // 2
    hb = _rms(x_ref[0], g_ref[...]).astype(BF16)
    ang = inv_ref[...] * pos_ref[0].astype(F32)
    cos = jnp.cos(ang)
    sin = jnp.sin(ang)

    def proj_t(r0, rows):
        return lax.dot_general(wt_ref[r0:r0 + rows, :], hb, _NT, preferred_element_type=F32)

    def rope_t(x):
        x1, x2 = x[0:half], x[half:HEAD_DIM]
        return jnp.concatenate([x1 * cos - x2 * sin, x2 * cos + x1 * sin], axis=0)

    qw = n_q * HEAD_DIM
    for c in range(n_q // 4):
        qt = proj_t(c * 4 * HEAD_DIM, 4 * HEAD_DIM)
        for a in range(4):
            hq = c * 4 + a
            qt_ref[0, hq * HEAD_DIM:(hq + 1) * HEAD_DIM, :] = rope_t(
                qt[a * HEAD_DIM:(a + 1) * HEAD_DIM]).astype(BF16)
    kt = proj_t(qw, n_kv * HEAD_DIM)
    zpad = jnp.zeros((LANES - HEAD_DIM, tm), F32)
    for g in range(n_kv):
        kg = jnp.concatenate([rope_t(kt[g * HEAD_DIM:(g + 1) * HEAD_DIM]), zpad], axis=0)
        k_ref[0, :, g * LANES:(g + 1) * LANES] = kg.T.astype(BF16)
    vt = proj_t(qw + n_kv * HEAD_DIM, n_kv * HEAD_DIM).astype(BF16)
    pad_row = lax.broadcasted_iota(jnp.int32, (V_ROWS - HEAD_DIM, tm), 0)
    ones_pad = jnp.where(pad_row == 0, 1.0, 0.0).astype(BF16)
    for g in range(n_kv):
        vt_ref[0, g * V_ROWS:g * V_ROWS + HEAD_DIM, :] = vt[g * HEAD_DIM:(g + 1) * HEAD_DIM]
        vt_ref[0, g * V_ROWS + HEAD_DIM:(g + 1) * V_ROWS, :] = ones_pad


def _pre1(h, g, wt, pos, inv, tm, n_q, n_kv):
    B, S, D = h.shape
    return pl.pallas_call(
        functools.partial(_pre1_kernel, n_q=n_q, n_kv=n_kv),
        grid=(B, S // tm),
        in_specs=[
            pl.BlockSpec((1, tm, D), lambda b, i: (b, i, 0)),
            pl.BlockSpec((1, D), lambda b, i: (0, 0)),
            pl.BlockSpec(wt.shape, lambda b, i: (0, 0)),
            pl.BlockSpec((1, 1, tm), lambda b, i: (b, 0, i)),
            pl.BlockSpec(inv.shape, lambda b, i: (0, 0)),
        ],
        out_specs=[
            pl.BlockSpec((1, n_q * HEAD_DIM, tm), lambda b, i: (b, 0, i)),
            pl.BlockSpec((1, tm, n_kv * LANES), lambda b, i: (b, i, 0)),
            pl.BlockSpec((1, n_kv * V_ROWS, tm), lambda b, i: (b, 0, i)),
        ],
        out_shape=[
            jax.ShapeDtypeStruct((B, n_q * HEAD_DIM, S), BF16),
            jax.ShapeDtypeStruct((B, S, n_kv * LANES), BF16),
            jax.ShapeDtypeStruct((B, n_kv * V_ROWS, S), BF16),
        ],
        compiler_params=_params(("arbitrary", "arbitrary")),
        name="pre1",
    )(h, g, wt, pos, inv)


def _swa_kernel(sink_ref, qt_ref, kp_ref, ko_ref, vp_ref, vo_ref, o_ref, *, n_kv, group):
    i = pl.program_id(1)
    W = WINDOW
    n_sub = qt_ref.shape[2] // W
    r = lax.broadcasted_iota(jnp.int32, (2 * W, W), 0)
    c = lax.broadcasted_iota(jnp.int32, (2 * W, W), 1)
    rel = c + W - r
    band = (rel >= 0) & (rel < W)
    valid = [jnp.concatenate([band & ((r >= W) | (i > 0)) if u == 0 else band] * group, axis=1)
             for u in range(n_sub)]
    seg = lax.broadcasted_iota(jnp.int32, (1, group * W), 1) // W
    zpad = jnp.zeros((LANES - HEAD_DIM, group * W), BF16)
    chains = [(u, g) for u in range(n_sub) for g in range(n_kv)]

    def keys(u, g):
        ls = slice(g * LANES, (g + 1) * LANES)
        if u == 0:
            return jnp.concatenate([kp_ref[0, :, ls], ko_ref[0, 0:W, ls]], axis=0)
        return ko_ref[0, (u - 1) * W:(u + 1) * W, ls]

    def values(u, g):
        rs = slice(g * V_ROWS, (g + 1) * V_ROWS)
        if u == 0:
            return jnp.concatenate([vp_ref[0, rs, :], vo_ref[0, rs, 0:W]], axis=1)
        return vo_ref[0, rs, (u - 1) * W:(u + 1) * W]

    st = []
    for u, g in chains:
        qg = jnp.concatenate(
            [qt_ref[0, (g * group + a) * HEAD_DIM:(g * group + a + 1) * HEAD_DIM, u * W:(u + 1) * W]
             for a in range(group)], axis=1)
        st.append(jnp.dot(keys(u, g), jnp.concatenate([qg, zpad], axis=0),
                          preferred_element_type=F32))
    p, sink_term = [], []
    for n, (u, g) in enumerate(chains):
        sg = jnp.where(valid[u], st[n], NEG_INF)
        sink = jnp.zeros((1, group * W), F32)
        for a in range(group):
            sink = jnp.where(seg == a, sink_ref[g * group + a] * LOG2E, sink)
        m = jnp.maximum(jnp.max(sg, axis=0, keepdims=True), sink)
        p.append(jnp.exp2(sg - m).astype(BF16))
        sink_term.append(jnp.exp2(sink - m))
    acc = [jnp.dot(values(u, g), p[n], preferred_element_type=F32)
           for n, (u, g) in enumerate(chains)]
    for n, (u, g) in enumerate(chains):
        o = acc[n][0:HEAD_DIM] / (acc[n][HEAD_DIM:HEAD_DIM + 1] + sink_term[n])
        for a in range(0, group, 2):
            pair = jnp.concatenate([o[:, a * W:(a + 1) * W], o[:, (a + 1) * W:(a + 2) * W]], axis=0)
            l0 = (g * group + a) * HEAD_DIM
            o_ref[0, u * W:(u + 1) * W, l0:l0 + 2 * HEAD_DIM] = pair.T.astype(o_ref.dtype)


def _swa(qt, kpad, vt, sinks, n_q, n_kv):
    B, _, S = qt.shape
    W = WINDOW
    n_sub = SWA_SUB
    prev = lambda i: jnp.maximum(n_sub * i - 1, 0)
    return pl.pallas_call(
        functools.partial(_swa_kernel, n_kv=n_kv, group=n_q // n_kv),
        grid_spec=pltpu.PrefetchScalarGridSpec(
            num_scalar_prefetch=1,
            grid=(B, S // (n_sub * W)),
            in_specs=[
                pl.BlockSpec((1, n_q * HEAD_DIM, n_sub * W), lambda b, i, s: (b, 0, i)),
                pl.BlockSpec((1, W, n_kv * LANES), lambda b, i, s: (b, prev(i), 0)),
                pl.BlockSpec((1, n_sub * W, n_kv * LANES), lambda b, i, s: (b, i, 0)),
                pl.BlockSpec((1, n_kv * V_ROWS, W), lambda b, i, s: (b, 0, prev(i))),
                pl.BlockSpec((1, n_kv * V_ROWS, n_sub * W), lambda b, i, s: (b, 0, i)),
            ],
            out_specs=pl.BlockSpec((1, n_sub * W, n_q * HEAD_DIM), lambda b, i, s: (b, i, 0)),
        ),
        out_shape=jax.ShapeDtypeStruct((B, S, n_q * HEAD_DIM), BF16),
        compiler_params=_params(("arbitrary", "arbitrary")),
        name="swa",
    )(sinks, qt, kpad, kpad, vt, vt)


def _post_kernel(*refs, n_mix, final_norm):
    h_ref = refs[0]
    mix_refs = refs[1:1 + n_mix]
    (p_ref, wo_ref, gf_ref, wg_ref, wu_ref, wd_ref, gp_ref, wpg_ref, wpp_ref, gfin_ref,
     o_ref) = refs[1 + n_mix:]
    h = h_ref[...]
    off = 0
    for m_ref in mix_refs:
        w = m_ref.shape[1]
        h = h + jnp.dot(m_ref[...], wo_ref[off:off + w, :], preferred_element_type=F32)
        off += w
    hb = _rms(h, gf_ref[...]).astype(BF16)
    g = jnp.dot(hb, wg_ref[...], preferred_element_type=F32)
    u = jnp.dot(hb, wu_ref[...], preferred_element_type=F32)
    act = (g * jax.nn.sigmoid(g) * u).astype(BF16)
    h = h + jnp.dot(act, wd_ref[...], preferred_element_type=F32)
    gate = jax.nn.sigmoid(jnp.dot(_rms(h, gp_ref[...]).astype(BF16), wpg_ref[...],
                                  preferred_element_type=F32))
    h = h + gate * jnp.dot(p_ref[...].astype(BF16), wpp_ref[...], preferred_element_type=F32)
    if final_norm:
        h = _rms(h, gfin_ref[...])
    o_ref[...] = h


def _post(h, mixes, p_all, layer, wo, gf, wg, wu, wd, gp, wpg, wpp, gfin, tm, final_norm):
    T, D = h.shape
    row = lambda w: pl.BlockSpec((tm, w), lambda i: (i, 0))
    full = lambda a: pl.BlockSpec(a.shape, lambda i: (0, 0))
    lay = lambda a: pl.BlockSpec((None,) + a.shape[1:], lambda i: (layer, 0, 0),
                                 pipeline_mode=pl.Buffered(1))
    return pl.pallas_call(
        functools.partial(_post_kernel, n_mix=len(mixes), final_norm=final_norm),
        grid=(T // tm,),
        in_specs=[row(D)] + [row(m.shape[1]) for m in mixes]
        + [pl.BlockSpec((None, tm, p_all.shape[2]), lambda i: (layer, i, 0))]
        + [full(wo)] + [lay(a) for a in (gf, wg, wu, wd, gp, wpg, wpp)] + [full(gfin)],
        out_specs=row(D),
        out_shape=jax.ShapeDtypeStruct((T, D), F32),
        compiler_params=_params(("arbitrary",)),
        name="post",
    )(h, *mixes, p_all, wo, gf, wg, wu, wd, gp, wpg, wpp, gfin)


def _layer0_weights(w_in, b_f, n_fox):
    fw = n_fox * HEAD_DIM
    scale = HEAD_DIM ** -0.5 * LOG2E
    D = w_in.shape[0]
    qa, ka, va, qs, ks, vs = (w_in[:, i * fw:(i + 1) * fw] for i in range(6))
    wt = jnp.concatenate([qa * scale, va, qs * scale, vs], axis=1).T.astype(BF16)
    wk = jnp.concatenate([ka, ks], axis=1).astype(BF16)
    gate_pad = ((0, 0), (0, LANES - 3 * n_fox))
    wf = jnp.pad(jnp.tile(w_in[:, 6 * fw:], (1, 3)), gate_pad).astype(BF16)
    bf = jnp.pad(jnp.tile(b_f.reshape(1, n_fox), (1, 3)), gate_pad)
    heads = jnp.arange(n_fox)
    pk = jnp.zeros((LANES, n_fox // 2 * LANES), F32)
    aq = jnp.zeros((n_fox * FEAT_ROWS, LANES), F32)
    for piece in range(3):
        pk = pk.at[piece * n_fox + heads,
                   heads // 2 * LANES + heads % 2 * FEAT_ROWS + 3 + piece].set(-1.0)
        aq = aq.at[heads * FEAT_ROWS + piece, piece * n_fox + heads].set(1.0)
    return wt, wk, wf, bf, pk.astype(BF16), aq.astype(BF16)


def kernel(x, p, positions, norm_mix, norm_ffn, norm_ple, norm_final, ev_w_in, ev_b_f, ev_w_out,
           od_w_in, od_sinks, od_w_out, ffn_w_gate, ffn_w_up, ffn_w_down, ple_w_proj, ple_w_gate):
    B, S, D = x.shape
    T = B * S
    n_heads = D // HEAD_DIM
    n_fox = n_heads // 2
    fox_w = n_fox * HEAD_DIM
    n_q, n_kv = n_heads, 4
    assert S % BLK == 0 and n_fox == 8
    row = lambda a: a.reshape(1, -1)

    wt, wk, wf, bf, pk, aq = _layer0_weights(ev_w_in[0], ev_b_f[0], n_fox)
    tri = (jnp.arange(BLK)[None, :] > jnp.arange(BLK)[:, None]).astype(BF16)
    (qtf, vtf, kf, feat, gt, qn, fb, kmsq, o_sb) = _pre0(
        x, row(norm_mix[0]), wt, wk, wf, bf, pk, aq, tri, n_fox)
    fb_heads = fb[:, :, 0, :n_fox].transpose(0, 2, 1)
    o_fox = _fox(fb_heads, qtf, feat, kf, vtf, gt, qn, kmsq)

    tm = min(512, T)
    depth = norm_ffn.shape[0]
    p_all = p.reshape(depth, T, -1)
    stacked = (norm_ffn.reshape(depth, 1, D), ffn_w_gate.astype(BF16), ffn_w_up.astype(BF16),
               ffn_w_down.astype(BF16), norm_ple.reshape(depth, 1, D), ple_w_gate.astype(BF16),
               ple_w_proj.astype(BF16))
    h = _post(x.reshape(T, D), [o_fox.reshape(T, fox_w), o_sb.reshape(T, fox_w)], p_all, 0,
              ev_w_out[0].astype(BF16), *stacked, row(norm_final), tm, final_norm=False)

    qw = n_q * HEAD_DIM
    kw = n_kv * HEAD_DIM
    col_scale1 = jnp.ones((qw + 2 * kw,), F32).at[:qw].set(HEAD_DIM ** -0.5 * LOG2E)
    w1t = (od_w_in[0] * col_scale1).T.astype(BF16)
    half = HEAD_DIM // 2
    inv = (ROPE_THETA ** (-jnp.arange(half, dtype=F32) / half)).reshape(half, 1)
    qt1, k1, vt1 = _pre1(h.reshape(B, S, D), row(norm_mix[1]), w1t, positions.reshape(B, 1, S),
                         inv, min(tm, S), n_q, n_kv)
    o_swa = _swa(qt1, k1, vt1, od_sinks[0], n_q, n_kv)
    out = _post(h, [o_swa.reshape(T, qw)], p_all, 1, od_w_out[0].astype(BF16), *stacked,
                row(norm_final), tm, final_norm=True)
    return out.reshape(B, S, D)
```

```python
import functools

import jax
import jax.numpy as jnp
from jax import lax
from jax.experimental import pallas as pl
from jax.experimental.pallas import tpu as pltpu

F32 = jnp.float32
BF16 = jnp.bfloat16

HEAD_DIM = 64
LANES = 128
BLK = 256
V_ROWS = 80
FEAT_ROWS = 16
WINDOW = 128
ROPE_THETA = 10000.0
EPS = 1e-6
NEG_INF = -1e30
LOG2E = 1.4426950408889634
SKIP_LOG2 = 60.0 * LOG2E
FOX_HEADS = 8
SWA_SUB = 4
VMEM_LIMIT = 56 * 1024 * 1024

_NT = (((1,), (1,)), ((), ()))


def _params(sem):
    return pltpu.CompilerParams(dimension_semantics=sem, vmem_limit_bytes=VMEM_LIMIT)


def _rms(x, g):
    return x * lax.rsqrt(jnp.mean(x * x, axis=-1, keepdims=True) + EPS) * g


def _log_sigmoid(x):
    return jnp.minimum(x, 0.0) - jnp.log(1.0 + jnp.exp(-jnp.abs(x)))


def _split3(x):
    a = x.astype(BF16)
    r = x - a.astype(F32)
    b = r.astype(BF16)
    c = (r - b.astype(F32)).astype(BF16)
    return a, b, c


def _pre0_kernel(x_ref, g_ref, wt_ref, wk_ref, wf_ref, bf_ref, pk_ref, aq_ref, tri_ref,
                 qtf_ref, vtf_ref, kf_ref, feat_ref, gt_ref, qn_ref, fb_ref, kmsq_ref, osb_ref,
                 carry_ref, qts_s, ks_s, vts_s, r_ref, acc_ref, *, n_fox):
    tm = x_ref.shape[1]
    fw = n_fox * HEAD_DIM
    i = pl.program_id(1)
    nb = pl.num_programs(1) - 1
    extra = i == nb
    t = jnp.minimum(i, nb - 1)
    qb = jnp.maximum(i - 1, 0)
    heads = range(n_fox)

    @pl.when(i == 0)
    def _():
        carry_ref[...] = jnp.zeros_like(carry_ref)
        qts_s[...] = jnp.zeros_like(qts_s)
        ks_s[0:tm, :] = jnp.zeros((tm, fw), BF16)
        vts_s[:, 0] = jnp.zeros((n_fox, HEAD_DIM, tm), BF16)

    rowi = lax.broadcasted_iota(jnp.int32, (BLK, tm), 0)
    coli = lax.broadcasted_iota(jnp.int32, (BLK, tm), 1)
    strict = rowi < coli
    top = lax.broadcasted_iota(jnp.int32, (LANES, tm), 0) < HEAD_DIM
    zero = jnp.zeros((LANES, tm), BF16)
    qh = []
    for h in heads:
        pair = qts_s[(h // 2) * LANES:(h // 2 + 1) * LANES, :]
        qh.append(jnp.where(top, pair, zero) if h % 2 == 0 else jnp.where(top, zero, pair))
    tri_sb = tri_ref[...]

    def kpair(k0, h):
        return ks_s[pl.ds(k0, BLK), (h // 2) * LANES:(h // 2 + 1) * LANES]

    def stick_logs(z, mask=None):
        nz = -z
        l1 = jnp.minimum(nz, 0.0) - jnp.log2(1.0 + jnp.exp2(jnp.minimum(z, nz)))
        lb = z + l1
        if mask is not None:
            l1 = jnp.where(mask, l1, 0.0)
        return lb, l1.astype(BF16), l1[0:1, :]

    def suffix(l1b):
        return jnp.dot(tri_sb, l1b, preferred_element_type=F32)

    def tproj(c, half):
        r0 = c * fw + half * (fw // 2)
        return lax.dot_general(wt_ref[r0:r0 + fw // 2, :], hb, _NT,
                               preferred_element_type=F32).astype(BF16)

    jp = jnp.maximum(qb - 1, 0)
    kp0 = pl.multiple_of(jp * BLK, BLK)
    kd0 = pl.multiple_of(qb * BLK, BLK)
    has_prev = qb > 0
    cp = jnp.where(has_prev, 0.0, NEG_INF)

    hb = _rms(x_ref[0], g_ref[...]).astype(BF16)
    gate = jnp.dot(hb, wf_ref[...], preferred_element_type=F32) + bf_ref[...]
    lf = _log_sigmoid(gate) * LOG2E

    def gate_stage(k, c):
        if k == 0:
            row = lax.broadcasted_iota(jnp.int32, (tm, tm), 0)
            col = lax.broadcasted_iota(jnp.int32, (tm, tm), 1)
            tri = jnp.where(row >= col, 1.0, 0.0).astype(BF16)
            G = jnp.zeros((tm, LANES), F32)
            for piece in _split3(lf):
                G = G + jnp.dot(tri, piece, preferred_element_type=F32)
            base = jnp.where(extra, carry_ref[1], carry_ref[0])
            fb_ref[0, 0] = base
            carry_ref[1] = base
            carry_ref[0] = base + G[tm - 1:tm, :]
            c["G"] = G
        elif k == 1:
            g_hi, g_mid, g_lo = _split3(c["G"])
            lane_t = lax.broadcasted_iota(jnp.int32, (tm, LANES), 1)
            gp = jnp.where(lane_t < n_fox, g_hi, jnp.where(lane_t < 2 * n_fox, g_mid, g_lo))
            c["kfeat"] = jnp.dot(gp, pk_ref[...], preferred_element_type=F32)
        elif k == 2:
            GT = c["G"].T
            gt_ref[0] = GT[0:8, :]
            t_hi, t_mid, t_lo = _split3(GT)
            row_t = lax.broadcasted_iota(jnp.int32, (LANES, tm), 0)
            c["gpt"] = jnp.where(row_t < n_fox, t_hi, jnp.where(row_t < 2 * n_fox, t_mid, t_lo))
        elif k == 3:
            qfeat = jnp.dot(aq_ref[...], c["gpt"], preferred_element_type=F32)
            frow = lax.broadcasted_iota(jnp.int32, (n_fox * FEAT_ROWS, 1), 0) % FEAT_ROWS
            qones = jnp.where((frow >= 3) & (frow < 6), 1.0, 0.0)
            feat_ref[0] = (qfeat + qones).astype(BF16)

    lp, ld, tp, sfp, sfd, chain = [], [], [], [], [], {}
    for h in heads:
        z2 = jnp.dot(jnp.concatenate([kpair(kp0, h), kpair(kd0, h)], axis=0), qh[h],
                     preferred_element_type=F32)
        lp.append(stick_logs(z2[0:BLK]))
        ld.append(stick_logs(z2[BLK:2 * BLK], strict))
        tp.append(tproj(h // 2, h % 2))
        if h >= 1:
            sfp.append(suffix(lp[h - 1][1]))
            sfd.append(suffix(ld[h - 1][1]))
        if h % 2 == 0:
            gate_stage(h // 2, chain)
    sfp.append(suffix(lp[n_fox - 1][1]))
    sfd.append(suffix(ld[n_fox - 1][1]))
    qtf = jnp.concatenate(tp[0:2], axis=0)
    vtf = jnp.concatenate(tp[2:4], axis=0)
    qts_new = jnp.concatenate(tp[4:6], axis=0)
    vts = jnp.concatenate(tp[6:8], axis=0)
    kfeat = chain["kfeat"]

    a_first, kk = [], []
    nk = 2 * wk_ref.shape[1] // n_fox
    for h in heads:
        tot_d = sfd[h][0:1, :] + ld[h][2]
        tot_p = sfp[h][0:1, :] + lp[h][2]
        a_d = jnp.where(strict, jnp.exp2(ld[h][0] + sfd[h]), 0.0)
        a_p = jnp.exp2(lp[h][0] + sfp[h] + (tot_d + cp))
        a_first.append(jnp.concatenate([a_p, a_d], axis=0).astype(BF16))
        r_ref[h] = tot_d + jnp.where(has_prev, tot_p, 0.0)
        if h % 2 == 0:
            kk.append(jnp.dot(hb, wk_ref[:, (h // 2) * nk:(h // 2 + 1) * nk],
                              preferred_element_type=F32))
    kk = jnp.concatenate(kk, axis=1)
    for h in heads:
        vt2 = jnp.concatenate([vts_s[h, jp], vts_s[h, qb]], axis=1)
        acc_ref[h] = jnp.dot(vt2, a_first[h], preferred_element_type=F32)

    qtf_ref[0] = qtf
    lane_row = lax.broadcasted_iota(jnp.int32, (V_ROWS - HEAD_DIM, tm), 0)
    ones_pad = jnp.where(lane_row == 0, 1.0, 0.0).astype(BF16)
    for h in heads:
        vtf_ref[0, h, 0, 0:HEAD_DIM, :] = vtf[h * HEAD_DIM:(h + 1) * HEAD_DIM, :]
        vtf_ref[0, h, 0, HEAD_DIM:V_ROWS, :] = ones_pad
    qts_s[...] = qts_new
    for h in heads:
        vts_s[h, t] = vts[h * HEAD_DIM:(h + 1) * HEAD_DIM, :]

    q32 = qtf.astype(F32)
    qn_rows = [jnp.sqrt(jnp.sum(jnp.square(q32[h * HEAD_DIM:(h + 1) * HEAD_DIM, :]),
                                axis=0, keepdims=True)) for h in heads]
    qn_ref[0] = jnp.concatenate(qn_rows, axis=0)

    ks_s[pl.ds(pl.multiple_of(t * tm, tm), tm), :] = kk[:, fw:].astype(BF16)
    kfox = kk[:, :fw].astype(BF16)
    k32 = kfox.astype(F32)
    lane = lax.broadcasted_iota(jnp.int32, (1, LANES), 1)
    first_head = lane < HEAD_DIM
    kmsq = jnp.zeros((1, LANES), F32)
    for pr in range(n_fox // 2):
        sq = jnp.square(k32[:, pr * LANES:(pr + 1) * LANES])
        for e in range(2):
            mine = first_head if e == 0 else jnp.logical_not(first_head)
            ss = jnp.sum(jnp.where(mine, sq, 0.0), axis=-1, keepdims=True)
            kmsq = jnp.where(lane == 2 * pr + e, jnp.max(ss, axis=0, keepdims=True), kmsq)
    kmsq_ref[0, 0] = kmsq

    kones = jnp.where((lane % FEAT_ROWS < 3) & (lane < 2 * FEAT_ROWS), 1.0, 0.0)
    for pr in range(n_fox // 2):
        kf_ref[0, :, 2 * pr * LANES:(2 * pr + 1) * LANES] = kfox[:, pr * LANES:(pr + 1) * LANES]
        kf_ref[0, :, (2 * pr + 1) * LANES:(2 * pr + 2) * LANES] = (
            kfeat[:, pr * LANES:(pr + 1) * LANES] + kones).astype(BF16)

    def rmax():
        out = jnp.max(r_ref[0])
        for h in heads[1:]:
            out = jnp.maximum(out, jnp.max(r_ref[h]))
        return out

    def block(j):
        k0 = pl.multiple_of(j * BLK, BLK)
        z = [jnp.dot(kpair(k0, h), qh[h], preferred_element_type=F32) for h in heads]
        lg = [stick_logs(z[h]) for h in heads]
        sfx = [suffix(lg[h][1]) for h in heads]
        a = []
        for h in heads:
            r_old = r_ref[h]
            a.append(jnp.exp2(lg[h][0] + sfx[h] + r_old).astype(BF16))
            r_ref[h] = r_old + (sfx[h][0:1, :] + lg[h][2])
        for h in heads:
            acc_ref[h] = acc_ref[h] + jnp.dot(vts_s[h, j], a[h], preferred_element_type=F32)
        return rmax()

    def body(carry):
        j, _ = carry
        return j - 1, block(j)

    lax.while_loop(lambda c: (c[0] >= 0) & (c[1] > -SKIP_LOG2), body, (qb - 2, rmax()))

    osb_ref[0] = jnp.concatenate([acc_ref[h] for h in heads], axis=0).T.astype(osb_ref.dtype)


def _pre0(x, g, wt, wk, wf, bf, pk, aq, tri, n_fox):
    B, S, D = x.shape
    tm = BLK
    nb = S // tm
    fw = n_fox * HEAD_DIM
    const = lambda a: pl.BlockSpec(a.shape, lambda b, s: (0,) * a.ndim)
    tile = lambda s: jnp.minimum(s, nb - 1)
    tok_lane = lambda rows: pl.BlockSpec((1, rows, tm), lambda b, s: (b, 0, tile(s)))
    return pl.pallas_call(
        functools.partial(_pre0_kernel, n_fox=n_fox),
        grid=(B, nb + 1),
        in_specs=[pl.BlockSpec((1, tm, D), lambda b, s: (b, tile(s), 0))]
        + [const(a) for a in (g, wt, wk, wf, bf, pk, aq, tri)],
        out_specs=[
            tok_lane(fw),
            pl.BlockSpec((1, n_fox, 1, V_ROWS, tm), lambda b, s: (b, 0, tile(s), 0, 0)),
            pl.BlockSpec((1, tm, n_fox * LANES), lambda b, s: (b, tile(s), 0)),
            tok_lane(n_fox * FEAT_ROWS),
            tok_lane(8),
            tok_lane(8),
            pl.BlockSpec((1, 1, 1, LANES), lambda b, s: (b, tile(s), 0, 0)),
            pl.BlockSpec((1, 1, 1, LANES), lambda b, s: (b, tile(s), 0, 0)),
            pl.BlockSpec((1, tm, fw), lambda b, s: (b, jnp.maximum(s - 1, 0), 0)),
        ],
        out_shape=[
            jax.ShapeDtypeStruct((B, fw, S), BF16),
            jax.ShapeDtypeStruct((B, n_fox, nb, V_ROWS, tm), BF16),
            jax.ShapeDtypeStruct((B, S, n_fox * LANES), BF16),
            jax.ShapeDtypeStruct((B, n_fox * FEAT_ROWS, S), BF16),
            jax.ShapeDtypeStruct((B, 8, S), F32),
            jax.ShapeDtypeStruct((B, 8, S), F32),
            jax.ShapeDtypeStruct((B, nb, 1, LANES), F32),
            jax.ShapeDtypeStruct((B, nb, 1, LANES), F32),
            jax.ShapeDtypeStruct((B, S, fw), BF16),
        ],
        scratch_shapes=[
            pltpu.VMEM((2, 1, LANES), F32),
            pltpu.VMEM((fw, tm), BF16),
            pltpu.VMEM((S, fw), BF16),
            pltpu.VMEM((n_fox, nb, HEAD_DIM, tm), BF16),
            pltpu.VMEM((n_fox, 1, tm), F32),
            pltpu.VMEM((n_fox, HEAD_DIM, tm), F32),
        ],
        compiler_params=_params(("arbitrary", "arbitrary")),
        name="pre0",
    )(x, g, wt, wk, wf, bf, pk, aq, tri)


def _fox_kernel(fb_ref, qt_ref, feat_ref, k_ref, vt_ref, gt_ref, qn_ref, kmsq_ref,
                o_ref, m_ref, acc_ref):
    b = pl.program_id(0)
    hp = pl.program_id(1)
    qi = pl.program_id(2)
    bq = qt_ref.shape[2]
    n_h = acc_ref.shape[0]
    heads = range(n_h)
    hg = [n_h * hp + h for h in heads]
    row = lax.broadcasted_iota(jnp.int32, (BLK, bq), 0)
    col = lax.broadcasted_iota(jnp.int32, (BLK, bq), 1)
    causal = row <= col
    kmax_sq = jnp.max(kmsq_ref[0], axis=0)
    lane = lax.broadcasted_iota(jnp.int32, (1, LANES), 1)

    def zeros(rows):
        return jnp.zeros((rows, bq), BF16)

    qaug = []
    for h in heads:
        q = qt_ref[0, h * HEAD_DIM:(h + 1) * HEAD_DIM, :]
        f = feat_ref[0, h * FEAT_ROWS:(h + 1) * FEAT_ROWS, :]
        parts = [q, zeros(HEAD_DIM), f, zeros(FEAT_ROWS)] if h % 2 == 0 else \
                [zeros(HEAD_DIM), q, zeros(FEAT_ROWS), f]
        qaug.append(jnp.concatenate(parts + [zeros(LANES - 2 * FEAT_ROWS)], axis=0))

    def kpair(k0, h):
        return k_ref[0, pl.ds(k0, BLK), (h // 2) * 2 * LANES:(h // 2 + 1) * 2 * LANES]

    jp = jnp.maximum(qi - 1, 0)
    kp0 = pl.multiple_of(jp * BLK, BLK)
    kd0 = pl.multiple_of(qi * BLK, BLK)
    st = [jnp.dot(jnp.concatenate([kpair(kp0, h), kpair(kd0, h)], axis=0),
                  qaug[h], preferred_element_type=F32) for h in heads]
    p_first = []
    for h in heads:
        cp = jnp.where(qi > 0, fb_ref[b, hg[h], qi] - fb_ref[b, hg[h], jp], NEG_INF)
        st_p = st[h][0:BLK]
        st_d = jnp.where(causal, st[h][BLK:2 * BLK], NEG_INF)
        m = jnp.maximum(jnp.max(st_d, axis=0, keepdims=True),
                        jnp.max(st_p, axis=0, keepdims=True) + cp)
        m_ref[h] = m
        p_first.append(jnp.concatenate([jnp.exp2(st_p - (m - cp)), jnp.exp2(st_d - m)],
                                       axis=0).astype(BF16))
    for h in heads:
        vt2 = jnp.concatenate([vt_ref[0, h, jp], vt_ref[0, h, qi]], axis=1)
        acc_ref[h] = jnp.dot(vt2, p_first[h], preferred_element_type=F32)

    def block(j):
        k0 = pl.multiple_of(j * BLK, BLK)
        st = [jnp.dot(kpair(k0, h), qaug[h], preferred_element_type=F32)
              for h in heads]
        p, alpha = [], []
        for h in heads:
            c = fb_ref[b, hg[h], qi] - fb_ref[b, hg[h], j]
            m_old = m_ref[h]
            m_new = jnp.maximum(m_old, jnp.max(st[h], axis=0, keepdims=True) + c)
            p.append(jnp.exp2(st[h] - (m_new - c)).astype(BF16))
            alpha.append(jnp.exp2(m_old - m_new))
            m_ref[h] = m_new
        for h in heads:
            acc_ref[h] = alpha[h] * acc_ref[h] + jnp.dot(vt_ref[0, h, j], p[h],
                                                         preferred_element_type=F32)

    th = []
    for h in heads:
        kmax = jnp.sqrt(jnp.sum(jnp.where(lane == hg[h], kmax_sq, 0.0), axis=-1, keepdims=True))
        bound = qn_ref[0, pl.ds(hg[h], 1), :] * kmax + gt_ref[0, pl.ds(hg[h], 1), :] - m_ref[h]
        th.append(jnp.max(bound) + SKIP_LOG2)

    def needed(j):
        jn = jnp.clip(j + 1, 0, qi)
        need = fb_ref[b, hg[0], jn] - fb_ref[b, hg[0], qi] <= th[0]
        for h in heads[1:]:
            need = need | (fb_ref[b, hg[h], jn] - fb_ref[b, hg[h], qi] <= th[h])
        return (j >= 0) & need

    def body(j):
        block(j)
        return j - 1

    lax.while_loop(needed, body, qi - 2)

    out_t = jnp.concatenate(
        [acc_ref[h, 0:HEAD_DIM, :] / acc_ref[h, HEAD_DIM:HEAD_DIM + 1, :] for h in heads], axis=0)
    o_ref[0] = out_t.T.astype(o_ref.dtype)


def _fox(fb, qt, feat, kf, vt, gt, qn, kmsq):
    B, _, S = qt.shape
    n_h = FOX_HEADS
    n_grp = qt.shape[1] // (n_h * HEAD_DIM)
    nb = S // BLK
    return pl.pallas_call(
        _fox_kernel,
        grid_spec=pltpu.PrefetchScalarGridSpec(
            num_scalar_prefetch=1,
            grid=(B, n_grp, nb),
            in_specs=[
                pl.BlockSpec((1, n_h * HEAD_DIM, BLK), lambda b, h, i, s: (b, h, i)),
                pl.BlockSpec((1, n_h * FEAT_ROWS, BLK), lambda b, h, i, s: (b, h, i)),
                pl.BlockSpec((1, S, n_h * LANES), lambda b, h, i, s: (b, 0, h),
                             pipeline_mode=pl.Buffered(1)),
                pl.BlockSpec((1, n_h, nb, V_ROWS, BLK), lambda b, h, i, s: (b, h, 0, 0, 0),
                             pipeline_mode=pl.Buffered(1)),
                pl.BlockSpec((1, 8, BLK), lambda b, h, i, s: (b, 0, i)),
                pl.BlockSpec((1, 8, BLK), lambda b, h, i, s: (b, 0, i)),
                pl.BlockSpec((1, nb, 1, LANES), lambda b, h, i, s: (b, 0, 0, 0)),
            ],
            out_specs=pl.BlockSpec((1, BLK, n_h * HEAD_DIM), lambda b, h, i, s: (b, i, h)),
            scratch_shapes=[
                pltpu.VMEM((n_h, 1, BLK), F32),
                pltpu.VMEM((n_h, V_ROWS, BLK), F32),
            ],
        ),
        out_shape=jax.ShapeDtypeStruct((B, S, n_grp * n_h * HEAD_DIM), BF16),
        compiler_params=_params(("arbitrary", "arbitrary", "arbitrary")),
        name="fox",
    )(fb, qt, feat, kf, vt, gt, qn, kmsq)


def _pre1_kernel(x_ref, g_ref, wt_ref, pos_ref, inv_ref, qt_ref, k_ref, vt_ref, *, n_q, n_kv):
    tm = x_ref.shape[1]
    half = HEAD_DIM // 2
    hb = _rms(x_ref[0], g_ref[...]).astype(BF16)
    ang = inv_ref[...] * pos_ref[0].astype(F32)
    cos = jnp.cos(ang)
    sin = jnp.sin(ang)

    def proj_t(r0, rows):
        return lax.dot_general(wt_ref[r0:r0 + rows, :], hb, _NT, preferred_element_type=F32)

    def rope_t(x):
        x1, x2 = x[0:half], x[half:HEAD_DIM]
        return jnp.concatenate([x1 * cos - x2 * sin, x2 * cos + x1 * sin], axis=0)

    qw = n_q * HEAD_DIM
    for c in range(n_q // 4):
        qt = proj_t(c * 4 * HEAD_DIM, 4 * HEAD_DIM)
        for a in range(4):
            hq = c * 4 + a
            qt_ref[0, hq * HEAD_DIM:(hq + 1) * HEAD_DIM, :] = rope_t(
                qt[a * HEAD_DIM:(a + 1) * HEAD_DIM]).astype(BF16)
    kt = proj_t(qw, n_kv * HEAD_DIM)
    zpad = jnp.zeros((LANES - HEAD_DIM, tm), F32)
    for g in range(n_kv):
        kg = jnp.concatenate([rope_t(kt[g * HEAD_DIM:(g + 1) * HEAD_DIM]), zpad], axis=0)
        k_ref[0, :, g * LANES:(g + 1) * LANES] = kg.T.astype(BF16)
    vt = proj_t(qw + n_kv * HEAD_DIM, n_kv * HEAD_DIM).astype(BF16)
    pad_row = lax.broadcasted_iota(jnp.int32, (V_ROWS - HEAD_DIM, tm), 0)
    ones_pad = jnp.where(pad_row == 0, 1.0, 0.0).astype(BF16)
    for g in range(n_kv):
        vt_ref[0, g * V_ROWS:g * V_ROWS + HEAD_DIM, :] = vt[g * HEAD_DIM:(g + 1) * HEAD_DIM]
        vt_ref[0, g * V_ROWS + HEAD_DIM:(g + 1) * V_ROWS, :] = ones_pad


def _pre1(h, g, wt, pos, inv, tm, n_q, n_kv):
    B, S, D = h.shape
    return pl.pallas_call(
        functools.partial(_pre1_kernel, n_q=n_q, n_kv=n_kv),
        grid=(B, S // tm),
        in_specs=[
            pl.BlockSpec((1, tm, D), lambda b, i: (b, i, 0)),
            pl.BlockSpec((1, D), lambda b, i: (0, 0)),
            pl.BlockSpec(wt.shape, lambda b, i: (0, 0)),
            pl.BlockSpec((1, 1, tm), lambda b, i: (b, 0, i)),
            pl.BlockSpec(inv.shape, lambda b, i: (0, 0)),
        ],
        out_specs=[
            pl.BlockSpec((1, n_q * HEAD_DIM, tm), lambda b, i: (b, 0, i)),
            pl.BlockSpec((1, tm, n_kv * LANES), lambda b, i: (b, i, 0)),
            pl.BlockSpec((1, n_kv * V_ROWS, tm), lambda b, i: (b, 0, i)),
        ],
        out_shape=[
            jax.ShapeDtypeStruct((B, n_q * HEAD_DIM, S), BF16),
            jax.ShapeDtypeStruct((B, S, n_kv * LANES), BF16),
            jax.ShapeDtypeStruct((B, n_kv * V_ROWS, S), BF16),
        ],
        compiler_params=_params(("arbitrary", "arbitrary")),
        name="pre1",
    )(h, g, wt, pos, inv)


def _swa_kernel(sink_ref, qt_ref, kp_ref, ko_ref, vp_ref, vo_ref, o_ref, *, n_kv, group):
    i = pl.program_id(1)
    W = WINDOW
    n_sub = qt_ref.shape[2] // W
    r = lax.broadcasted_iota(jnp.int32, (2 * W, W), 0)
    c = lax.broadcasted_iota(jnp.int32, (2 * W, W), 1)
    rel = c + W - r
    band = (rel >= 0) & (rel < W)
    valid = [jnp.concatenate([band & ((r >= W) | (i > 0)) if u == 0 else band] * group, axis=1)
             for u in range(n_sub)]
    seg = lax.broadcasted_iota(jnp.int32, (1, group * W), 1) // W
    zpad = jnp.zeros((LANES - HEAD_DIM, group * W), BF16)
    chains = [(u, g) for u in range(n_sub) for g in range(n_kv)]

    def keys(u, g):
        ls = slice(g * LANES, (g + 1) * LANES)
        if u == 0:
            return jnp.concatenate([kp_ref[0, :, ls], ko_ref[0, 0:W, ls]], axis=0)
        return ko_ref[0, (u - 1) * W:(u + 1) * W, ls]

    def values(u, g):
        rs = slice(g * V_ROWS, (g + 1) * V_ROWS)
        if u == 0:
            return jnp.concatenate([vp_ref[0, rs, :], vo_ref[0, rs, 0:W]], axis=1)
        return vo_ref[0, rs, (u - 1) * W:(u + 1) * W]

    st = []
    for u, g in chains:
        qg = jnp.concatenate(
            [qt_ref[0, (g * group + a) * HEAD_DIM:(g * group + a + 1) * HEAD_DIM, u * W:(u + 1) * W]
             for a in range(group)], axis=1)
        st.append(jnp.dot(keys(u, g), jnp.concatenate([qg, zpad], axis=0),
                          preferred_element_type=F32))
    p, sink_term = [], []
    for n, (u, g) in enumerate(chains):
        sg = jnp.where(valid[u], st[n], NEG_INF)
        sink = jnp.zeros((1, group * W), F32)
        for a in range(group):
            sink = jnp.where(seg == a, sink_ref[g * group + a] * LOG2E, sink)
        m = jnp.maximum(jnp.max(sg, axis=0, keepdims=True), sink)
        p.append(jnp.exp2(sg - m).astype(BF16))
        sink_term.append(jnp.exp2(sink - m))
    acc = [jnp.dot(values(u, g), p[n], preferred_element_type=F32)
           for n, (u, g) in enumerate(chains)]
    for n, (u, g) in enumerate(chains):
        o = acc[n][0:HEAD_DIM] / (acc[n][HEAD_DIM:HEAD_DIM + 1] + sink_term[n])
        for a in range(0, group, 2):
            pair = jnp.concatenate([o[:, a * W:(a + 1) * W], o[:, (a + 1) * W:(a + 2) * W]], axis=0)
            l0 = (g * group + a) * HEAD_DIM
            o_ref[0, u * W:(u + 1) * W, l0:l0 + 2 * HEAD_DIM] = pair.T.astype(o_ref.dtype)


def _swa(qt, kpad, vt, sinks, n_q, n_kv):
    B, _, S = qt.shape
    W = WINDOW
    n_sub = SWA_SUB
    prev = lambda i: jnp.maximum(n_sub * i - 1, 0)
    return pl.pallas_call(
        functools.partial(_swa_kernel, n_kv=n_kv, group=n_q // n_kv),
        grid_spec=pltpu.PrefetchScalarGridSpec(
            num_scalar_prefetch=1,
            grid=(B, S // (n_sub * W)),
            in_specs=[
                pl.BlockSpec((1, n_q * HEAD_DIM, n_sub * W), lambda b, i, s: (b, 0, i)),
                pl.BlockSpec((1, W, n_kv * LANES), lambda b, i, s: (b, prev(i), 0)),
                pl.BlockSpec((1, n_sub * W, n_kv * LANES), lambda b, i, s: (b, i, 0)),
                pl.BlockSpec((1, n_kv * V_ROWS, W), lambda b, i, s: (b, 0, prev(i))),
                pl.BlockSpec((1, n_kv * V_ROWS, n_sub * W), lambda b, i, s: (b, 0, i)),
            ],
            out_specs=pl.BlockSpec((1, n_sub * W, n_q * HEAD_DIM), lambda b, i, s: (b, i, 0)),
        ),
        out_shape=jax.ShapeDtypeStruct((B, S, n_q * HEAD_DIM), BF16),
        compiler_params=_params(("arbitrary", "arbitrary")),
        name="swa",
    )(sinks, qt, kpad, kpad, vt, vt)


def _post_kernel(*refs, n_mix, final_norm):
    h_ref = refs[0]
    mix_refs = refs[1:1 + n_mix]
    (p_ref, wo_ref, gf_ref, wg_ref, wu_ref, wd_ref, gp_ref, wpg_ref, wpp_ref, gfin_ref,
     o_ref) = refs[1 + n_mix:]
    h = h_ref[...]
    off = 0
    for m_ref in mix_refs:
        w = m_ref.shape[1]
        h = h + jnp.dot(m_ref[...], wo_ref[off:off + w, :], preferred_element_type=F32)
        off += w
    hb = _rms(h, gf_ref[...]).astype(BF16)
    g = jnp.dot(hb, wg_ref[...], preferred_element_type=F32)
    u = jnp.dot(hb, wu_ref[...], preferred_element_type=F32)
    act = (g * jax.nn.sigmoid(g) * u).astype(BF16)
    h = h + jnp.dot(act, wd_ref[...], preferred_element_type=F32)
    gate = jax.nn.sigmoid(jnp.dot(_rms(h, gp_ref[...]).astype(BF16), wpg_ref[...],
                                  preferred_element_type=F32))
    h = h + gate * jnp.dot(p_ref[...].astype(BF16), wpp_ref[...], preferred_element_type=F32)
    if final_norm:
        h = _rms(h, gfin_ref[...])
    o_ref[...] = h


def _post(h, mixes, p_all, layer, wo, gf, wg, wu, wd, gp, wpg, wpp, gfin, tm, final_norm):
    T, D = h.shape
    row = lambda w: pl.BlockSpec((tm, w), lambda i: (i, 0))
    full = lambda a: pl.BlockSpec(a.shape, lambda i: (0, 0))
    lay = lambda a: pl.BlockSpec((None,) + a.shape[1:], lambda i: (layer, 0, 0),
                                 pipeline_mode=pl.Buffered(1))
    return pl.pallas_call(
        functools.partial(_post_kernel, n_mix=len(mixes), final_norm=final_norm),
        grid=(T // tm,),
        in_specs=[row(D)] + [row(m.shape[1]) for m in mixes]
        + [pl.BlockSpec((None, tm, p_all.shape[2]), lambda i: (layer, i, 0))]
        + [full(wo)] + [lay(a) for a in (gf, wg, wu, wd, gp, wpg, wpp)] + [full(gfin)],
        out_specs=row(D),
        out_shape=jax.ShapeDtypeStruct((T, D), F32),
        compiler_params=_params(("arbitrary",)),
        name="post",
    )(h, *mixes, p_all, wo, gf, wg, wu, wd, gp, wpg, wpp, gfin)


def _layer0_weights(w_in, b_f, n_fox):
    fw = n_fox * HEAD_DIM
    scale = HEAD_DIM ** -0.5 * LOG2E
    D = w_in.shape[0]
    qa, ka, va, qs, ks, vs = (w_in[:, i * fw:(i + 1) * fw] for i in range(6))
    wt = jnp.concatenate([qa * scale, va, qs * scale, vs], axis=1).T.astype(BF16)
    wk = jnp.concatenate([ka, ks], axis=1).astype(BF16)
    gate_pad = ((0, 0), (0, LANES - 3 * n_fox))
    wf = jnp.pad(jnp.tile(w_in[:, 6 * fw:], (1, 3)), gate_pad).astype(BF16)
    bf = jnp.pad(jnp.tile(b_f.reshape(1, n_fox), (1, 3)), gate_pad)
    heads = jnp.arange(n_fox)
    pk = jnp.zeros((LANES, n_fox // 2 * LANES), F32)
    aq = jnp.zeros((n_fox * FEAT_ROWS, LANES), F32)
    for piece in range(3):
        pk = pk.at[piece * n_fox + heads,
                   heads // 2 * LANES + heads % 2 * FEAT_ROWS + 3 + piece].set(-1.0)
        aq = aq.at[heads * FEAT_ROWS + piece, piece * n_fox + heads].set(1.0)
    return wt, wk, wf, bf, pk.astype(BF16), aq.astype(BF16)


def kernel(x, p, positions, norm_mix, norm_ffn, norm_ple, norm_final, ev_w_in, ev_b_f, ev_w_out,
           od_w_in, od_sinks, od_w_out, ffn_w_gate, ffn_w_up, ffn_w_down, ple_w_proj, ple_w_gate):
    B, S, D = x.shape
    T = B * S
    n_heads = D // HEAD_DIM
    n_fox = n_heads // 2
    fox_w = n_fox * HEAD_DIM
    n_q, n_kv = n_heads, 4
    assert S % BLK == 0 and n_fox == 8
    row = lambda a: a.reshape(1, -1)

    wt, wk, wf, bf, pk, aq = _layer0_weights(ev_w_in[0], ev_b_f[0], n_fox)
    tri = (jnp.arange(BLK)[None, :] > jnp.arange(BLK)[:, None]).astype(BF16)
    (qtf, vtf, kf, feat, gt, qn, fb, kmsq, o_sb) = _pre0(
        x, row(norm_mix[0]), wt, wk, wf, bf, pk, aq, tri, n_fox)
    fb_heads = fb[:, :, 0, :n_fox].transpose(0, 2, 1)
    o_fox = _fox(fb_heads, qtf, feat, kf, vtf, gt, qn, kmsq)

    tm = min(512, T)
    depth = norm_ffn.shape[0]
    p_all = p.reshape(depth, T, -1)
    stacked = (norm_ffn.reshape(depth, 1, D), ffn_w_gate.astype(BF16), ffn_w_up.astype(BF16),
               ffn_w_down.astype(BF16), norm_ple.reshape(depth, 1, D), ple_w_gate.astype(BF16),
               ple_w_proj.astype(BF16))
    h = _post(x.reshape(T, D), [o_fox.reshape(T, fox_w), o_sb.reshape(T, fox_w)], p_all, 0,
              ev_w_out[0].astype(BF16), *stacked, row(norm_final), tm, final_norm=False)

    qw = n_q * HEAD_DIM
    kw = n_kv * HEAD_DIM
    col_scale1 = jnp.ones((qw + 2 * kw,), F32).at[:qw].set(HEAD_DIM ** -0.5 * LOG2E)
    w1t = (od_w_in[0] * col_scale1).T.astype(BF16)
    half = HEAD_DIM // 2
    inv = (ROPE_THETA ** (-jnp.arange(half, dtype=F32) / half)).reshape(half, 1)
    qt1, k1, vt1 = _pre1(h.reshape(B, S, D), row(norm_mix[1]), w1t, positions.reshape(B, 1, S),
                         inv, min(tm, S), n_q, n_kv)
    o_swa = _swa(qt1, k1, vt1, od_sinks[0], n_q, n_kv)
    out = _post(h, [o_swa.reshape(T, qw)], p_all, 1, od_w_out[0].astype(BF16), *stacked,
                row(norm_final), tm, final_norm=True)
    return out.reshape(B, S, D)
```

```python
import functools

import jax
import jax.numpy as jnp
import numpy as np
from jax import lax
from jax.experimental import pallas as pl
from jax.experimental.pallas import tpu as pltpu

F32 = jnp.float32
BF16 = jnp.bfloat16

HEAD_DIM = 64
LANES = 128
BLK = 256
V_ROWS = 80
FEAT_ROWS = 16
WINDOW = 128
ROPE_THETA = 10000.0
EPS = 1e-6
NEG_INF = -1e30
LOG2E = 1.4426950408889634
SKIP_LOG2 = 60.0 * LOG2E
FOX_HEADS = 8
SWA_SUB = 4
VMEM_LIMIT = 56 * 1024 * 1024

_NT = (((1,), (1,)), ((), ()))


def _params(sem):
    return pltpu.CompilerParams(dimension_semantics=sem, vmem_limit_bytes=VMEM_LIMIT)


def _rms(x, g):
    return x * lax.rsqrt(jnp.mean(x * x, axis=-1, keepdims=True) + EPS) * g


def _log_sigmoid(x):
    return jnp.minimum(x, 0.0) - jnp.log(1.0 + jnp.exp(-jnp.abs(x)))


def _split3(x):
    a = x.astype(BF16)
    r = x - a.astype(F32)
    b = r.astype(BF16)
    c = (r - b.astype(F32)).astype(BF16)
    return a, b, c


def _pre0_kernel(x_ref, g_ref, wt_ref, wk_ref, wf_ref, bf_ref, pk_ref, aq_ref, tri_ref,
                 qtf_ref, vtf_ref, kf_ref, feat_ref, gt_ref, qn_ref, fb_ref, kmsq_ref, osb_ref,
                 carry_ref, qts_s, ks_s, vts_s, r_ref, acc_ref, *, n_fox):
    tm = x_ref.shape[1]
    fw = n_fox * HEAD_DIM
    i = pl.program_id(1)
    nb = pl.num_programs(1) - 1
    extra = i == nb
    t = jnp.minimum(i, nb - 1)
    qb = jnp.maximum(i - 1, 0)
    heads = range(n_fox)

    @pl.when(i == 0)
    def _():
        carry_ref[...] = jnp.zeros_like(carry_ref)
        qts_s[...] = jnp.zeros_like(qts_s)
        ks_s[0:tm, :] = jnp.zeros((tm, fw), BF16)
        vts_s[:, 0] = jnp.zeros((n_fox, HEAD_DIM, tm), BF16)

    rowi = lax.broadcasted_iota(jnp.int32, (BLK, tm), 0)
    coli = lax.broadcasted_iota(jnp.int32, (BLK, tm), 1)
    strict = rowi < coli
    top = lax.broadcasted_iota(jnp.int32, (LANES, tm), 0) < HEAD_DIM
    zero = jnp.zeros((LANES, tm), BF16)
    qh = []
    for h in heads:
        pair = qts_s[(h // 2) * LANES:(h // 2 + 1) * LANES, :]
        qh.append(jnp.where(top, pair, zero) if h % 2 == 0 else jnp.where(top, zero, pair))
    tri_sb = tri_ref[...]

    def kpair(k0, h):
        return ks_s[pl.ds(k0, BLK), (h // 2) * LANES:(h // 2 + 1) * LANES]

    def stick_logs(z, mask=None):
        nz = -z
        l1 = jnp.minimum(nz, 0.0) - jnp.log2(1.0 + jnp.exp2(jnp.minimum(z, nz)))
        lb = z + l1
        if mask is not None:
            l1 = jnp.where(mask, l1, 0.0)
        return lb, l1.astype(BF16), l1[0:1, :]

    def suffix(l1b):
        return jnp.dot(tri_sb, l1b, preferred_element_type=F32)

    def tproj(c, half):
        r0 = c * fw + half * (fw // 2)
        return lax.dot_general(wt_ref[r0:r0 + fw // 2, :], hb, _NT,
                               preferred_element_type=F32).astype(BF16)

    jp = jnp.maximum(qb - 1, 0)
    kp0 = pl.multiple_of(jp * BLK, BLK)
    kd0 = pl.multiple_of(qb * BLK, BLK)
    has_prev = qb > 0
    cp = jnp.where(has_prev, 0.0, NEG_INF)

    hb = _rms(x_ref[0], g_ref[...]).astype(BF16)
    gate = jnp.dot(hb, wf_ref[...], preferred_element_type=F32) + bf_ref[...]
    lf = _log_sigmoid(gate) * LOG2E

    def gate_stage(k, c):
        if k == 0:
            row = lax.broadcasted_iota(jnp.int32, (tm, tm), 0)
            col = lax.broadcasted_iota(jnp.int32, (tm, tm), 1)
            tri = jnp.where(row >= col, 1.0, 0.0).astype(BF16)
            G = jnp.zeros((tm, LANES), F32)
            for piece in _split3(lf):
                G = G + jnp.dot(tri, piece, preferred_element_type=F32)
            base = jnp.where(extra, carry_ref[1], carry_ref[0])
            fb_ref[0, 0] = base
            carry_ref[1] = base
            carry_ref[0] = base + G[tm - 1:tm, :]
            c["G"] = G
        elif k == 1:
            g_hi, g_mid, g_lo = _split3(c["G"])
            lane_t = lax.broadcasted_iota(jnp.int32, (tm, LANES), 1)
            gp = jnp.where(lane_t < n_fox, g_hi, jnp.where(lane_t < 2 * n_fox, g_mid, g_lo))
            c["kfeat"] = jnp.dot(gp, pk_ref[...], preferred_element_type=F32)
        elif k == 2:
            GT = c["G"].T
            gt_ref[0] = GT[0:8, :]
            t_hi, t_mid, t_lo = _split3(GT)
            row_t = lax.broadcasted_iota(jnp.int32, (LANES, tm), 0)
            c["gpt"] = jnp.where(row_t < n_fox, t_hi, jnp.where(row_t < 2 * n_fox, t_mid, t_lo))
        elif k == 3:
            qfeat = jnp.dot(aq_ref[...], c["gpt"], preferred_element_type=F32)
            frow = lax.broadcasted_iota(jnp.int32, (n_fox * FEAT_ROWS, 1), 0) % FEAT_ROWS
            qones = jnp.where((frow >= 3) & (frow < 6), 1.0, 0.0)
            feat_ref[0] = (qfeat + qones).astype(BF16)

    lp, ld, tp, sfp, sfd, chain = [], [], [], [], [], {}
    for h in heads:
        z2 = jnp.dot(jnp.concatenate([kpair(kp0, h), kpair(kd0, h)], axis=0), qh[h],
                     preferred_element_type=F32)
        lp.append(stick_logs(z2[0:BLK]))
        ld.append(stick_logs(z2[BLK:2 * BLK], strict))
        tp.append(tproj(h // 2, h % 2))
        if h >= 1:
            sfp.append(suffix(lp[h - 1][1]))
            sfd.append(suffix(ld[h - 1][1]))
        if h % 2 == 0:
            gate_stage(h // 2, chain)
    sfp.append(suffix(lp[n_fox - 1][1]))
    sfd.append(suffix(ld[n_fox - 1][1]))
    qtf = jnp.concatenate(tp[0:2], axis=0)
    vtf = jnp.concatenate(tp[2:4], axis=0)
    qts_new = jnp.concatenate(tp[4:6], axis=0)
    vts = jnp.concatenate(tp[6:8], axis=0)
    kfeat = chain["kfeat"]

    a_first, kk = [], []
    nk = 2 * wk_ref.shape[1] // n_fox
    for h in heads:
        tot_d = sfd[h][0:1, :] + ld[h][2]
        tot_p = sfp[h][0:1, :] + lp[h][2]
        a_d = jnp.where(strict, jnp.exp2(ld[h][0] + sfd[h]), 0.0)
        a_p = jnp.exp2(lp[h][0] + sfp[h] + (tot_d + cp))
        a_first.append(jnp.concatenate([a_p, a_d], axis=0).astype(BF16))
        r_ref[h] = tot_d + jnp.where(has_prev, tot_p, 0.0)
        if h % 2 == 0:
            kk.append(jnp.dot(hb, wk_ref[:, (h // 2) * nk:(h // 2 + 1) * nk],
                              preferred_element_type=F32))
    kk = jnp.concatenate(kk, axis=1)
    for h in heads:
        vt2 = jnp.concatenate([vts_s[h, jp], vts_s[h, qb]], axis=1)
        acc_ref[h] = jnp.dot(vt2, a_first[h], preferred_element_type=F32)

    qtf_ref[0] = qtf
    lane_row = lax.broadcasted_iota(jnp.int32, (V_ROWS - HEAD_DIM, tm), 0)
    ones_pad = jnp.where(lane_row == 0, 1.0, 0.0).astype(BF16)
    for h in heads:
        vtf_ref[0, h, 0, 0:HEAD_DIM, :] = vtf[h * HEAD_DIM:(h + 1) * HEAD_DIM, :]
        vtf_ref[0, h, 0, HEAD_DIM:V_ROWS, :] = ones_pad
    qts_s[...] = qts_new
    for h in heads:
        vts_s[h, t] = vts[h * HEAD_DIM:(h + 1) * HEAD_DIM, :]

    q32 = qtf.astype(F32)
    qn_rows = [jnp.sqrt(jnp.sum(jnp.square(q32[h * HEAD_DIM:(h + 1) * HEAD_DIM, :]),
                                axis=0, keepdims=True)) for h in heads]
    qn_ref[0] = jnp.concatenate(qn_rows, axis=0)

    ks_s[pl.ds(pl.multiple_of(t * tm, tm), tm), :] = kk[:, fw:].astype(BF16)
    kfox = kk[:, :fw].astype(BF16)
    k32 = kfox.astype(F32)
    lane = lax.broadcasted_iota(jnp.int32, (1, LANES), 1)
    first_head = lane < HEAD_DIM
    kmsq = jnp.zeros((1, LANES), F32)
    for pr in range(n_fox // 2):
        sq = jnp.square(k32[:, pr * LANES:(pr + 1) * LANES])
        for e in range(2):
            mine = first_head if e == 0 else jnp.logical_not(first_head)
            ss = jnp.sum(jnp.where(mine, sq, 0.0), axis=-1, keepdims=True)
            kmsq = jnp.where(lane == 2 * pr + e, jnp.max(ss, axis=0, keepdims=True), kmsq)
    kmsq_ref[0, 0] = kmsq

    kones = jnp.where((lane % FEAT_ROWS < 3) & (lane < 2 * FEAT_ROWS), 1.0, 0.0)
    for pr in range(n_fox // 2):
        kf_ref[0, :, 2 * pr * LANES:(2 * pr + 1) * LANES] = kfox[:, pr * LANES:(pr + 1) * LANES]
        kf_ref[0, :, (2 * pr + 1) * LANES:(2 * pr + 2) * LANES] = (
            kfeat[:, pr * LANES:(pr + 1) * LANES] + kones).astype(BF16)

    def rmax():
        out = r_ref[0]
        for h in heads[1:]:
            out = jnp.maximum(out, r_ref[h])
        return jnp.max(out)

    def block(j):
        k0 = pl.multiple_of(j * BLK, BLK)
        z = [jnp.dot(kpair(k0, h), qh[h], preferred_element_type=F32) for h in heads]
        lg = [stick_logs(z[h]) for h in heads]
        sfx = [suffix(lg[h][1]) for h in heads]
        a = []
        for h in heads:
            r_old = r_ref[h]
            a.append(jnp.exp2(lg[h][0] + sfx[h] + r_old).astype(BF16))
            r_ref[h] = r_old + (sfx[h][0:1, :] + lg[h][2])
        for h in heads:
            acc_ref[h] = acc_ref[h] + jnp.dot(vts_s[h, j], a[h], preferred_element_type=F32)
        return rmax()

    def body(carry):
        j, _ = carry
        return j - 1, block(j)

    lax.while_loop(lambda c: (c[0] >= 0) & (c[1] > -SKIP_LOG2), body, (qb - 2, rmax()))

    osb_ref[0] = jnp.concatenate([acc_ref[h] for h in heads], axis=0).T.astype(osb_ref.dtype)


def _pre0(x, g, wt, wk, wf, bf, pk, aq, tri, n_fox):
    B, S, D = x.shape
    tm = BLK
    nb = S // tm
    fw = n_fox * HEAD_DIM
    const = lambda a: pl.BlockSpec(a.shape, lambda b, s: (0,) * a.ndim)
    tile = lambda s: jnp.minimum(s, nb - 1)
    tok_lane = lambda rows: pl.BlockSpec((1, rows, tm), lambda b, s: (b, 0, tile(s)))
    return pl.pallas_call(
        functools.partial(_pre0_kernel, n_fox=n_fox),
        grid=(B, nb + 1),
        in_specs=[pl.BlockSpec((1, tm, D), lambda b, s: (b, tile(s), 0))]
        + [const(a) for a in (g, wt, wk, wf, bf, pk, aq, tri)],
        out_specs=[
            tok_lane(fw),
            pl.BlockSpec((1, n_fox, 1, V_ROWS, tm), lambda b, s: (b, 0, tile(s), 0, 0)),
            pl.BlockSpec((1, tm, n_fox * LANES), lambda b, s: (b, tile(s), 0)),
            tok_lane(n_fox * FEAT_ROWS),
            tok_lane(8),
            tok_lane(8),
            pl.BlockSpec((1, 1, 1, LANES), lambda b, s: (b, tile(s), 0, 0)),
            pl.BlockSpec((1, 1, 1, LANES), lambda b, s: (b, tile(s), 0, 0)),
            pl.BlockSpec((1, tm, fw), lambda b, s: (b, jnp.maximum(s - 1, 0), 0)),
        ],
        out_shape=[
            jax.ShapeDtypeStruct((B, fw, S), BF16),
            jax.ShapeDtypeStruct((B, n_fox, nb, V_ROWS, tm), BF16),
            jax.ShapeDtypeStruct((B, S, n_fox * LANES), BF16),
            jax.ShapeDtypeStruct((B, n_fox * FEAT_ROWS, S), BF16),
            jax.ShapeDtypeStruct((B, 8, S), F32),
            jax.ShapeDtypeStruct((B, 8, S), F32),
            jax.ShapeDtypeStruct((B, nb, 1, LANES), F32),
            jax.ShapeDtypeStruct((B, nb, 1, LANES), F32),
            jax.ShapeDtypeStruct((B, S, fw), BF16),
        ],
        scratch_shapes=[
            pltpu.VMEM((2, 1, LANES), F32),
            pltpu.VMEM((fw, tm), BF16),
            pltpu.VMEM((S, fw), BF16),
            pltpu.VMEM((n_fox, nb, HEAD_DIM, tm), BF16),
            pltpu.VMEM((n_fox, 1, tm), F32),
            pltpu.VMEM((n_fox, HEAD_DIM, tm), F32),
        ],
        compiler_params=_params(("arbitrary", "arbitrary")),
        name="pre0",
    )(x, g, wt, wk, wf, bf, pk, aq, tri)


def _fox_kernel(fb_ref, qt_ref, feat_ref, k_ref, vt_ref, gt_ref, qn_ref, kmsq_ref,
                o_ref, m_ref, acc_ref):
    b = pl.program_id(0)
    hp = pl.program_id(1)
    qi = pl.program_id(2)
    bq = qt_ref.shape[2]
    n_h = acc_ref.shape[0]
    heads = range(n_h)
    hg = [n_h * hp + h for h in heads]
    row = lax.broadcasted_iota(jnp.int32, (BLK, bq), 0)
    col = lax.broadcasted_iota(jnp.int32, (BLK, bq), 1)
    causal = row <= col
    kmax_sq = jnp.max(kmsq_ref[0], axis=0)
    lane = lax.broadcasted_iota(jnp.int32, (1, LANES), 1)

    def zeros(rows):
        return jnp.zeros((rows, bq), BF16)

    qaug = []
    for h in heads:
        q = qt_ref[0, h * HEAD_DIM:(h + 1) * HEAD_DIM, :]
        f = feat_ref[0, h * FEAT_ROWS:(h + 1) * FEAT_ROWS, :]
        parts = [q, zeros(HEAD_DIM), f, zeros(FEAT_ROWS)] if h % 2 == 0 else \
                [zeros(HEAD_DIM), q, zeros(FEAT_ROWS), f]
        qaug.append(jnp.concatenate(parts + [zeros(LANES - 2 * FEAT_ROWS)], axis=0))

    def kpair(k0, h):
        return k_ref[0, pl.ds(k0, BLK), (h // 2) * 2 * LANES:(h // 2 + 1) * 2 * LANES]

    jp = jnp.maximum(qi - 1, 0)
    kp0 = pl.multiple_of(jp * BLK, BLK)
    kd0 = pl.multiple_of(qi * BLK, BLK)
    st = [jnp.dot(jnp.concatenate([kpair(kp0, h), kpair(kd0, h)], axis=0),
                  qaug[h], preferred_element_type=F32) for h in heads]
    p_first = []
    for h in heads:
        cp = jnp.where(qi > 0, fb_ref[b, hg[h], qi] - fb_ref[b, hg[h], jp], NEG_INF)
        st_p = st[h][0:BLK]
        st_d = jnp.where(causal, st[h][BLK:2 * BLK], NEG_INF)
        m = jnp.maximum(jnp.max(st_d, axis=0, keepdims=True),
                        jnp.max(st_p, axis=0, keepdims=True) + cp)
        m_ref[h] = m
        p_first.append(jnp.concatenate([jnp.exp2(st_p - (m - cp)), jnp.exp2(st_d - m)],
                                       axis=0).astype(BF16))
    for h in heads:
        vt2 = jnp.concatenate([vt_ref[0, h, jp], vt_ref[0, h, qi]], axis=1)
        acc_ref[h] = jnp.dot(vt2, p_first[h], preferred_element_type=F32)

    def block(j):
        k0 = pl.multiple_of(j * BLK, BLK)
        st = [jnp.dot(kpair(k0, h), qaug[h], preferred_element_type=F32)
              for h in heads]
        p, alpha = [], []
        for h in heads:
            c = fb_ref[b, hg[h], qi] - fb_ref[b, hg[h], j]
            m_old = m_ref[h]
            m_new = jnp.maximum(m_old, jnp.max(st[h], axis=0, keepdims=True) + c)
            p.append(jnp.exp2(st[h] - (m_new - c)).astype(BF16))
            alpha.append(jnp.exp2(m_old - m_new))
            m_ref[h] = m_new
        for h in heads:
            acc_ref[h] = alpha[h] * acc_ref[h] + jnp.dot(vt_ref[0, h, j], p[h],
                                                         preferred_element_type=F32)

    def margin(j):
        jn = jnp.clip(j + 1, 0, qi)
        worst = None
        for h in heads:
            kmax = jnp.sqrt(jnp.sum(jnp.where(lane == hg[h], kmax_sq, 0.0), axis=-1, keepdims=True))
            gap = fb_ref[b, hg[h], jn] - fb_ref[b, hg[h], qi]
            mh = (qn_ref[0, pl.ds(hg[h], 1), :] * kmax + gt_ref[0, pl.ds(hg[h], 1), :]
                  - m_ref[h] - gap)
            worst = mh if worst is None else jnp.maximum(worst, mh)
        return jnp.max(worst) + SKIP_LOG2

    def body(carry):
        j, _ = carry
        block(j)
        return j - 1, margin(j - 1)

    lax.while_loop(lambda c: (c[0] >= 0) & (c[1] >= 0.0), body, (qi - 2, margin(qi - 2)))

    out_t = jnp.concatenate(
        [acc_ref[h, 0:HEAD_DIM, :] / acc_ref[h, HEAD_DIM:HEAD_DIM + 1, :] for h in heads], axis=0)
    o_ref[0] = out_t.T.astype(o_ref.dtype)


def _fox(fb, qt, feat, kf, vt, gt, qn, kmsq):
    B, _, S = qt.shape
    n_h = FOX_HEADS
    n_grp = qt.shape[1] // (n_h * HEAD_DIM)
    nb = S // BLK
    return pl.pallas_call(
        _fox_kernel,
        grid_spec=pltpu.PrefetchScalarGridSpec(
            num_scalar_prefetch=1,
            grid=(B, n_grp, nb),
            in_specs=[
                pl.BlockSpec((1, n_h * HEAD_DIM, BLK), lambda b, h, i, s: (b, h, i)),
                pl.BlockSpec((1, n_h * FEAT_ROWS, BLK), lambda b, h, i, s: (b, h, i)),
                pl.BlockSpec((1, S, n_h * LANES), lambda b, h, i, s: (b, 0, h),
                             pipeline_mode=pl.Buffered(1)),
                pl.BlockSpec((1, n_h, nb, V_ROWS, BLK), lambda b, h, i, s: (b, h, 0, 0, 0),
                             pipeline_mode=pl.Buffered(1)),
                pl.BlockSpec((1, 8, BLK), lambda b, h, i, s: (b, 0, i)),
                pl.BlockSpec((1, 8, BLK), lambda b, h, i, s: (b, 0, i)),
                pl.BlockSpec((1, nb, 1, LANES), lambda b, h, i, s: (b, 0, 0, 0)),
            ],
            out_specs=pl.BlockSpec((1, BLK, n_h * HEAD_DIM), lambda b, h, i, s: (b, i, h)),
            scratch_shapes=[
                pltpu.VMEM((n_h, 1, BLK), F32),
                pltpu.VMEM((n_h, V_ROWS, BLK), F32),
            ],
        ),
        out_shape=jax.ShapeDtypeStruct((B, S, n_grp * n_h * HEAD_DIM), BF16),
        compiler_params=_params(("arbitrary", "arbitrary", "arbitrary")),
        name="fox",
    )(fb, qt, feat, kf, vt, gt, qn, kmsq)


def _pre1_kernel(x_ref, g_ref, wt_ref, pos_ref, inv_ref, qt_ref, k_ref, vt_ref, *, n_q, n_kv):
    tm = x_ref.shape[1]
    half = HEAD_DIM // 2
    hb = _rms(x_ref[0], g_ref[...]).astype(BF16)
    ang = inv_ref[...] * pos_ref[0].astype(F32)
    cos = jnp.cos(ang)
    sin = jnp.sin(ang)

    def proj_t(r0, rows):
        return lax.dot_general(wt_ref[r0:r0 + rows, :], hb, _NT, preferred_element_type=F32)

    def rope_t(x):
        x1, x2 = x[0:half], x[half:HEAD_DIM]
        return jnp.concatenate([x1 * cos - x2 * sin, x2 * cos + x1 * sin], axis=0)

    qw = n_q * HEAD_DIM
    for c in range(n_q // 4):
        qt = proj_t(c * 4 * HEAD_DIM, 4 * HEAD_DIM)
        for a in range(4):
            hq = c * 4 + a
            qt_ref[0, hq * HEAD_DIM:(hq + 1) * HEAD_DIM, :] = rope_t(
                qt[a * HEAD_DIM:(a + 1) * HEAD_DIM]).astype(BF16)
    kt = proj_t(qw, n_kv * HEAD_DIM)
    zpad = jnp.zeros((LANES - HEAD_DIM, tm), F32)
    for g in range(n_kv):
        kg = jnp.concatenate([rope_t(kt[g * HEAD_DIM:(g + 1) * HEAD_DIM]), zpad], axis=0)
        k_ref[0, :, g * LANES:(g + 1) * LANES] = kg.T.astype(BF16)
    vt = proj_t(qw + n_kv * HEAD_DIM, n_kv * HEAD_DIM).astype(BF16)
    pad_row = lax.broadcasted_iota(jnp.int32, (V_ROWS - HEAD_DIM, tm), 0)
    ones_pad = jnp.where(pad_row == 0, 1.0, 0.0).astype(BF16)
    for g in range(n_kv):
        vt_ref[0, g * V_ROWS:g * V_ROWS + HEAD_DIM, :] = vt[g * HEAD_DIM:(g + 1) * HEAD_DIM]
        vt_ref[0, g * V_ROWS + HEAD_DIM:(g + 1) * V_ROWS, :] = ones_pad


def _pre1(h, g, wt, pos, inv, tm, n_q, n_kv):
    B, S, D = h.shape
    return pl.pallas_call(
        functools.partial(_pre1_kernel, n_q=n_q, n_kv=n_kv),
        grid=(B, S // tm),
        in_specs=[
            pl.BlockSpec((1, tm, D), lambda b, i: (b, i, 0)),
            pl.BlockSpec((1, D), lambda b, i: (0, 0)),
            pl.BlockSpec(wt.shape, lambda b, i: (0, 0)),
            pl.BlockSpec((1, 1, tm), lambda b, i: (b, 0, i)),
            pl.BlockSpec(inv.shape, lambda b, i: (0, 0)),
        ],
        out_specs=[
            pl.BlockSpec((1, n_q * HEAD_DIM, tm), lambda b, i: (b, 0, i)),
            pl.BlockSpec((1, tm, n_kv * LANES), lambda b, i: (b, i, 0)),
            pl.BlockSpec((1, n_kv * V_ROWS, tm), lambda b, i: (b, 0, i)),
        ],
        out_shape=[
            jax.ShapeDtypeStruct((B, n_q * HEAD_DIM, S), BF16),
            jax.ShapeDtypeStruct((B, S, n_kv * LANES), BF16),
            jax.ShapeDtypeStruct((B, n_kv * V_ROWS, S), BF16),
        ],
        compiler_params=_params(("arbitrary", "arbitrary")),
        name="pre1",
    )(h, g, wt, pos, inv)


def _swa_kernel(sink_ref, qt_ref, kp_ref, ko_ref, vp_ref, vo_ref, o_ref, *, n_kv, group):
    i = pl.program_id(1)
    W = WINDOW
    n_sub = qt_ref.shape[2] // W
    r = lax.broadcasted_iota(jnp.int32, (2 * W, W), 0)
    c = lax.broadcasted_iota(jnp.int32, (2 * W, W), 1)
    rel = c + W - r
    band = (rel >= 0) & (rel < W)
    valid = [jnp.concatenate([band & ((r >= W) | (i > 0)) if u == 0 else band] * group, axis=1)
             for u in range(n_sub)]
    seg = lax.broadcasted_iota(jnp.int32, (1, group * W), 1) // W
    zpad = jnp.zeros((LANES - HEAD_DIM, group * W), BF16)
    chains = [(u, g) for u in range(n_sub) for g in range(n_kv)]

    def keys(u, g):
        ls = slice(g * LANES, (g + 1) * LANES)
        if u == 0:
            return jnp.concatenate([kp_ref[0, :, ls], ko_ref[0, 0:W, ls]], axis=0)
        return ko_ref[0, (u - 1) * W:(u + 1) * W, ls]

    def values(u, g):
        rs = slice(g * V_ROWS, (g + 1) * V_ROWS)
        if u == 0:
            return jnp.concatenate([vp_ref[0, rs, :], vo_ref[0, rs, 0:W]], axis=1)
        return vo_ref[0, rs, (u - 1) * W:(u + 1) * W]

    st = []
    for u, g in chains:
        qg = jnp.concatenate(
            [qt_ref[0, (g * group + a) * HEAD_DIM:(g * group + a + 1) * HEAD_DIM, u * W:(u + 1) * W]
             for a in range(group)], axis=1)
        st.append(jnp.dot(keys(u, g), jnp.concatenate([qg, zpad], axis=0),
                          preferred_element_type=F32))
    p, sink_term = [], []
    for n, (u, g) in enumerate(chains):
        sg = jnp.where(valid[u], st[n], NEG_INF)
        sink = jnp.zeros((1, group * W), F32)
        for a in range(group):
            sink = jnp.where(seg == a, sink_ref[g * group + a] * LOG2E, sink)
        m = jnp.maximum(jnp.max(sg, axis=0, keepdims=True), sink)
        p.append(jnp.exp2(sg - m).astype(BF16))
        sink_term.append(jnp.exp2(sink - m))
    acc = [jnp.dot(values(u, g), p[n], preferred_element_type=F32)
           for n, (u, g) in enumerate(chains)]
    for n, (u, g) in enumerate(chains):
        o = acc[n][0:HEAD_DIM] / (acc[n][HEAD_DIM:HEAD_DIM + 1] + sink_term[n])
        for a in range(0, group, 2):
            pair = jnp.concatenate([o[:, a * W:(a + 1) * W], o[:, (a + 1) * W:(a + 2) * W]], axis=0)
            l0 = (g * group + a) * HEAD_DIM
            o_ref[0, u * W:(u + 1) * W, l0:l0 + 2 * HEAD_DIM] = pair.T.astype(o_ref.dtype)


def _swa(qt, kpad, vt, sinks, n_q, n_kv):
    B, _, S = qt.shape
    W = WINDOW
    n_sub = SWA_SUB
    prev = lambda i: jnp.maximum(n_sub * i - 1, 0)
    return pl.pallas_call(
        functools.partial(_swa_kernel, n_kv=n_kv, group=n_q // n_kv),
        grid_spec=pltpu.PrefetchScalarGridSpec(
            num_scalar_prefetch=1,
            grid=(B, S // (n_sub * W)),
            in_specs=[
                pl.BlockSpec((1, n_q * HEAD_DIM, n_sub * W), lambda b, i, s: (b, 0, i)),
                pl.BlockSpec((1, W, n_kv * LANES), lambda b, i, s: (b, prev(i), 0)),
                pl.BlockSpec((1, n_sub * W, n_kv * LANES), lambda b, i, s: (b, i, 0)),
                pl.BlockSpec((1, n_kv * V_ROWS, W), lambda b, i, s: (b, 0, prev(i))),
                pl.BlockSpec((1, n_kv * V_ROWS, n_sub * W), lambda b, i, s: (b, 0, i)),
            ],
            out_specs=pl.BlockSpec((1, n_sub * W, n_q * HEAD_DIM), lambda b, i, s: (b, i, 0)),
        ),
        out_shape=jax.ShapeDtypeStruct((B, S, n_q * HEAD_DIM), BF16),
        compiler_params=_params(("arbitrary", "arbitrary")),
        name="swa",
    )(sinks, qt, kpad, kpad, vt, vt)


def _post_kernel(*refs, n_mix, final_norm):
    h_ref = refs[0]
    mix_refs = refs[1:1 + n_mix]
    (p_ref, wo_ref, gf_ref, wg_ref, wu_ref, wd_ref, gp_ref, wpg_ref, wpp_ref, gfin_ref,
     o_ref) = refs[1 + n_mix:]
    h = h_ref[...]
    off = 0
    for m_ref in mix_refs:
        w = m_ref.shape[1]
        h = h + jnp.dot(m_ref[...], wo_ref[off:off + w, :], preferred_element_type=F32)
        off += w
    hb = _rms(h, gf_ref[...]).astype(BF16)
    g = jnp.dot(hb, wg_ref[...], preferred_element_type=F32)
    u = jnp.dot(hb, wu_ref[...], preferred_element_type=F32)
    act = (g * jax.nn.sigmoid(g) * u).astype(BF16)
    h = h + jnp.dot(act, wd_ref[...], preferred_element_type=F32)
    gate = jax.nn.sigmoid(jnp.dot(_rms(h, gp_ref[...]).astype(BF16), wpg_ref[...],
                                  preferred_element_type=F32))
    h = h + gate * jnp.dot(p_ref[...].astype(BF16), wpp_ref[...], preferred_element_type=F32)
    if final_norm:
        h = _rms(h, gfin_ref[...])
    o_ref[...] = h


def _post(h, mixes, p_all, layer, wo, gf, wg, wu, wd, gp, wpg, wpp, gfin, tm, final_norm):
    T, D = h.shape
    row = lambda w: pl.BlockSpec((tm, w), lambda i: (i, 0))
    full = lambda a: pl.BlockSpec(a.shape, lambda i: (0, 0))
    lay = lambda a: pl.BlockSpec((None,) + a.shape[1:], lambda i: (layer, 0, 0),
                                 pipeline_mode=pl.Buffered(1))
    return pl.pallas_call(
        functools.partial(_post_kernel, n_mix=len(mixes), final_norm=final_norm),
        grid=(T // tm,),
        in_specs=[row(D)] + [row(m.shape[1]) for m in mixes]
        + [pl.BlockSpec((None, tm, p_all.shape[2]), lambda i: (layer, i, 0))]
        + [full(wo)] + [lay(a) for a in (gf, wg, wu, wd, gp, wpg, wpp)] + [full(gfin)],
        out_specs=row(D),
        out_shape=jax.ShapeDtypeStruct((T, D), F32),
        compiler_params=_params(("arbitrary",)),
        name="post",
    )(h, *mixes, p_all, wo, gf, wg, wu, wd, gp, wpg, wpp, gfin)


def _layer0_weights(w_in, b_f, n_fox):
    fw = n_fox * HEAD_DIM
    scale = HEAD_DIM ** -0.5 * LOG2E
    D = w_in.shape[0]
    qa, ka, va, qs, ks, vs = (w_in[:, i * fw:(i + 1) * fw] for i in range(6))
    wt = jnp.concatenate([qa * scale, va, qs * scale, vs], axis=1).T.astype(BF16)
    wk = jnp.concatenate([ka, ks], axis=1).astype(BF16)
    gate_pad = ((0, 0), (0, LANES - 3 * n_fox))
    wf = jnp.pad(jnp.tile(w_in[:, 6 * fw:], (1, 3)), gate_pad).astype(BF16)
    bf = jnp.pad(jnp.tile(b_f.reshape(1, n_fox), (1, 3)), gate_pad)
    heads = np.arange(n_fox)
    pk = np.zeros((LANES, n_fox // 2 * LANES), np.float32)
    aq = np.zeros((n_fox * FEAT_ROWS, LANES), np.float32)
    for piece in range(3):
        pk[piece * n_fox + heads, heads // 2 * LANES + heads % 2 * FEAT_ROWS + 3 + piece] = -1.0
        aq[heads * FEAT_ROWS + piece, piece * n_fox + heads] = 1.0
    return wt, wk, wf, bf, jnp.asarray(pk, BF16), jnp.asarray(aq, BF16)


def kernel(x, p, positions, norm_mix, norm_ffn, norm_ple, norm_final, ev_w_in, ev_b_f, ev_w_out,
           od_w_in, od_sinks, od_w_out, ffn_w_gate, ffn_w_up, ffn_w_down, ple_w_proj, ple_w_gate):
    B, S, D = x.shape
    T = B * S
    n_heads = D // HEAD_DIM
    n_fox = n_heads // 2
    fox_w = n_fox * HEAD_DIM
    n_q, n_kv = n_heads, 4
    assert S % BLK == 0 and n_fox == 8
    row = lambda a: a.reshape(1, -1)

    wt, wk, wf, bf, pk, aq = _layer0_weights(ev_w_in[0], ev_b_f[0], n_fox)
    tri = jnp.asarray(np.arange(BLK)[None, :] > np.arange(BLK)[:, None], BF16)
    (qtf, vtf, kf, feat, gt, qn, fb, kmsq, o_sb) = _pre0(
        x, row(norm_mix[0]), wt, wk, wf, bf, pk, aq, tri, n_fox)
    fb_heads = fb[:, :, 0, :n_fox].transpose(0, 2, 1)
    o_fox = _fox(fb_heads, qtf, feat, kf, vtf, gt, qn, kmsq)

    tm = min(512, T)
    depth = norm_ffn.shape[0]
    p_all = p.reshape(depth, T, -1)
    stacked = (norm_ffn.reshape(depth, 1, D), ffn_w_gate.astype(BF16), ffn_w_up.astype(BF16),
               ffn_w_down.astype(BF16), norm_ple.reshape(depth, 1, D), ple_w_gate.astype(BF16),
               ple_w_proj.astype(BF16))
    h = _post(x.reshape(T, D), [o_fox.reshape(T, fox_w), o_sb.reshape(T, fox_w)], p_all, 0,
              ev_w_out[0].astype(BF16), *stacked, row(norm_final), tm, final_norm=False)

    qw = n_q * HEAD_DIM
    kw = n_kv * HEAD_DIM
    col_scale1 = jnp.ones((qw + 2 * kw,), F32).at[:qw].set(HEAD_DIM ** -0.5 * LOG2E)
    w1t = (od_w_in[0] * col_scale1).T.astype(BF16)
    half = HEAD_DIM // 2
    inv = (ROPE_THETA ** (-jnp.arange(half, dtype=F32) / half)).reshape(half, 1)
    qt1, k1, vt1 = _pre1(h.reshape(B, S, D), row(norm_mix[1]), w1t, positions.reshape(B, 1, S),
                         inv, min(2 * tm, S), n_q, n_kv)
    o_swa = _swa(qt1, k1, vt1, od_sinks[0], n_q, n_kv)
    out = _post(h, [o_swa.reshape(T, qw)], p_all, 1, od_w_out[0].astype(BF16), *stacked,
                row(norm_final), tm, final_norm=True)
    return out.reshape(B, S, D)
```

```python
import functools

import jax
import jax.numpy as jnp
import numpy as np
from jax import lax
from jax.experimental import pallas as pl
from jax.experimental.pallas import tpu as pltpu

F32 = jnp.float32
BF16 = jnp.bfloat16

HEAD_DIM = 64
LANES = 128
BLK = 256
V_ROWS = 80
FEAT_ROWS = 16
WINDOW = 128
ROPE_THETA = 10000.0
EPS = 1e-6
NEG_INF = -1e30
LOG2E = 1.4426950408889634
SKIP_LOG2 = 60.0 * LOG2E
FOX_HEADS = 8
SWA_SUB = 8
VMEM_LIMIT = 56 * 1024 * 1024

_NT = (((1,), (1,)), ((), ()))


def _params(sem):
    return pltpu.CompilerParams(dimension_semantics=sem, vmem_limit_bytes=VMEM_LIMIT)


def _rms(x, g):
    return x * lax.rsqrt(jnp.mean(x * x, axis=-1, keepdims=True) + EPS) * g


def _log_sigmoid(x):
    return jnp.minimum(x, 0.0) - jnp.log(1.0 + jnp.exp(-jnp.abs(x)))


def _split3(x):
    a = x.astype(BF16)
    r = x - a.astype(F32)
    b = r.astype(BF16)
    c = (r - b.astype(F32)).astype(BF16)
    return a, b, c


def _pre0_kernel(x_ref, g_ref, wt_ref, wk_ref, wf_ref, bf_ref, pk_ref, aq_ref, tri_ref,
                 qtf_ref, vtf_ref, kf_ref, feat_ref, gt_ref, qn_ref, fb_ref, kmsq_ref, osb_ref,
                 carry_ref, qts_s, ks_s, vts_s, r_ref, acc_ref, *, n_fox):
    tm = x_ref.shape[1]
    fw = n_fox * HEAD_DIM
    i = pl.program_id(1)
    nb = pl.num_programs(1) - 1
    extra = i == nb
    t = jnp.minimum(i, nb - 1)
    qb = jnp.maximum(i - 1, 0)
    heads = range(n_fox)

    @pl.when(i == 0)
    def _():
        carry_ref[...] = jnp.zeros_like(carry_ref)
        qts_s[...] = jnp.zeros_like(qts_s)
        ks_s[0:tm, :] = jnp.zeros((tm, fw), BF16)
        vts_s[:, 0] = jnp.zeros((n_fox, HEAD_DIM, tm), BF16)

    rowi = lax.broadcasted_iota(jnp.int32, (BLK, tm), 0)
    coli = lax.broadcasted_iota(jnp.int32, (BLK, tm), 1)
    strict = rowi < coli
    top = lax.broadcasted_iota(jnp.int32, (LANES, tm), 0) < HEAD_DIM
    zero = jnp.zeros((LANES, tm), BF16)
    qh = []
    for h in heads:
        pair = qts_s[(h // 2) * LANES:(h // 2 + 1) * LANES, :]
        qh.append(jnp.where(top, pair, zero) if h % 2 == 0 else jnp.where(top, zero, pair))
    tri_sb = tri_ref[...]

    def kpair(k0, h):
        return ks_s[pl.ds(k0, BLK), (h // 2) * LANES:(h // 2 + 1) * LANES]

    def stick_logs(z, mask=None):
        nz = -z
        l1 = jnp.minimum(nz, 0.0) - jnp.log2(1.0 + jnp.exp2(jnp.minimum(z, nz)))
        lb = z + l1
        if mask is not None:
            l1 = jnp.where(mask, l1, 0.0)
        return lb, l1.astype(BF16), l1[0:1, :]

    def suffix(l1b):
        return jnp.dot(tri_sb, l1b, preferred_element_type=F32)

    def tproj(c, half):
        r0 = c * fw + half * (fw // 2)
        return lax.dot_general(wt_ref[r0:r0 + fw // 2, :], hb, _NT,
                               preferred_element_type=F32).astype(BF16)

    jp = jnp.maximum(qb - 1, 0)
    kp0 = pl.multiple_of(jp * BLK, BLK)
    kd0 = pl.multiple_of(qb * BLK, BLK)
    has_prev = qb > 0
    cp = jnp.where(has_prev, 0.0, NEG_INF)

    hb = _rms(x_ref[0], g_ref[...]).astype(BF16)
    gate = jnp.dot(hb, wf_ref[...], preferred_element_type=F32) + bf_ref[...]
    lf = _log_sigmoid(gate) * LOG2E

    def gate_stage(k, c):
        if k == 0:
            row = lax.broadcasted_iota(jnp.int32, (tm, tm), 0)
            col = lax.broadcasted_iota(jnp.int32, (tm, tm), 1)
            tri = jnp.where(row >= col, 1.0, 0.0).astype(BF16)
            G = jnp.zeros((tm, LANES), F32)
            for piece in _split3(lf):
                G = G + jnp.dot(tri, piece, preferred_element_type=F32)
            base = jnp.where(extra, carry_ref[1], carry_ref[0])
            fb_ref[0, 0] = base
            carry_ref[1] = base
            carry_ref[0] = base + G[tm - 1:tm, :]
            c["G"] = G
        elif k == 1:
            g_hi, g_mid, g_lo = _split3(c["G"])
            lane_t = lax.broadcasted_iota(jnp.int32, (tm, LANES), 1)
            gp = jnp.where(lane_t < n_fox, g_hi, jnp.where(lane_t < 2 * n_fox, g_mid, g_lo))
            c["kfeat"] = jnp.dot(gp, pk_ref[...], preferred_element_type=F32)
        elif k == 2:
            GT = c["G"].T
            gt_ref[0] = GT[0:8, :]
            t_hi, t_mid, t_lo = _split3(GT)
            row_t = lax.broadcasted_iota(jnp.int32, (LANES, tm), 0)
            c["gpt"] = jnp.where(row_t < n_fox, t_hi, jnp.where(row_t < 2 * n_fox, t_mid, t_lo))
        elif k == 3:
            qfeat = jnp.dot(aq_ref[...], c["gpt"], preferred_element_type=F32)
            frow = lax.broadcasted_iota(jnp.int32, (n_fox * FEAT_ROWS, 1), 0) % FEAT_ROWS
            qones = jnp.where((frow >= 3) & (frow < 6), 1.0, 0.0)
            feat_ref[0] = (qfeat + qones).astype(BF16)

    lp, ld, tp, sfp, sfd, chain = [], [], [], [], [], {}
    for h in heads:
        z2 = jnp.dot(jnp.concatenate([kpair(kp0, h), kpair(kd0, h)], axis=0), qh[h],
                     preferred_element_type=F32)
        lp.append(stick_logs(z2[0:BLK]))
        ld.append(stick_logs(z2[BLK:2 * BLK], strict))
        tp.append(tproj(h // 2, h % 2))
        if h >= 1:
            sfp.append(suffix(lp[h - 1][1]))
            sfd.append(suffix(ld[h - 1][1]))
        if 1 <= h < 5:
            gate_stage(h - 1, chain)
    sfp.append(suffix(lp[n_fox - 1][1]))
    sfd.append(suffix(ld[n_fox - 1][1]))
    qtf = jnp.concatenate(tp[0:2], axis=0)
    vtf = jnp.concatenate(tp[2:4], axis=0)
    qts_new = jnp.concatenate(tp[4:6], axis=0)
    vts = jnp.concatenate(tp[6:8], axis=0)
    kfeat = chain["kfeat"]

    a_first, kk = [], []
    nk = 2 * wk_ref.shape[1] // n_fox
    for h in heads:
        tot_d = sfd[h][0:1, :] + ld[h][2]
        tot_p = sfp[h][0:1, :] + lp[h][2]
        a_d = jnp.where(strict, jnp.exp2(ld[h][0] + sfd[h]), 0.0)
        a_p = jnp.exp2(lp[h][0] + sfp[h] + (tot_d + cp))
        a_first.append(jnp.concatenate([a_p, a_d], axis=0).astype(BF16))
        r_ref[h] = tot_d + jnp.where(has_prev, tot_p, 0.0)
        if h % 2 == 0:
            kk.append(jnp.dot(hb, wk_ref[:, (h // 2) * nk:(h // 2 + 1) * nk],
                              preferred_element_type=F32))
    kk = jnp.concatenate(kk, axis=1)
    for h in heads:
        vt2 = jnp.concatenate([vts_s[h, jp], vts_s[h, qb]], axis=1)
        acc_ref[h] = jnp.dot(vt2, a_first[h], preferred_element_type=F32)

    qtf_ref[0] = qtf
    lane_row = lax.broadcasted_iota(jnp.int32, (V_ROWS - HEAD_DIM, tm), 0)
    ones_pad = jnp.where(lane_row == 0, 1.0, 0.0).astype(BF16)
    for h in heads:
        vtf_ref[0, h, 0, 0:HEAD_DIM, :] = vtf[h * HEAD_DIM:(h + 1) * HEAD_DIM, :]
        vtf_ref[0, h, 0, HEAD_DIM:V_ROWS, :] = ones_pad
    qts_s[...] = qts_new
    for h in heads:
        vts_s[h, t] = vts[h * HEAD_DIM:(h + 1) * HEAD_DIM, :]

    q32 = qtf.astype(F32)
    qn_rows = [jnp.sqrt(jnp.sum(jnp.square(q32[h * HEAD_DIM:(h + 1) * HEAD_DIM, :]),
                                axis=0, keepdims=True)) for h in heads]
    qn_ref[0] = jnp.concatenate(qn_rows, axis=0)

    ks_s[pl.ds(pl.multiple_of(t * tm, tm), tm), :] = kk[:, fw:].astype(BF16)
    kfox = kk[:, :fw].astype(BF16)
    k32 = kfox.astype(F32)
    lane = lax.broadcasted_iota(jnp.int32, (1, LANES), 1)
    first_head = lane < HEAD_DIM
    kmsq = jnp.zeros((1, LANES), F32)
    for pr in range(n_fox // 2):
        sq = jnp.square(k32[:, pr * LANES:(pr + 1) * LANES])
        for e in range(2):
            mine = first_head if e == 0 else jnp.logical_not(first_head)
            ss = jnp.sum(jnp.where(mine, sq, 0.0), axis=-1, keepdims=True)
            kmsq = jnp.where(lane == 2 * pr + e, jnp.max(ss, axis=0, keepdims=True), kmsq)
    kmsq_ref[0, 0] = kmsq

    kones = jnp.where((lane % FEAT_ROWS < 3) & (lane < 2 * FEAT_ROWS), 1.0, 0.0)
    for pr in range(n_fox // 2):
        kf_ref[0, :, 2 * pr * LANES:(2 * pr + 1) * LANES] = kfox[:, pr * LANES:(pr + 1) * LANES]
        kf_ref[0, :, (2 * pr + 1) * LANES:(2 * pr + 2) * LANES] = (
            kfeat[:, pr * LANES:(pr + 1) * LANES] + kones).astype(BF16)

    def rmax():
        out = r_ref[0]
        for h in heads[1:]:
            out = jnp.maximum(out, r_ref[h])
        return jnp.max(out)

    def block(j):
        k0 = pl.multiple_of(j * BLK, BLK)
        z = [jnp.dot(kpair(k0, h), qh[h], preferred_element_type=F32) for h in heads]
        lg = [stick_logs(z[h]) for h in heads]
        sfx = [suffix(lg[h][1]) for h in heads]
        a = []
        for h in heads:
            r_old = r_ref[h]
            a.append(jnp.exp2(lg[h][0] + sfx[h] + r_old).astype(BF16))
            r_ref[h] = r_old + (sfx[h][0:1, :] + lg[h][2])
        for h in heads:
            acc_ref[h] = acc_ref[h] + jnp.dot(vts_s[h, j], a[h], preferred_element_type=F32)
        return rmax()

    def body(carry):
        j, _ = carry
        return j - 1, block(j)

    lax.while_loop(lambda c: (c[0] >= 0) & (c[1] > -SKIP_LOG2), body, (qb - 2, rmax()))

    osb_ref[0] = jnp.concatenate([acc_ref[h] for h in heads], axis=0).T.astype(osb_ref.dtype)


def _pre0(x, g, wt, wk, wf, bf, pk, aq, tri, n_fox):
    B, S, D = x.shape
    tm = BLK
    nb = S // tm
    fw = n_fox * HEAD_DIM
    const = lambda a: pl.BlockSpec(a.shape, lambda b, s: (0,) * a.ndim)
    tile = lambda s: jnp.minimum(s, nb - 1)
    tok_lane = lambda rows: pl.BlockSpec((1, rows, tm), lambda b, s: (b, 0, tile(s)))
    return pl.pallas_call(
        functools.partial(_pre0_kernel, n_fox=n_fox),
        grid=(B, nb + 1),
        in_specs=[pl.BlockSpec((1, tm, D), lambda b, s: (b, tile(s), 0))]
        + [const(a) for a in (g, wt, wk, wf, bf, pk, aq, tri)],
        out_specs=[
            tok_lane(fw),
            pl.BlockSpec((1, n_fox, 1, V_ROWS, tm), lambda b, s: (b, 0, tile(s), 0, 0)),
            pl.BlockSpec((1, tm, n_fox * LANES), lambda b, s: (b, tile(s), 0)),
            tok_lane(n_fox * FEAT_ROWS),
            tok_lane(8),
            tok_lane(8),
            pl.BlockSpec((1, 1, 1, LANES), lambda b, s: (b, tile(s), 0, 0)),
            pl.BlockSpec((1, 1, 1, LANES), lambda b, s: (b, tile(s), 0, 0)),
            pl.BlockSpec((1, tm, fw), lambda b, s: (b, jnp.maximum(s - 1, 0), 0)),
        ],
        out_shape=[
            jax.ShapeDtypeStruct((B, fw, S), BF16),
            jax.ShapeDtypeStruct((B, n_fox, nb, V_ROWS, tm), BF16),
            jax.ShapeDtypeStruct((B, S, n_fox * LANES), BF16),
            jax.ShapeDtypeStruct((B, n_fox * FEAT_ROWS, S), BF16),
            jax.ShapeDtypeStruct((B, 8, S), F32),
            jax.ShapeDtypeStruct((B, 8, S), F32),
            jax.ShapeDtypeStruct((B, nb, 1, LANES), F32),
            jax.ShapeDtypeStruct((B, nb, 1, LANES), F32),
            jax.ShapeDtypeStruct((B, S, fw), BF16),
        ],
        scratch_shapes=[
            pltpu.VMEM((2, 1, LANES), F32),
            pltpu.VMEM((fw, tm), BF16),
            pltpu.VMEM((S, fw), BF16),
            pltpu.VMEM((n_fox, nb, HEAD_DIM, tm), BF16),
            pltpu.VMEM((n_fox, 1, tm), F32),
            pltpu.VMEM((n_fox, HEAD_DIM, tm), F32),
        ],
        compiler_params=_params(("arbitrary", "arbitrary")),
        name="pre0",
    )(x, g, wt, wk, wf, bf, pk, aq, tri)


def _fox_kernel(fb_ref, qt_ref, feat_ref, k_ref, vt_ref, gt_ref, qn_ref, kmsq_ref,
                o_ref, m_ref, acc_ref):
    b = pl.program_id(0)
    hp = pl.program_id(1)
    qi = pl.program_id(2)
    bq = qt_ref.shape[2]
    n_h = acc_ref.shape[0]
    heads = range(n_h)
    hg = [n_h * hp + h for h in heads]
    row = lax.broadcasted_iota(jnp.int32, (BLK, bq), 0)
    col = lax.broadcasted_iota(jnp.int32, (BLK, bq), 1)
    causal = row <= col
    kmax_sq = jnp.max(kmsq_ref[0], axis=0)
    lane = lax.broadcasted_iota(jnp.int32, (1, LANES), 1)

    def zeros(rows):
        return jnp.zeros((rows, bq), BF16)

    qaug = []
    for h in heads:
        q = qt_ref[0, h * HEAD_DIM:(h + 1) * HEAD_DIM, :]
        f = feat_ref[0, h * FEAT_ROWS:(h + 1) * FEAT_ROWS, :]
        parts = [q, zeros(HEAD_DIM), f, zeros(FEAT_ROWS)] if h % 2 == 0 else \
                [zeros(HEAD_DIM), q, zeros(FEAT_ROWS), f]
        qaug.append(jnp.concatenate(parts + [zeros(LANES - 2 * FEAT_ROWS)], axis=0))

    def kpair(k0, h):
        return k_ref[0, pl.ds(k0, BLK), (h // 2) * 2 * LANES:(h // 2 + 1) * 2 * LANES]

    jp = jnp.maximum(qi - 1, 0)
    kp0 = pl.multiple_of(jp * BLK, BLK)
    kd0 = pl.multiple_of(qi * BLK, BLK)
    st = [jnp.dot(jnp.concatenate([kpair(kp0, h), kpair(kd0, h)], axis=0),
                  qaug[h], preferred_element_type=F32) for h in heads]
    p_first = []
    for h in heads:
        cp = jnp.where(qi > 0, fb_ref[b, hg[h], qi] - fb_ref[b, hg[h], jp], NEG_INF)
        st_p = st[h][0:BLK]
        st_d = jnp.where(causal, st[h][BLK:2 * BLK], NEG_INF)
        m = jnp.maximum(jnp.max(st_d, axis=0, keepdims=True),
                        jnp.max(st_p, axis=0, keepdims=True) + cp)
        m_ref[h] = m
        p_first.append(jnp.concatenate([jnp.exp2(st_p - (m - cp)), jnp.exp2(st_d - m)],
                                       axis=0).astype(BF16))
    for h in heads:
        vt2 = jnp.concatenate([vt_ref[0, h, jp], vt_ref[0, h, qi]], axis=1)
        acc_ref[h] = jnp.dot(vt2, p_first[h], preferred_element_type=F32)

    def block(j):
        k0 = pl.multiple_of(j * BLK, BLK)
        st = [jnp.dot(kpair(k0, h), qaug[h], preferred_element_type=F32)
              for h in heads]
        p, alpha = [], []
        for h in heads:
            c = fb_ref[b, hg[h], qi] - fb_ref[b, hg[h], j]
            m_old = m_ref[h]
            m_new = jnp.maximum(m_old, jnp.max(st[h], axis=0, keepdims=True) + c)
            p.append(jnp.exp2(st[h] - (m_new - c)).astype(BF16))
            alpha.append(jnp.exp2(m_old - m_new))
            m_ref[h] = m_new
        for h in heads:
            acc_ref[h] = alpha[h] * acc_ref[h] + jnp.dot(vt_ref[0, h, j], p[h],
                                                         preferred_element_type=F32)

    def margin(j):
        jn = jnp.clip(j + 1, 0, qi)
        worst = None
        for h in heads:
            kmax = jnp.sqrt(jnp.sum(jnp.where(lane == hg[h], kmax_sq, 0.0), axis=-1, keepdims=True))
            gap = fb_ref[b, hg[h], jn] - fb_ref[b, hg[h], qi]
            mh = (qn_ref[0, pl.ds(hg[h], 1), :] * kmax + gt_ref[0, pl.ds(hg[h], 1), :]
                  - m_ref[h] - gap)
            worst = mh if worst is None else jnp.maximum(worst, mh)
        return jnp.max(worst) + SKIP_LOG2

    def body(carry):
        j, _ = carry
        block(j)
        return j - 1, margin(j - 1)

    lax.while_loop(lambda c: (c[0] >= 0) & (c[1] >= 0.0), body, (qi - 2, margin(qi - 2)))

    out_t = jnp.concatenate(
        [acc_ref[h, 0:HEAD_DIM, :] / acc_ref[h, HEAD_DIM:HEAD_DIM + 1, :] for h in heads], axis=0)
    o_ref[0] = out_t.T.astype(o_ref.dtype)


def _fox(fb, qt, feat, kf, vt, gt, qn, kmsq):
    B, _, S = qt.shape
    n_h = FOX_HEADS
    n_grp = qt.shape[1] // (n_h * HEAD_DIM)
    nb = S // BLK
    return pl.pallas_call(
        _fox_kernel,
        grid_spec=pltpu.PrefetchScalarGridSpec(
            num_scalar_prefetch=1,
            grid=(B, n_grp, nb),
            in_specs=[
                pl.BlockSpec((1, n_h * HEAD_DIM, BLK), lambda b, h, i, s: (b, h, i)),
                pl.BlockSpec((1, n_h * FEAT_ROWS, BLK), lambda b, h, i, s: (b, h, i)),
                pl.BlockSpec((1, S, n_h * LANES), lambda b, h, i, s: (b, 0, h),
                             pipeline_mode=pl.Buffered(1)),
                pl.BlockSpec((1, n_h, nb, V_ROWS, BLK), lambda b, h, i, s: (b, h, 0, 0, 0),
                             pipeline_mode=pl.Buffered(1)),
                pl.BlockSpec((1, 8, BLK), lambda b, h, i, s: (b, 0, i)),
                pl.BlockSpec((1, 8, BLK), lambda b, h, i, s: (b, 0, i)),
                pl.BlockSpec((1, nb, 1, LANES), lambda b, h, i, s: (b, 0, 0, 0)),
            ],
            out_specs=pl.BlockSpec((1, BLK, n_h * HEAD_DIM), lambda b, h, i, s: (b, i, h)),
            scratch_shapes=[
                pltpu.VMEM((n_h, 1, BLK), F32),
                pltpu.VMEM((n_h, V_ROWS, BLK), F32),
            ],
        ),
        out_shape=jax.ShapeDtypeStruct((B, S, n_grp * n_h * HEAD_DIM), BF16),
        compiler_params=_params(("arbitrary", "arbitrary", "arbitrary")),
        name="fox",
    )(fb, qt, feat, kf, vt, gt, qn, kmsq)


def _pre1_kernel(x_ref, g_ref, wt_ref, pos_ref, inv_ref, qt_ref, k_ref, vt_ref, *, n_q, n_kv):
    tm = x_ref.shape[1]
    half = HEAD_DIM // 2
    hb = _rms(x_ref[0], g_ref[...]).astype(BF16)
    ang = inv_ref[...] * pos_ref[0].astype(F32)
    cos = jnp.cos(ang)
    sin = jnp.sin(ang)

    def proj_t(r0, rows):
        return lax.dot_general(wt_ref[r0:r0 + rows, :], hb, _NT, preferred_element_type=F32)

    def rope_t(x):
        x1, x2 = x[0:half], x[half:HEAD_DIM]
        return jnp.concatenate([x1 * cos - x2 * sin, x2 * cos + x1 * sin], axis=0)

    qw = n_q * HEAD_DIM
    for c in range(n_q // 4):
        qt = proj_t(c * 4 * HEAD_DIM, 4 * HEAD_DIM)
        for a in range(4):
            hq = c * 4 + a
            qt_ref[0, hq * HEAD_DIM:(hq + 1) * HEAD_DIM, :] = rope_t(
                qt[a * HEAD_DIM:(a + 1) * HEAD_DIM]).astype(BF16)
    kt = proj_t(qw, n_kv * HEAD_DIM)
    zpad = jnp.zeros((LANES - HEAD_DIM, tm), F32)
    for g in range(n_kv):
        kg = jnp.concatenate([rope_t(kt[g * HEAD_DIM:(g + 1) * HEAD_DIM]), zpad], axis=0)
        k_ref[0, :, g * LANES:(g + 1) * LANES] = kg.T.astype(BF16)
    vt = proj_t(qw + n_kv * HEAD_DIM, n_kv * HEAD_DIM).astype(BF16)
    pad_row = lax.broadcasted_iota(jnp.int32, (V_ROWS - HEAD_DIM, tm), 0)
    ones_pad = jnp.where(pad_row == 0, 1.0, 0.0).astype(BF16)
    for g in range(n_kv):
        vt_ref[0, g * V_ROWS:g * V_ROWS + HEAD_DIM, :] = vt[g * HEAD_DIM:(g + 1) * HEAD_DIM]
        vt_ref[0, g * V_ROWS + HEAD_DIM:(g + 1) * V_ROWS, :] = ones_pad


def _pre1(h, g, wt, pos, inv, tm, n_q, n_kv):
    B, S, D = h.shape
    return pl.pallas_call(
        functools.partial(_pre1_kernel, n_q=n_q, n_kv=n_kv),
        grid=(B, S // tm),
        in_specs=[
            pl.BlockSpec((1, tm, D), lambda b, i: (b, i, 0)),
            pl.BlockSpec((1, D), lambda b, i: (0, 0)),
            pl.BlockSpec(wt.shape, lambda b, i: (0, 0)),
            pl.BlockSpec((1, 1, tm), lambda b, i: (b, 0, i)),
            pl.BlockSpec(inv.shape, lambda b, i: (0, 0)),
        ],
        out_specs=[
            pl.BlockSpec((1, n_q * HEAD_DIM, tm), lambda b, i: (b, 0, i)),
            pl.BlockSpec((1, tm, n_kv * LANES), lambda b, i: (b, i, 0)),
            pl.BlockSpec((1, n_kv * V_ROWS, tm), lambda b, i: (b, 0, i)),
        ],
        out_shape=[
            jax.ShapeDtypeStruct((B, n_q * HEAD_DIM, S), BF16),
            jax.ShapeDtypeStruct((B, S, n_kv * LANES), BF16),
            jax.ShapeDtypeStruct((B, n_kv * V_ROWS, S), BF16),
        ],
        compiler_params=_params(("arbitrary", "arbitrary")),
        name="pre1",
    )(h, g, wt, pos, inv)


def _swa_kernel(sink_ref, qt_ref, kp_ref, ko_ref, vp_ref, vo_ref, o_ref, *, n_kv, group):
    i = pl.program_id(1)
    W = WINDOW
    n_sub = qt_ref.shape[2] // W
    r = lax.broadcasted_iota(jnp.int32, (2 * W, W), 0)
    c = lax.broadcasted_iota(jnp.int32, (2 * W, W), 1)
    rel = c + W - r
    band = (rel >= 0) & (rel < W)
    valid = [jnp.concatenate([band & ((r >= W) | (i > 0)) if u == 0 else band] * group, axis=1)
             for u in range(n_sub)]
    seg = lax.broadcasted_iota(jnp.int32, (1, group * W), 1) // W
    zpad = jnp.zeros((LANES - HEAD_DIM, group * W), BF16)
    chains = [(u, g) for u in range(n_sub) for g in range(n_kv)]

    def keys(u, g):
        ls = slice(g * LANES, (g + 1) * LANES)
        if u == 0:
            return jnp.concatenate([kp_ref[0, :, ls], ko_ref[0, 0:W, ls]], axis=0)
        return ko_ref[0, (u - 1) * W:(u + 1) * W, ls]

    def values(u, g):
        rs = slice(g * V_ROWS, (g + 1) * V_ROWS)
        if u == 0:
            return jnp.concatenate([vp_ref[0, rs, :], vo_ref[0, rs, 0:W]], axis=1)
        return vo_ref[0, rs, (u - 1) * W:(u + 1) * W]

    st = []
    for u, g in chains:
        qg = jnp.concatenate(
            [qt_ref[0, (g * group + a) * HEAD_DIM:(g * group + a + 1) * HEAD_DIM, u * W:(u + 1) * W]
             for a in range(group)], axis=1)
        st.append(jnp.dot(keys(u, g), jnp.concatenate([qg, zpad], axis=0),
                          preferred_element_type=F32))
    p, sink_term = [], []
    for n, (u, g) in enumerate(chains):
        sg = jnp.where(valid[u], st[n], NEG_INF)
        sink = jnp.zeros((1, group * W), F32)
        for a in range(group):
            sink = jnp.where(seg == a, sink_ref[g * group + a] * LOG2E, sink)
        m = jnp.maximum(jnp.max(sg, axis=0, keepdims=True), sink)
        p.append(jnp.exp2(sg - m).astype(BF16))
        sink_term.append(jnp.exp2(sink - m))
    acc = [jnp.dot(values(u, g), p[n], preferred_element_type=F32)
           for n, (u, g) in enumerate(chains)]
    for n, (u, g) in enumerate(chains):
        o = acc[n][0:HEAD_DIM] / (acc[n][HEAD_DIM:HEAD_DIM + 1] + sink_term[n])
        for a in range(0, group, 2):
            pair = jnp.concatenate([o[:, a * W:(a + 1) * W], o[:, (a + 1) * W:(a + 2) * W]], axis=0)
            l0 = (g * group + a) * HEAD_DIM
            o_ref[0, u * W:(u + 1) * W, l0:l0 + 2 * HEAD_DIM] = pair.T.astype(o_ref.dtype)


def _swa(qt, kpad, vt, sinks, n_q, n_kv):
    B, _, S = qt.shape
    W = WINDOW
    n_sub = SWA_SUB
    prev = lambda i: jnp.maximum(n_sub * i - 1, 0)
    return pl.pallas_call(
        functools.partial(_swa_kernel, n_kv=n_kv, group=n_q // n_kv),
        grid_spec=pltpu.PrefetchScalarGridSpec(
            num_scalar_prefetch=1,
            grid=(B, S // (n_sub * W)),
            in_specs=[
                pl.BlockSpec((1, n_q * HEAD_DIM, n_sub * W), lambda b, i, s: (b, 0, i)),
                pl.BlockSpec((1, W, n_kv * LANES), lambda b, i, s: (b, prev(i), 0)),
                pl.BlockSpec((1, n_sub * W, n_kv * LANES), lambda b, i, s: (b, i, 0)),
                pl.BlockSpec((1, n_kv * V_ROWS, W), lambda b, i, s: (b, 0, prev(i))),
                pl.BlockSpec((1, n_kv * V_ROWS, n_sub * W), lambda b, i, s: (b, 0, i)),
            ],
            out_specs=pl.BlockSpec((1, n_sub * W, n_q * HEAD_DIM), lambda b, i, s: (b, i, 0)),
        ),
        out_shape=jax.ShapeDtypeStruct((B, S, n_q * HEAD_DIM), BF16),
        compiler_params=_params(("arbitrary", "arbitrary")),
        name="swa",
    )(sinks, qt, kpad, kpad, vt, vt)


def _post_kernel(*refs, n_mix, final_norm):
    h_ref = refs[0]
    mix_refs = refs[1:1 + n_mix]
    (p_ref, wo_ref, gf_ref, wg_ref, wu_ref, wd_ref, gp_ref, wpg_ref, wpp_ref, gfin_ref,
     o_ref) = refs[1 + n_mix:]
    h = h_ref[...]
    off = 0
    for m_ref in mix_refs:
        w = m_ref.shape[1]
        h = h + jnp.dot(m_ref[...], wo_ref[off:off + w, :], preferred_element_type=F32)
        off += w
    pb = p_ref[...].astype(BF16)
    half = wpp_ref.shape[1] // 2
    pp_lo = jnp.dot(pb, wpp_ref[:, :half], preferred_element_type=F32)
    hb = _rms(h, gf_ref[...]).astype(BF16)
    g = jnp.dot(hb, wg_ref[...], preferred_element_type=F32)
    u = jnp.dot(hb, wu_ref[...], preferred_element_type=F32)
    act = (g * jax.nn.sigmoid(g) * u).astype(BF16)
    h = h + jnp.dot(act, wd_ref[...], preferred_element_type=F32)
    pp_hi = jnp.dot(pb, wpp_ref[:, half:], preferred_element_type=F32)
    gate = jax.nn.sigmoid(jnp.dot(_rms(h, gp_ref[...]).astype(BF16), wpg_ref[...],
                                  preferred_element_type=F32))
    h = h + gate * jnp.concatenate([pp_lo, pp_hi], axis=1)
    if final_norm:
        h = _rms(h, gfin_ref[...])
    o_ref[...] = h


def _post(h, mixes, p_all, layer, wo, gf, wg, wu, wd, gp, wpg, wpp, gfin, tm, final_norm):
    T, D = h.shape
    row = lambda w: pl.BlockSpec((tm, w), lambda i: (i, 0))
    full = lambda a: pl.BlockSpec(a.shape, lambda i: (0, 0))
    lay = lambda a: pl.BlockSpec((None,) + a.shape[1:], lambda i: (layer, 0, 0),
                                 pipeline_mode=pl.Buffered(1))
    return pl.pallas_call(
        functools.partial(_post_kernel, n_mix=len(mixes), final_norm=final_norm),
        grid=(T // tm,),
        in_specs=[row(D)] + [row(m.shape[1]) for m in mixes]
        + [pl.BlockSpec((None, tm, p_all.shape[2]), lambda i: (layer, i, 0))]
        + [full(wo)] + [lay(a) for a in (gf, wg, wu, wd, gp, wpg, wpp)] + [full(gfin)],
        out_specs=row(D),
        out_shape=jax.ShapeDtypeStruct((T, D), F32),
        compiler_params=_params(("arbitrary",)),
        name="post",
    )(h, *mixes, p_all, wo, gf, wg, wu, wd, gp, wpg, wpp, gfin)


def _layer0_weights(w_in, b_f, n_fox):
    fw = n_fox * HEAD_DIM
    scale = HEAD_DIM ** -0.5 * LOG2E
    D = w_in.shape[0]
    qa, ka, va, qs, ks, vs = (w_in[:, i * fw:(i + 1) * fw] for i in range(6))
    wt = jnp.concatenate([qa * scale, va, qs * scale, vs], axis=1).T.astype(BF16)
    wk = jnp.concatenate([ka, ks], axis=1).astype(BF16)
    gate_pad = ((0, 0), (0, LANES - 3 * n_fox))
    wf = jnp.pad(jnp.tile(w_in[:, 6 * fw:], (1, 3)), gate_pad).astype(BF16)
    bf = jnp.pad(jnp.tile(b_f.reshape(1, n_fox), (1, 3)), gate_pad)
    heads = np.arange(n_fox)
    pk = np.zeros((LANES, n_fox // 2 * LANES), np.float32)
    aq = np.zeros((n_fox * FEAT_ROWS, LANES), np.float32)
    for piece in range(3):
        pk[piece * n_fox + heads, heads // 2 * LANES + heads % 2 * FEAT_ROWS + 3 + piece] = -1.0
        aq[heads * FEAT_ROWS + piece, piece * n_fox + heads] = 1.0
    return wt, wk, wf, bf, jnp.asarray(pk, BF16), jnp.asarray(aq, BF16)


def kernel(x, p, positions, norm_mix, norm_ffn, norm_ple, norm_final, ev_w_in, ev_b_f, ev_w_out,
           od_w_in, od_sinks, od_w_out, ffn_w_gate, ffn_w_up, ffn_w_down, ple_w_proj, ple_w_gate):
    B, S, D = x.shape
    T = B * S
    n_heads = D // HEAD_DIM
    n_fox = n_heads // 2
    fox_w = n_fox * HEAD_DIM
    n_q, n_kv = n_heads, 4
    assert S % BLK == 0 and n_fox == 8
    row = lambda a: a.reshape(1, -1)

    wt, wk, wf, bf, pk, aq = _layer0_weights(ev_w_in[0], ev_b_f[0], n_fox)
    tri = jnp.asarray(np.arange(BLK)[None, :] > np.arange(BLK)[:, None], BF16)
    (qtf, vtf, kf, feat, gt, qn, fb, kmsq, o_sb) = _pre0(
        x, row(norm_mix[0]), wt, wk, wf, bf, pk, aq, tri, n_fox)
    fb_heads = fb[:, :, 0, :n_fox].transpose(0, 2, 1)
    o_fox = _fox(fb_heads, qtf, feat, kf, vtf, gt, qn, kmsq)

    tm = min(512, T)
    depth = norm_ffn.shape[0]
    p_all = p.reshape(depth, T, -1)
    stacked = (norm_ffn.reshape(depth, 1, D), ffn_w_gate.astype(BF16), ffn_w_up.astype(BF16),
               ffn_w_down.astype(BF16), norm_ple.reshape(depth, 1, D), ple_w_gate.astype(BF16),
               ple_w_proj.astype(BF16))
    h = _post(x.reshape(T, D), [o_fox.reshape(T, fox_w), o_sb.reshape(T, fox_w)], p_all, 0,
              ev_w_out[0].astype(BF16), *stacked, row(norm_final), tm, final_norm=False)

    qw = n_q * HEAD_DIM
    kw = n_kv * HEAD_DIM
    col_scale1 = jnp.ones((qw + 2 * kw,), F32).at[:qw].set(HEAD_DIM ** -0.5 * LOG2E)
    w1t = (od_w_in[0] * col_scale1).T.astype(BF16)
    half = HEAD_DIM // 2
    inv = (ROPE_THETA ** (-jnp.arange(half, dtype=F32) / half)).reshape(half, 1)
    qt1, k1, vt1 = _pre1(h.reshape(B, S, D), row(norm_mix[1]), w1t, positions.reshape(B, 1, S),
                         inv, min(2 * tm, S), n_q, n_kv)
    o_swa = _swa(qt1, k1, vt1, od_sinks[0], n_q, n_kv)
    out = _post(h, [o_swa.reshape(T, qw)], p_all, 1, od_w_out[0].astype(BF16), *stacked,
                row(norm_final), tm, final_norm=True)
    return out.reshape(B, S, D)
```

```python
import functools

import jax
import jax.numpy as jnp
import numpy as np
from jax import lax
from jax.experimental import pallas as pl
from jax.experimental.pallas import tpu as pltpu

F32 = jnp.float32
BF16 = jnp.bfloat16

HEAD_DIM = 64
LANES = 128
BLK = 256
V_ROWS = 80
FEAT_ROWS = 16
WINDOW = 128
ROPE_THETA = 10000.0
EPS = 1e-6
NEG_INF = -1e30
LOG2E = 1.4426950408889634
SKIP_LOG2 = 60.0 * LOG2E
FOX_HEADS = 8
SWA_SUB = 4
VMEM_LIMIT = 56 * 1024 * 1024

_NT = (((1,), (1,)), ((), ()))


def _params(sem):
    return pltpu.CompilerParams(dimension_semantics=sem, vmem_limit_bytes=VMEM_LIMIT)


def _rms(x, g):
    return x * lax.rsqrt(jnp.mean(x * x, axis=-1, keepdims=True) + EPS) * g


def _log_sigmoid(x):
    return jnp.minimum(x, 0.0) - jnp.log(1.0 + jnp.exp(-jnp.abs(x)))


def _split3(x):
    a = x.astype(BF16)
    r = x - a.astype(F32)
    b = r.astype(BF16)
    c = (r - b.astype(F32)).astype(BF16)
    return a, b, c


def _pre0_kernel(x_ref, g_ref, wt_ref, wk_ref, wf_ref, bf_ref, pk_ref, aq_ref, tri_ref,
                 qtf_ref, vtf_ref, kf_ref, feat_ref, gt_ref, qn_ref, fb_ref, kmsq_ref, osb_ref,
                 carry_ref, qts_s, ks_s, vts_s, r_ref, acc_ref, *, n_fox):
    tm = x_ref.shape[1]
    fw = n_fox * HEAD_DIM
    i = pl.program_id(1)
    nb = pl.num_programs(1) - 1
    extra = i == nb
    t = jnp.minimum(i, nb - 1)
    qb = jnp.maximum(i - 1, 0)
    heads = range(n_fox)

    @pl.when(i == 0)
    def _():
        carry_ref[...] = jnp.zeros_like(carry_ref)
        qts_s[...] = jnp.zeros_like(qts_s)
        ks_s[0:tm, :] = jnp.zeros((tm, fw), BF16)
        vts_s[:, 0] = jnp.zeros((n_fox, HEAD_DIM, tm), BF16)

    rowi = lax.broadcasted_iota(jnp.int32, (BLK, tm), 0)
    coli = lax.broadcasted_iota(jnp.int32, (BLK, tm), 1)
    strict = rowi < coli
    top = lax.broadcasted_iota(jnp.int32, (LANES, tm), 0) < HEAD_DIM
    zero = jnp.zeros((LANES, tm), BF16)
    qh = []
    for h in heads:
        pair = qts_s[(h // 2) * LANES:(h // 2 + 1) * LANES, :]
        qh.append(jnp.where(top, pair, zero) if h % 2 == 0 else jnp.where(top, zero, pair))
    tri_sb = tri_ref[...]

    def kpair(k0, h):
        return ks_s[pl.ds(k0, BLK), (h // 2) * LANES:(h // 2 + 1) * LANES]

    def stick_logs(z, mask=None):
        nz = -z
        l1 = jnp.minimum(nz, 0.0) - jnp.log2(1.0 + jnp.exp2(jnp.minimum(z, nz)))
        lb = z + l1
        if mask is not None:
            l1 = jnp.where(mask, l1, 0.0)
        return lb, l1.astype(BF16), l1[0:1, :]

    def suffix(l1b):
        return jnp.dot(tri_sb, l1b, preferred_element_type=F32)

    def tproj(c, half):
        r0 = c * fw + half * (fw // 2)
        return lax.dot_general(wt_ref[r0:r0 + fw // 2, :], hb, _NT,
                               preferred_element_type=F32).astype(BF16)

    jp = jnp.maximum(qb - 1, 0)
    kp0 = pl.multiple_of(jp * BLK, BLK)
    kd0 = pl.multiple_of(qb * BLK, BLK)
    has_prev = qb > 0
    cp = jnp.where(has_prev, 0.0, NEG_INF)

    hb = _rms(x_ref[0], g_ref[...]).astype(BF16)
    gate = jnp.dot(hb, wf_ref[...], preferred_element_type=F32) + bf_ref[...]
    lf = _log_sigmoid(gate) * LOG2E

    def gate_stage(k, c):
        if k == 0:
            row = lax.broadcasted_iota(jnp.int32, (tm, tm), 0)
            col = lax.broadcasted_iota(jnp.int32, (tm, tm), 1)
            tri = jnp.where(row >= col, 1.0, 0.0).astype(BF16)
            G = jnp.zeros((tm, LANES), F32)
            for piece in _split3(lf):
                G = G + jnp.dot(tri, piece, preferred_element_type=F32)
            base = jnp.where(extra, carry_ref[1], carry_ref[0])
            fb_ref[0, 0] = base
            carry_ref[1] = base
            carry_ref[0] = base + G[tm - 1:tm, :]
            c["G"] = G
        elif k == 1:
            g_hi, g_mid, g_lo = _split3(c["G"])
            lane_t = lax.broadcasted_iota(jnp.int32, (tm, LANES), 1)
            gp = jnp.where(lane_t < n_fox, g_hi, jnp.where(lane_t < 2 * n_fox, g_mid, g_lo))
            c["kfeat"] = jnp.dot(gp, pk_ref[...], preferred_element_type=F32)
        elif k == 2:
            GT = c["G"].T
            gt_ref[0] = GT[0:8, :]
            t_hi, t_mid, t_lo = _split3(GT)
            row_t = lax.broadcasted_iota(jnp.int32, (LANES, tm), 0)
            c["gpt"] = jnp.where(row_t < n_fox, t_hi, jnp.where(row_t < 2 * n_fox, t_mid, t_lo))
        elif k == 3:
            qfeat = jnp.dot(aq_ref[...], c["gpt"], preferred_element_type=F32)
            frow = lax.broadcasted_iota(jnp.int32, (n_fox * FEAT_ROWS, 1), 0) % FEAT_ROWS
            qones = jnp.where((frow >= 3) & (frow < 6), 1.0, 0.0)
            feat_ref[0] = (qfeat + qones).astype(BF16)

    lp, ld, tp, sfp, sfd, chain = [], [], [], [], [], {}
    for h in heads:
        z2 = jnp.dot(jnp.concatenate([kpair(kp0, h), kpair(kd0, h)], axis=0), qh[h],
                     preferred_element_type=F32)
        lp.append(stick_logs(z2[0:BLK]))
        ld.append(stick_logs(z2[BLK:2 * BLK], strict))
        tp.append(tproj(h // 2, h % 2))
        if h >= 1:
            sfp.append(suffix(lp[h - 1][1]))
            sfd.append(suffix(ld[h - 1][1]))
        if 1 <= h < 5:
            gate_stage(h - 1, chain)
    sfp.append(suffix(lp[n_fox - 1][1]))
    sfd.append(suffix(ld[n_fox - 1][1]))
    qtf = jnp.concatenate(tp[0:2], axis=0)
    vtf = jnp.concatenate(tp[2:4], axis=0)
    qts_new = jnp.concatenate(tp[4:6], axis=0)
    vts = jnp.concatenate(tp[6:8], axis=0)
    kfeat = chain["kfeat"]

    a_first, kk = [], []
    nk = 2 * wk_ref.shape[1] // n_fox
    for h in heads:
        tot_d = sfd[h][0:1, :] + ld[h][2]
        tot_p = sfp[h][0:1, :] + lp[h][2]
        a_d = jnp.where(strict, jnp.exp2(ld[h][0] + sfd[h]), 0.0)
        a_p = jnp.exp2(lp[h][0] + sfp[h] + (tot_d + cp))
        a_first.append(jnp.concatenate([a_p, a_d], axis=0).astype(BF16))
        r_ref[h] = tot_d + jnp.where(has_prev, tot_p, 0.0)
        if h % 2 == 0:
            kk.append(jnp.dot(hb, wk_ref[:, (h // 2) * nk:(h // 2 + 1) * nk],
                              preferred_element_type=F32))
    kk = jnp.concatenate(kk, axis=1)
    for h in heads:
        vt2 = jnp.concatenate([vts_s[h, jp], vts_s[h, qb]], axis=1)
        acc_ref[h] = jnp.dot(vt2, a_first[h], preferred_element_type=F32)

    qtf_ref[0] = qtf
    lane_row = lax.broadcasted_iota(jnp.int32, (V_ROWS - HEAD_DIM, tm), 0)
    ones_pad = jnp.where(lane_row == 0, 1.0, 0.0).astype(BF16)
    for h in heads:
        vtf_ref[0, h, 0, 0:HEAD_DIM, :] = vtf[h * HEAD_DIM:(h + 1) * HEAD_DIM, :]
        vtf_ref[0, h, 0, HEAD_DIM:V_ROWS, :] = ones_pad
    qts_s[...] = qts_new
    for h in heads:
        vts_s[h, t] = vts[h * HEAD_DIM:(h + 1) * HEAD_DIM, :]

    q32 = qtf.astype(F32)
    qn_rows = [jnp.sqrt(jnp.sum(jnp.square(q32[h * HEAD_DIM:(h + 1) * HEAD_DIM, :]),
                                axis=0, keepdims=True)) for h in heads]
    qn_ref[0] = jnp.concatenate(qn_rows, axis=0)

    ks_s[pl.ds(pl.multiple_of(t * tm, tm), tm), :] = kk[:, fw:].astype(BF16)
    kfox = kk[:, :fw].astype(BF16)
    k32 = kfox.astype(F32)
    lane = lax.broadcasted_iota(jnp.int32, (1, LANES), 1)
    first_head = lane < HEAD_DIM
    kmsq = jnp.zeros((1, LANES), F32)
    for pr in range(n_fox // 2):
        sq = jnp.square(k32[:, pr * LANES:(pr + 1) * LANES])
        for e in range(2):
            mine = first_head if e == 0 else jnp.logical_not(first_head)
            ss = jnp.sum(jnp.where(mine, sq, 0.0), axis=-1, keepdims=True)
            kmsq = jnp.where(lane == 2 * pr + e, jnp.max(ss, axis=0, keepdims=True), kmsq)
    kmsq_ref[0, 0] = kmsq

    kones = jnp.where((lane % FEAT_ROWS < 3) & (lane < 2 * FEAT_ROWS), 1.0, 0.0)
    for pr in range(n_fox // 2):
        kf_ref[0, :, 2 * pr * LANES:(2 * pr + 1) * LANES] = kfox[:, pr * LANES:(pr + 1) * LANES]
        kf_ref[0, :, (2 * pr + 1) * LANES:(2 * pr + 2) * LANES] = (
            kfeat[:, pr * LANES:(pr + 1) * LANES] + kones).astype(BF16)

    def rmax():
        out = r_ref[0]
        for h in heads[1:]:
            out = jnp.maximum(out, r_ref[h])
        return jnp.max(out)

    def block(j):
        k0 = pl.multiple_of(j * BLK, BLK)
        z = [jnp.dot(kpair(k0, h), qh[h], preferred_element_type=F32) for h in heads]
        lg = [stick_logs(z[h]) for h in heads]
        sfx = [suffix(lg[h][1]) for h in heads]
        a = []
        for h in heads:
            r_old = r_ref[h]
            a.append(jnp.exp2(lg[h][0] + sfx[h] + r_old).astype(BF16))
            r_ref[h] = r_old + (sfx[h][0:1, :] + lg[h][2])
        for h in heads:
            acc_ref[h] = acc_ref[h] + jnp.dot(vts_s[h, j], a[h], preferred_element_type=F32)
        return rmax()

    def body(carry):
        j, _ = carry
        return j - 1, block(j)

    lax.while_loop(lambda c: (c[0] >= 0) & (c[1] > -SKIP_LOG2), body, (qb - 2, rmax()))

    osb_ref[0] = jnp.concatenate([acc_ref[h] for h in heads], axis=0).T.astype(osb_ref.dtype)


def _pre0(x, g, wt, wk, wf, bf, pk, aq, tri, n_fox):
    B, S, D = x.shape
    tm = BLK
    nb = S // tm
    fw = n_fox * HEAD_DIM
    const = lambda a: pl.BlockSpec(a.shape, lambda b, s: (0,) * a.ndim)
    tile = lambda s: jnp.minimum(s, nb - 1)
    tok_lane = lambda rows: pl.BlockSpec((1, rows, tm), lambda b, s: (b, 0, tile(s)))
    return pl.pallas_call(
        functools.partial(_pre0_kernel, n_fox=n_fox),
        grid=(B, nb + 1),
        in_specs=[pl.BlockSpec((1, tm, D), lambda b, s: (b, tile(s), 0))]
        + [const(a) for a in (g, wt, wk, wf, bf, pk, aq, tri)],
        out_specs=[
            tok_lane(fw),
            pl.BlockSpec((1, n_fox, 1, V_ROWS, tm), lambda b, s: (b, 0, tile(s), 0, 0)),
            pl.BlockSpec((1, tm, n_fox * LANES), lambda b, s: (b, tile(s), 0)),
            tok_lane(n_fox * FEAT_ROWS),
            tok_lane(8),
            tok_lane(8),
            pl.BlockSpec((1, 1, 1, LANES), lambda b, s: (b, tile(s), 0, 0)),
            pl.BlockSpec((1, 1, 1, LANES), lambda b, s: (b, tile(s), 0, 0)),
            pl.BlockSpec((1, tm, fw), lambda b, s: (b, jnp.maximum(s - 1, 0), 0)),
        ],
        out_shape=[
            jax.ShapeDtypeStruct((B, fw, S), BF16),
            jax.ShapeDtypeStruct((B, n_fox, nb, V_ROWS, tm), BF16),
            jax.ShapeDtypeStruct((B, S, n_fox * LANES), BF16),
            jax.ShapeDtypeStruct((B, n_fox * FEAT_ROWS, S), BF16),
            jax.ShapeDtypeStruct((B, 8, S), F32),
            jax.ShapeDtypeStruct((B, 8, S), F32),
            jax.ShapeDtypeStruct((B, nb, 1, LANES), F32),
            jax.ShapeDtypeStruct((B, nb, 1, LANES), F32),
            jax.ShapeDtypeStruct((B, S, fw), BF16),
        ],
        scratch_shapes=[
            pltpu.VMEM((2, 1, LANES), F32),
            pltpu.VMEM((fw, tm), BF16),
            pltpu.VMEM((S, fw), BF16),
            pltpu.VMEM((n_fox, nb, HEAD_DIM, tm), BF16),
            pltpu.VMEM((n_fox, 1, tm), F32),
            pltpu.VMEM((n_fox, HEAD_DIM, tm), F32),
        ],
        compiler_params=_params(("arbitrary", "arbitrary")),
        name="pre0",
    )(x, g, wt, wk, wf, bf, pk, aq, tri)


def _fox_kernel(fb_ref, qt_ref, feat_ref, k_ref, vt_ref, gt_ref, qn_ref, kmsq_ref,
                o_ref, m_ref, acc_ref):
    b = pl.program_id(0)
    hp = pl.program_id(1)
    qi = pl.program_id(2)
    bq = qt_ref.shape[2]
    n_h = acc_ref.shape[0]
    heads = range(n_h)
    hg = [n_h * hp + h for h in heads]
    row = lax.broadcasted_iota(jnp.int32, (BLK, bq), 0)
    col = lax.broadcasted_iota(jnp.int32, (BLK, bq), 1)
    causal = row <= col
    kmax_sq = jnp.max(kmsq_ref[0], axis=0)
    lane = lax.broadcasted_iota(jnp.int32, (1, LANES), 1)

    def zeros(rows):
        return jnp.zeros((rows, bq), BF16)

    qaug = []
    for h in heads:
        q = qt_ref[0, h * HEAD_DIM:(h + 1) * HEAD_DIM, :]
        f = feat_ref[0, h * FEAT_ROWS:(h + 1) * FEAT_ROWS, :]
        parts = [q, zeros(HEAD_DIM), f, zeros(FEAT_ROWS)] if h % 2 == 0 else \
                [zeros(HEAD_DIM), q, zeros(FEAT_ROWS), f]
        qaug.append(jnp.concatenate(parts + [zeros(LANES - 2 * FEAT_ROWS)], axis=0))

    def kpair(k0, h):
        return k_ref[0, pl.ds(k0, BLK), (h // 2) * 2 * LANES:(h // 2 + 1) * 2 * LANES]

    jp = jnp.maximum(qi - 1, 0)
    kp0 = pl.multiple_of(jp * BLK, BLK)
    kd0 = pl.multiple_of(qi * BLK, BLK)
    st = [jnp.dot(jnp.concatenate([kpair(kp0, h), kpair(kd0, h)], axis=0),
                  qaug[h], preferred_element_type=F32) for h in heads]
    p_first = []
    for h in heads:
        cp = jnp.where(qi > 0, fb_ref[b, hg[h], qi] - fb_ref[b, hg[h], jp], NEG_INF)
        st_p = st[h][0:BLK]
        st_d = jnp.where(causal, st[h][BLK:2 * BLK], NEG_INF)
        m = jnp.maximum(jnp.max(st_d, axis=0, keepdims=True),
                        jnp.max(st_p, axis=0, keepdims=True) + cp)
        m_ref[h] = m
        p_first.append(jnp.concatenate([jnp.exp2(st_p - (m - cp)), jnp.exp2(st_d - m)],
                                       axis=0).astype(BF16))
    for h in heads:
        vt2 = jnp.concatenate([vt_ref[0, h, jp], vt_ref[0, h, qi]], axis=1)
        acc_ref[h] = jnp.dot(vt2, p_first[h], preferred_element_type=F32)

    def block(j):
        k0 = pl.multiple_of(j * BLK, BLK)
        st = [jnp.dot(kpair(k0, h), qaug[h], preferred_element_type=F32)
              for h in heads]
        p, alpha = [], []
        for h in heads:
            c = fb_ref[b, hg[h], qi] - fb_ref[b, hg[h], j]
            m_old = m_ref[h]
            m_new = jnp.maximum(m_old, jnp.max(st[h], axis=0, keepdims=True) + c)
            p.append(jnp.exp2(st[h] - (m_new - c)).astype(BF16))
            alpha.append(jnp.exp2(m_old - m_new))
            m_ref[h] = m_new
        for h in heads:
            acc_ref[h] = alpha[h] * acc_ref[h] + jnp.dot(vt_ref[0, h, j], p[h],
                                                         preferred_element_type=F32)

    def margin(j):
        jn = jnp.clip(j + 1, 0, qi)
        worst = None
        for h in heads:
            kmax = jnp.sqrt(jnp.sum(jnp.where(lane == hg[h], kmax_sq, 0.0), axis=-1, keepdims=True))
            gap = fb_ref[b, hg[h], jn] - fb_ref[b, hg[h], qi]
            mh = (qn_ref[0, pl.ds(hg[h], 1), :] * kmax + gt_ref[0, pl.ds(hg[h], 1), :]
                  - m_ref[h] - gap)
            worst = mh if worst is None else jnp.maximum(worst, mh)
        return jnp.max(worst) + SKIP_LOG2

    def body(carry):
        j, _ = carry
        block(j)
        return j - 1, margin(j - 1)

    lax.while_loop(lambda c: (c[0] >= 0) & (c[1] >= 0.0), body, (qi - 2, margin(qi - 2)))

    out_t = jnp.concatenate(
        [acc_ref[h, 0:HEAD_DIM, :] / acc_ref[h, HEAD_DIM:HEAD_DIM + 1, :] for h in heads], axis=0)
    o_ref[0] = out_t.T.astype(o_ref.dtype)


def _fox(fb, qt, feat, kf, vt, gt, qn, kmsq):
    B, _, S = qt.shape
    n_h = FOX_HEADS
    n_grp = qt.shape[1] // (n_h * HEAD_DIM)
    nb = S // BLK
    return pl.pallas_call(
        _fox_kernel,
        grid_spec=pltpu.PrefetchScalarGridSpec(
            num_scalar_prefetch=1,
            grid=(B, n_grp, nb),
            in_specs=[
                pl.BlockSpec((1, n_h * HEAD_DIM, BLK), lambda b, h, i, s: (b, h, i)),
                pl.BlockSpec((1, n_h * FEAT_ROWS, BLK), lambda b, h, i, s: (b, h, i)),
                pl.BlockSpec((1, S, n_h * LANES), lambda b, h, i, s: (b, 0, h),
                             pipeline_mode=pl.Buffered(1)),
                pl.BlockSpec((1, n_h, nb, V_ROWS, BLK), lambda b, h, i, s: (b, h, 0, 0, 0),
                             pipeline_mode=pl.Buffered(1)),
                pl.BlockSpec((1, 8, BLK), lambda b, h, i, s: (b, 0, i)),
                pl.BlockSpec((1, 8, BLK), lambda b, h, i, s: (b, 0, i)),
                pl.BlockSpec((1, nb, 1, LANES), lambda b, h, i, s: (b, 0, 0, 0)),
            ],
            out_specs=pl.BlockSpec((1, BLK, n_h * HEAD_DIM), lambda b, h, i, s: (b, i, h)),
            scratch_shapes=[
                pltpu.VMEM((n_h, 1, BLK), F32),
                pltpu.VMEM((n_h, V_ROWS, BLK), F32),
            ],
        ),
        out_shape=jax.ShapeDtypeStruct((B, S, n_grp * n_h * HEAD_DIM), BF16),
        compiler_params=_params(("arbitrary", "arbitrary", "arbitrary")),
        name="fox",
    )(fb, qt, feat, kf, vt, gt, qn, kmsq)


def _pre1_kernel(x_ref, g_ref, wt_ref, pos_ref, inv_ref, qt_ref, k_ref, vt_ref, *, n_q, n_kv):
    tm = x_ref.shape[1]
    half = HEAD_DIM // 2
    hb = _rms(x_ref[0], g_ref[...]).astype(BF16)
    ang = inv_ref[...] * pos_ref[0].astype(F32)
    cos = jnp.cos(ang)
    sin = jnp.sin(ang)

    def proj_t(r0, rows):
        return lax.dot_general(wt_ref[r0:r0 + rows, :], hb, _NT, preferred_element_type=F32)

    def rope_t(x):
        x1, x2 = x[0:half], x[half:HEAD_DIM]
        return jnp.concatenate([x1 * cos - x2 * sin, x2 * cos + x1 * sin], axis=0)

    qw = n_q * HEAD_DIM
    for c in range(n_q // 4):
        qt = proj_t(c * 4 * HEAD_DIM, 4 * HEAD_DIM)
        for a in range(4):
            hq = c * 4 + a
            qt_ref[0, hq * HEAD_DIM:(hq + 1) * HEAD_DIM, :] = rope_t(
                qt[a * HEAD_DIM:(a + 1) * HEAD_DIM]).astype(BF16)
    kt = proj_t(qw, n_kv * HEAD_DIM)
    zpad = jnp.zeros((LANES - HEAD_DIM, tm), F32)
    for g in range(n_kv):
        kg = jnp.concatenate([rope_t(kt[g * HEAD_DIM:(g + 1) * HEAD_DIM]), zpad], axis=0)
        k_ref[0, :, g * LANES:(g + 1) * LANES] = kg.T.astype(BF16)
    vt = proj_t(qw + n_kv * HEAD_DIM, n_kv * HEAD_DIM).astype(BF16)
    pad_row = lax.broadcasted_iota(jnp.int32, (V_ROWS - HEAD_DIM, tm), 0)
    ones_pad = jnp.where(pad_row == 0, 1.0, 0.0).astype(BF16)
    for g in range(n_kv):
        vt_ref[0, g * V_ROWS:g * V_ROWS + HEAD_DIM, :] = vt[g * HEAD_DIM:(g + 1) * HEAD_DIM]
        vt_ref[0, g * V_ROWS + HEAD_DIM:(g + 1) * V_ROWS, :] = ones_pad


def _pre1(h, g, wt, pos, inv, tm, n_q, n_kv):
    B, S, D = h.shape
    return pl.pallas_call(
        functools.partial(_pre1_kernel, n_q=n_q, n_kv=n_kv),
        grid=(B, S // tm),
        in_specs=[
            pl.BlockSpec((1, tm, D), lambda b, i: (b, i, 0)),
            pl.BlockSpec((1, D), lambda b, i: (0, 0)),
            pl.BlockSpec(wt.shape, lambda b, i: (0, 0)),
            pl.BlockSpec((1, 1, tm), lambda b, i: (b, 0, i)),
            pl.BlockSpec(inv.shape, lambda b, i: (0, 0)),
        ],
        out_specs=[
            pl.BlockSpec((1, n_q * HEAD_DIM, tm), lambda b, i: (b, 0, i)),
            pl.BlockSpec((1, tm, n_kv * LANES), lambda b, i: (b, i, 0)),
            pl.BlockSpec((1, n_kv * V_ROWS, tm), lambda b, i: (b, 0, i)),
        ],
        out_shape=[
            jax.ShapeDtypeStruct((B, n_q * HEAD_DIM, S), BF16),
            jax.ShapeDtypeStruct((B, S, n_kv * LANES), BF16),
            jax.ShapeDtypeStruct((B, n_kv * V_ROWS, S), BF16),
        ],
        compiler_params=_params(("arbitrary", "arbitrary")),
        name="pre1",
    )(h, g, wt, pos, inv)


def _swa_kernel(sink_ref, qt_ref, kp_ref, ko_ref, vp_ref, vo_ref, o_ref, *, n_kv, group):
    i = pl.program_id(1)
    W = WINDOW
    n_sub = qt_ref.shape[2] // W
    r = lax.broadcasted_iota(jnp.int32, (2 * W, W), 0)
    c = lax.broadcasted_iota(jnp.int32, (2 * W, W), 1)
    rel = c + W - r
    band = (rel >= 0) & (rel < W)
    valid = [jnp.concatenate([band & ((r >= W) | (i > 0)) if u == 0 else band] * group, axis=1)
             for u in range(n_sub)]
    seg = lax.broadcasted_iota(jnp.int32, (1, group * W), 1) // W
    zpad = jnp.zeros((LANES - HEAD_DIM, group * W), BF16)
    chains = [(u, g) for u in range(n_sub) for g in range(n_kv)]

    def keys(u, g):
        ls = slice(g * LANES, (g + 1) * LANES)
        if u == 0:
            return jnp.concatenate([kp_ref[0, :, ls], ko_ref[0, 0:W, ls]], axis=0)
        return ko_ref[0, (u - 1) * W:(u + 1) * W, ls]

    def values(u, g):
        rs = slice(g * V_ROWS, (g + 1) * V_ROWS)
        if u == 0:
            return jnp.concatenate([vp_ref[0, rs, :], vo_ref[0, rs, 0:W]], axis=1)
        return vo_ref[0, rs, (u - 1) * W:(u + 1) * W]

    st = []
    for u, g in chains:
        qg = jnp.concatenate(
            [qt_ref[0, (g * group + a) * HEAD_DIM:(g * group + a + 1) * HEAD_DIM, u * W:(u + 1) * W]
             for a in range(group)], axis=1)
        st.append(jnp.dot(keys(u, g), jnp.concatenate([qg, zpad], axis=0),
                          preferred_element_type=F32))
    p, sink_term = [], []
    for n, (u, g) in enumerate(chains):
        sg = jnp.where(valid[u], st[n], NEG_INF)
        sink = jnp.zeros((1, group * W), F32)
        for a in range(group):
            sink = jnp.where(seg == a, sink_ref[g * group + a] * LOG2E, sink)
        m = jnp.maximum(jnp.max(sg, axis=0, keepdims=True), sink)
        p.append(jnp.exp2(sg - m).astype(BF16))
        sink_term.append(jnp.exp2(sink - m))
    acc = [jnp.dot(values(u, g), p[n], preferred_element_type=F32)
           for n, (u, g) in enumerate(chains)]
    for n, (u, g) in enumerate(chains):
        o = acc[n][0:HEAD_DIM] / (acc[n][HEAD_DIM:HEAD_DIM + 1] + sink_term[n])
        for a in range(0, group, 2):
            pair = jnp.concatenate([o[:, a * W:(a + 1) * W], o[:, (a + 1) * W:(a + 2) * W]], axis=0)
            l0 = (g * group + a) * HEAD_DIM
            o_ref[0, u * W:(u + 1) * W, l0:l0 + 2 * HEAD_DIM] = pair.T.astype(o_ref.dtype)


def _swa(qt, kpad, vt, sinks, n_q, n_kv):
    B, _, S = qt.shape
    W = WINDOW
    n_sub = SWA_SUB
    prev = lambda i: jnp.maximum(n_sub * i - 1, 0)
    return pl.pallas_call(
        functools.partial(_swa_kernel, n_kv=n_kv, group=n_q // n_kv),
        grid_spec=pltpu.PrefetchScalarGridSpec(
            num_scalar_prefetch=1,
            grid=(B, S // (n_sub * W)),
            in_specs=[
                pl.BlockSpec((1, n_q * HEAD_DIM, n_sub * W), lambda b, i, s: (b, 0, i)),
                pl.BlockSpec((1, W, n_kv * LANES), lambda b, i, s: (b, prev(i), 0)),
                pl.BlockSpec((1, n_sub * W, n_kv * LANES), lambda b, i, s: (b, i, 0)),
                pl.BlockSpec((1, n_kv * V_ROWS, W), lambda b, i, s: (b, 0, prev(i))),
                pl.BlockSpec((1, n_kv * V_ROWS, n_sub * W), lambda b, i, s: (b, 0, i)),
            ],
            out_specs=pl.BlockSpec((1, n_sub * W, n_q * HEAD_DIM), lambda b, i, s: (b, i, 0)),
        ),
        out_shape=jax.ShapeDtypeStruct((B, S, n_q * HEAD_DIM), BF16),
        compiler_params=_params(("arbitrary", "arbitrary")),
        name="swa",
    )(sinks, qt, kpad, kpad, vt, vt)


def _post_kernel(*refs, n_mix, final_norm):
    h_ref = refs[0]
    mix_refs = refs[1:1 + n_mix]
    (p_ref, wo_ref, gf_ref, wg_ref, wu_ref, wd_ref, gp_ref, wpg_ref, wpp_ref, gfin_ref,
     o_ref) = refs[1 + n_mix:]
    h = h_ref[...]
    off = 0
    for m_ref in mix_refs:
        w = m_ref.shape[1]
        h = h + jnp.dot(m_ref[...], wo_ref[off:off + w, :], preferred_element_type=F32)
        off += w
    pb = p_ref[...].astype(BF16)
    half = wpp_ref.shape[1] // 2
    pp_lo = jnp.dot(pb, wpp_ref[:, :half], preferred_element_type=F32)
    hb = _rms(h, gf_ref[...]).astype(BF16)
    g = jnp.dot(hb, wg_ref[...], preferred_element_type=F32)
    u = jnp.dot(hb, wu_ref[...], preferred_element_type=F32)
    act = (g * jax.nn.sigmoid(g) * u).astype(BF16)
    h = h + jnp.dot(act, wd_ref[...], preferred_element_type=F32)
    pp_hi = jnp.dot(pb, wpp_ref[:, half:], preferred_element_type=F32)
    gate = jax.nn.sigmoid(jnp.dot(_rms(h, gp_ref[...]).astype(BF16), wpg_ref[...],
                                  preferred_element_type=F32))
    h = h + gate * jnp.concatenate([pp_lo, pp_hi], axis=1)
    if final_norm:
        h = _rms(h, gfin_ref[...])
    o_ref[...] = h


def _post(h, mixes, p_all, layer, wo, gf, wg, wu, wd, gp, wpg, wpp, gfin, tm, final_norm):
    T, D = h.shape
    row = lambda w: pl.BlockSpec((tm, w), lambda i: (i, 0))
    full = lambda a: pl.BlockSpec(a.shape, lambda i: (0, 0))
    lay = lambda a: pl.BlockSpec((None,) + a.shape[1:], lambda i: (layer, 0, 0),
                                 pipeline_mode=pl.Buffered(1))
    return pl.pallas_call(
        functools.partial(_post_kernel, n_mix=len(mixes), final_norm=final_norm),
        grid=(T // tm,),
        in_specs=[row(D)] + [row(m.shape[1]) for m in mixes]
        + [pl.BlockSpec((None, tm, p_all.shape[2]), lambda i: (layer, i, 0))]
        + [full(wo)] + [lay(a) for a in (gf, wg, wu, wd, gp, wpg, wpp)] + [full(gfin)],
        out_specs=row(D),
        out_shape=jax.ShapeDtypeStruct((T, D), F32),
        compiler_params=_params(("arbitrary",)),
        name="post",
    )(h, *mixes, p_all, wo, gf, wg, wu, wd, gp, wpg, wpp, gfin)


def _layer0_weights(w_in, b_f, n_fox):
    fw = n_fox * HEAD_DIM
    scale = HEAD_DIM ** -0.5 * LOG2E
    D = w_in.shape[0]
    qa, ka, va, qs, ks, vs = (w_in[:, i * fw:(i + 1) * fw] for i in range(6))
    wt = jnp.concatenate([qa * scale, va, qs * scale, vs], axis=1).T.astype(BF16)
    wk = jnp.concatenate([ka, ks], axis=1).astype(BF16)
    gate_pad = ((0, 0), (0, LANES - 3 * n_fox))
    wf = jnp.pad(jnp.tile(w_in[:, 6 * fw:], (1, 3)), gate_pad).astype(BF16)
    bf = jnp.pad(jnp.tile(b_f.reshape(1, n_fox), (1, 3)), gate_pad)
    heads = np.arange(n_fox)
    pk = np.zeros((LANES, n_fox // 2 * LANES), np.float32)
    aq = np.zeros((n_fox * FEAT_ROWS, LANES), np.float32)
    for piece in range(3):
        pk[piece * n_fox + heads, heads // 2 * LANES + heads % 2 * FEAT_ROWS + 3 + piece] = -1.0
        aq[heads * FEAT_ROWS + piece, piece * n_fox + heads] = 1.0
    return wt, wk, wf, bf, jnp.asarray(pk, BF16), jnp.asarray(aq, BF16)


def kernel(x, p, positions, norm_mix, norm_ffn, norm_ple, norm_final, ev_w_in, ev_b_f, ev_w_out,
           od_w_in, od_sinks, od_w_out, ffn_w_gate, ffn_w_up, ffn_w_down, ple_w_proj, ple_w_gate):
    B, S, D = x.shape
    T = B * S
    n_heads = D // HEAD_DIM
    n_fox = n_heads // 2
    fox_w = n_fox * HEAD_DIM
    n_q, n_kv = n_heads, 4
    assert S % BLK == 0 and n_fox == 8
    row = lambda a: a.reshape(1, -1)

    wt, wk, wf, bf, pk, aq = _layer0_weights(ev_w_in[0], ev_b_f[0], n_fox)
    tri = jnp.asarray(np.arange(BLK)[None, :] > np.arange(BLK)[:, None], BF16)
    (qtf, vtf, kf, feat, gt, qn, fb, kmsq, o_sb) = _pre0(
        x, row(norm_mix[0]), wt, wk, wf, bf, pk, aq, tri, n_fox)
    fb_heads = fb[:, :, 0, :n_fox].transpose(0, 2, 1)
    o_fox = _fox(fb_heads, qtf, feat, kf, vtf, gt, qn, kmsq)

    tm = min(512, T)
    depth = norm_ffn.shape[0]
    p_all = p.reshape(depth, T, -1)
    stacked = (norm_ffn.reshape(depth, 1, D), ffn_w_gate.astype(BF16), ffn_w_up.astype(BF16),
               ffn_w_down.astype(BF16), norm_ple.reshape(depth, 1, D), ple_w_gate.astype(BF16),
               ple_w_proj.astype(BF16))
    h = _post(x.reshape(T, D), [o_fox.reshape(T, fox_w), o_sb.reshape(T, fox_w)], p_all, 0,
              ev_w_out[0].astype(BF16), *stacked, row(norm_final), tm, final_norm=False)

    qw = n_q * HEAD_DIM
    kw = n_kv * HEAD_DIM
    col_scale1 = jnp.ones((qw + 2 * kw,), F32).at[:qw].set(HEAD_DIM ** -0.5 * LOG2E)
    w1t = (od_w_in[0] * col_scale1).T.astype(BF16)
    half = HEAD_DIM // 2
    inv = (ROPE_THETA ** (-jnp.arange(half, dtype=F32) / half)).reshape(half, 1)
    qt1, k1, vt1 = _pre1(h.reshape(B, S, D), row(norm_mix[1]), w1t, positions.reshape(B, 1, S),
                         inv, min(2 * tm, S), n_q, n_kv)
    o_swa = _swa(qt1, k1, vt1, od_sinks[0], n_q, n_kv)
    out = _post(h, [o_swa.reshape(T, qw)], p_all, 1, od_w_out[0].astype(BF16), *stacked,
                row(norm_final), tm, final_norm=True)
    return out.reshape(B, S, D)
```

```python
import functools

import jax
import jax.numpy as jnp
import numpy as np
from jax import lax
from jax.experimental import pallas as pl
from jax.experimental.pallas import tpu as pltpu

F32 = jnp.float32
BF16 = jnp.bfloat16

HEAD_DIM = 64
LANES = 128
BLK = 256
V_ROWS = 80
FEAT_ROWS = 16
WINDOW = 128
ROPE_THETA = 10000.0
EPS = 1e-6
NEG_INF = -1e30
LOG2E = 1.4426950408889634
SKIP_LOG2 = 60.0 * LOG2E
FOX_HEADS = 8
SWA_SUB = 4
VMEM_LIMIT = 56 * 1024 * 1024

_NT = (((1,), (1,)), ((), ()))


def _params(sem):
    return pltpu.CompilerParams(dimension_semantics=sem, vmem_limit_bytes=VMEM_LIMIT)


def _rms(x, g):
    return x * lax.rsqrt(jnp.mean(x * x, axis=-1, keepdims=True) + EPS) * g


def _log_sigmoid(x):
    return jnp.minimum(x, 0.0) - jnp.log(1.0 + jnp.exp(-jnp.abs(x)))


def _split3(x):
    a = x.astype(BF16)
    r = x - a.astype(F32)
    b = r.astype(BF16)
    c = (r - b.astype(F32)).astype(BF16)
    return a, b, c


def _pre0_kernel(x_ref, g_ref, wt_ref, wk_ref, wf_ref, bf_ref, pk_ref, aq_ref, tri_ref,
                 qtf_ref, vtf_ref, kf_ref, feat_ref, gt_ref, qn_ref, fb_ref, kmsq_ref, osb_ref,
                 carry_ref, qts_s, ks_s, vts_s, r_ref, acc_ref, *, n_fox):
    tm = x_ref.shape[1]
    fw = n_fox * HEAD_DIM
    i = pl.program_id(1)
    nb = pl.num_programs(1) - 1
    extra = i == nb
    t = jnp.minimum(i, nb - 1)
    qb = jnp.maximum(i - 1, 0)
    heads = range(n_fox)

    @pl.when(i == 0)
    def _():
        carry_ref[...] = jnp.zeros_like(carry_ref)
        qts_s[...] = jnp.zeros_like(qts_s)
        ks_s[0:tm, :] = jnp.zeros((tm, fw), BF16)
        vts_s[:, 0] = jnp.zeros((n_fox, HEAD_DIM, tm), BF16)

    rowi = lax.broadcasted_iota(jnp.int32, (BLK, tm), 0)
    coli = lax.broadcasted_iota(jnp.int32, (BLK, tm), 1)
    strict = rowi < coli
    top = lax.broadcasted_iota(jnp.int32, (LANES, tm), 0) < HEAD_DIM
    zero = jnp.zeros((LANES, tm), BF16)
    qh = []
    for h in heads:
        pair = qts_s[(h // 2) * LANES:(h // 2 + 1) * LANES, :]
        qh.append(jnp.where(top, pair, zero) if h % 2 == 0 else jnp.where(top, zero, pair))
    tri_sb = tri_ref[...]

    def kpair(k0, h):
        return ks_s[pl.ds(k0, BLK), (h // 2) * LANES:(h // 2 + 1) * LANES]

    def stick_logs(z, mask=None):
        nz = -z
        l1 = jnp.minimum(nz, 0.0) - jnp.log2(1.0 + jnp.exp2(jnp.minimum(z, nz)))
        lb = z + l1
        if mask is not None:
            l1 = jnp.where(mask, l1, 0.0)
        return lb, l1.astype(BF16), l1[0:1, :]

    def suffix(l1b):
        return jnp.dot(tri_sb, l1b, preferred_element_type=F32)

    def tproj(c, half):
        r0 = c * fw + half * (fw // 2)
        return lax.dot_general(wt_ref[r0:r0 + fw // 2, :], hb, _NT,
                               preferred_element_type=F32).astype(BF16)

    jp = jnp.maximum(qb - 1, 0)
    kp0 = pl.multiple_of(jp * BLK, BLK)
    kd0 = pl.multiple_of(qb * BLK, BLK)
    has_prev = qb > 0
    cp = jnp.where(has_prev, 0.0, NEG_INF)

    hb = _rms(x_ref[0], g_ref[...]).astype(BF16)
    gate = jnp.dot(hb, wf_ref[...], preferred_element_type=F32) + bf_ref[...]
    lf = _log_sigmoid(gate) * LOG2E

    def gate_stage(k, c):
        if k == 0:
            row = lax.broadcasted_iota(jnp.int32, (tm, tm), 0)
            col = lax.broadcasted_iota(jnp.int32, (tm, tm), 1)
            tri = jnp.where(row >= col, 1.0, 0.0).astype(BF16)
            G = jnp.zeros((tm, LANES), F32)
            for piece in _split3(lf):
                G = G + jnp.dot(tri, piece, preferred_element_type=F32)
            base = jnp.where(extra, carry_ref[1], carry_ref[0])
            fb_ref[0, 0] = base
            carry_ref[1] = base
            carry_ref[0] = base + G[tm - 1:tm, :]
            c["G"] = G
        elif k == 1:
            g_hi, g_mid, g_lo = _split3(c["G"])
            lane_t = lax.broadcasted_iota(jnp.int32, (tm, LANES), 1)
            gp = jnp.where(lane_t < n_fox, g_hi, jnp.where(lane_t < 2 * n_fox, g_mid, g_lo))
            c["kfeat"] = jnp.dot(gp, pk_ref[...], preferred_element_type=F32)
        elif k == 2:
            GT = c["G"].T
            gt_ref[0] = GT[0:8, :]
            t_hi, t_mid, t_lo = _split3(GT)
            row_t = lax.broadcasted_iota(jnp.int32, (LANES, tm), 0)
            c["gpt"] = jnp.where(row_t < n_fox, t_hi, jnp.where(row_t < 2 * n_fox, t_mid, t_lo))
        elif k == 3:
            qfeat = jnp.dot(aq_ref[...], c["gpt"], preferred_element_type=F32)
            frow = lax.broadcasted_iota(jnp.int32, (n_fox * FEAT_ROWS, 1), 0) % FEAT_ROWS
            qones = jnp.where((frow >= 3) & (frow < 6), 1.0, 0.0)
            feat_ref[0] = (qfeat + qones).astype(BF16)

    lp, ld, tp, sfp, sfd, chain = [], [], [], [], [], {}
    for h in heads:
        z2 = jnp.dot(jnp.concatenate([kpair(kp0, h), kpair(kd0, h)], axis=0), qh[h],
                     preferred_element_type=F32)
        lp.append(stick_logs(z2[0:BLK]))
        ld.append(stick_logs(z2[BLK:2 * BLK], strict))
        tp.append(tproj(h // 2, h % 2))
        if h >= 1:
            sfp.append(suffix(lp[h - 1][1]))
            sfd.append(suffix(ld[h - 1][1]))
        if 1 <= h < 5:
            gate_stage(h - 1, chain)
    sfp.append(suffix(lp[n_fox - 1][1]))
    sfd.append(suffix(ld[n_fox - 1][1]))
    qtf = jnp.concatenate(tp[0:2], axis=0)
    vtf = jnp.concatenate(tp[2:4], axis=0)
    qts_new = jnp.concatenate(tp[4:6], axis=0)
    vts = jnp.concatenate(tp[6:8], axis=0)
    kfeat = chain["kfeat"]

    a_first, kk = [], []
    nk = 2 * wk_ref.shape[1] // n_fox
    for h in heads:
        tot_d = sfd[h][0:1, :] + ld[h][2]
        tot_p = sfp[h][0:1, :] + lp[h][2]
        a_d = jnp.where(strict, jnp.exp2(ld[h][0] + sfd[h]), 0.0)
        a_p = jnp.exp2(lp[h][0] + sfp[h] + (tot_d + cp))
        a_first.append(jnp.concatenate([a_p, a_d], axis=0).astype(BF16))
        r_ref[h] = tot_d + jnp.where(has_prev, tot_p, 0.0)
        if h % 2 == 0:
            kk.append(jnp.dot(hb, wk_ref[:, (h // 2) * nk:(h // 2 + 1) * nk],
                              preferred_element_type=F32))
    kk = jnp.concatenate(kk, axis=1)
    for h in heads:
        vt2 = jnp.concatenate([vts_s[h, jp], vts_s[h, qb]], axis=1)
        acc_ref[h] = jnp.dot(vt2, a_first[h], preferred_element_type=F32)

    qtf_ref[0] = qtf
    lane_row = lax.broadcasted_iota(jnp.int32, (V_ROWS - HEAD_DIM, tm), 0)
    ones_pad = jnp.where(lane_row == 0, 1.0, 0.0).astype(BF16)
    for h in heads:
        vtf_ref[0, h, 0, 0:HEAD_DIM, :] = vtf[h * HEAD_DIM:(h + 1) * HEAD_DIM, :]
        vtf_ref[0, h, 0, HEAD_DIM:V_ROWS, :] = ones_pad
    qts_s[...] = qts_new
    for h in heads:
        vts_s[h, t] = vts[h * HEAD_DIM:(h + 1) * HEAD_DIM, :]

    q32 = qtf.astype(F32)
    qn_rows = [jnp.sqrt(jnp.sum(jnp.square(q32[h * HEAD_DIM:(h + 1) * HEAD_DIM, :]),
                                axis=0, keepdims=True)) for h in heads]
    qn_ref[0] = jnp.concatenate(qn_rows, axis=0)

    ks_s[pl.ds(pl.multiple_of(t * tm, tm), tm), :] = kk[:, fw:].astype(BF16)
    kfox = kk[:, :fw].astype(BF16)
    k32 = kfox.astype(F32)
    lane = lax.broadcasted_iota(jnp.int32, (1, LANES), 1)
    first_head = lane < HEAD_DIM
    kmsq = jnp.zeros((1, LANES), F32)
    for pr in range(n_fox // 2):
        sq = jnp.square(k32[:, pr * LANES:(pr + 1) * LANES])
        for e in range(2):
            mine = first_head if e == 0 else jnp.logical_not(first_head)
            ss = jnp.sum(jnp.where(mine, sq, 0.0), axis=-1, keepdims=True)
            kmsq = jnp.where(lane == 2 * pr + e, jnp.max(ss, axis=0, keepdims=True), kmsq)
    kmsq_ref[0, 0] = kmsq

    kones = jnp.where((lane % FEAT_ROWS < 3) & (lane < 2 * FEAT_ROWS), 1.0, 0.0)
    for pr in range(n_fox // 2):
        kf_ref[0, :, 2 * pr * LANES:(2 * pr + 1) * LANES] = kfox[:, pr * LANES:(pr + 1) * LANES]
        kf_ref[0, :, (2 * pr + 1) * LANES:(2 * pr + 2) * LANES] = (
            kfeat[:, pr * LANES:(pr + 1) * LANES] + kones).astype(BF16)

    def rmax():
        out = r_ref[0]
        for h in heads[1:]:
            out = jnp.maximum(out, r_ref[h])
        return jnp.max(out)

    def block(j):
        k0 = pl.multiple_of(j * BLK, BLK)
        z = [jnp.dot(kpair(k0, h), qh[h], preferred_element_type=F32) for h in heads]
        lg = [stick_logs(z[h]) for h in heads]
        sfx = [suffix(lg[h][1]) for h in heads]
        a = []
        for h in heads:
            r_old = r_ref[h]
            a.append(jnp.exp2(lg[h][0] + sfx[h] + r_old).astype(BF16))
            r_ref[h] = r_old + (sfx[h][0:1, :] + lg[h][2])
        for h in heads:
            acc_ref[h] = acc_ref[h] + jnp.dot(vts_s[h, j], a[h], preferred_element_type=F32)
        return rmax()

    def body(carry):
        j, _ = carry
        return j - 1, block(j)

    lax.while_loop(lambda c: (c[0] >= 0) & (c[1] > -SKIP_LOG2), body, (qb - 2, rmax()))

    osb_ref[0] = jnp.concatenate([acc_ref[h] for h in heads], axis=0).T.astype(osb_ref.dtype)


def _pre0(x, g, wt, wk, wf, bf, pk, aq, tri, n_fox):
    B, S, D = x.shape
    tm = BLK
    nb = S // tm
    fw = n_fox * HEAD_DIM
    const = lambda a: pl.BlockSpec(a.shape, lambda b, s: (0,) * a.ndim)
    tile = lambda s: jnp.minimum(s, nb - 1)
    tok_lane = lambda rows: pl.BlockSpec((1, rows, tm), lambda b, s: (b, 0, tile(s)))
    return pl.pallas_call(
        functools.partial(_pre0_kernel, n_fox=n_fox),
        grid=(B, nb + 1),
        in_specs=[pl.BlockSpec((1, tm, D), lambda b, s: (b, tile(s), 0))]
        + [const(a) for a in (g, wt, wk, wf, bf, pk, aq, tri)],
        out_specs=[
            tok_lane(fw),
            pl.BlockSpec((1, n_fox, 1, V_ROWS, tm), lambda b, s: (b, 0, tile(s), 0, 0)),
            pl.BlockSpec((1, tm, n_fox * LANES), lambda b, s: (b, tile(s), 0)),
            tok_lane(n_fox * FEAT_ROWS),
            tok_lane(8),
            tok_lane(8),
            pl.BlockSpec((1, 1, 1, LANES), lambda b, s: (b, tile(s), 0, 0)),
            pl.BlockSpec((1, 1, 1, LANES), lambda b, s: (b, tile(s), 0, 0)),
            pl.BlockSpec((1, tm, fw), lambda b, s: (b, jnp.maximum(s - 1, 0), 0)),
        ],
        out_shape=[
            jax.ShapeDtypeStruct((B, fw, S), BF16),
            jax.ShapeDtypeStruct((B, n_fox, nb, V_ROWS, tm), BF16),
            jax.ShapeDtypeStruct((B, S, n_fox * LANES), BF16),
            jax.ShapeDtypeStruct((B, n_fox * FEAT_ROWS, S), BF16),
            jax.ShapeDtypeStruct((B, 8, S), F32),
            jax.ShapeDtypeStruct((B, 8, S), F32),
            jax.ShapeDtypeStruct((B, nb, 1, LANES), F32),
            jax.ShapeDtypeStruct((B, nb, 1, LANES), F32),
            jax.ShapeDtypeStruct((B, S, fw), BF16),
        ],
        scratch_shapes=[
            pltpu.VMEM((2, 1, LANES), F32),
            pltpu.VMEM((fw, tm), BF16),
            pltpu.VMEM((S, fw), BF16),
            pltpu.VMEM((n_fox, nb, HEAD_DIM, tm), BF16),
            pltpu.VMEM((n_fox, 1, tm), F32),
            pltpu.VMEM((n_fox, HEAD_DIM, tm), F32),
        ],
        compiler_params=_params(("arbitrary", "arbitrary")),
        name="pre0",
    )(x, g, wt, wk, wf, bf, pk, aq, tri)


def _fox_kernel(fb_ref, qt_ref, feat_ref, k_ref, vt_ref, gt_ref, qn_ref, kmsq_ref,
                o_ref, m_ref, acc_ref):
    b = pl.program_id(0)
    hp = pl.program_id(1)
    qi = pl.program_id(2)
    bq = qt_ref.shape[2]
    n_h = acc_ref.shape[0]
    heads = range(n_h)
    hg = [n_h * hp + h for h in heads]
    row = lax.broadcasted_iota(jnp.int32, (BLK, bq), 0)
    col = lax.broadcasted_iota(jnp.int32, (BLK, bq), 1)
    causal = row <= col
    kmax_sq = jnp.max(kmsq_ref[0], axis=0)
    lane = lax.broadcasted_iota(jnp.int32, (1, LANES), 1)

    def zeros(rows):
        return jnp.zeros((rows, bq), BF16)

    qaug = []
    for h in heads:
        q = qt_ref[0, h * HEAD_DIM:(h + 1) * HEAD_DIM, :]
        f = feat_ref[0, h * FEAT_ROWS:(h + 1) * FEAT_ROWS, :]
        parts = [q, zeros(HEAD_DIM), f, zeros(FEAT_ROWS)] if h % 2 == 0 else \
                [zeros(HEAD_DIM), q, zeros(FEAT_ROWS), f]
        qaug.append(jnp.concatenate(parts + [zeros(LANES - 2 * FEAT_ROWS)], axis=0))

    def kpair(k0, h):
        return k_ref[0, pl.ds(k0, BLK), (h // 2) * 2 * LANES:(h // 2 + 1) * 2 * LANES]

    jp = jnp.maximum(qi - 1, 0)
    kp0 = pl.multiple_of(jp * BLK, BLK)
    kd0 = pl.multiple_of(qi * BLK, BLK)
    st = [jnp.dot(jnp.concatenate([kpair(kp0, h), kpair(kd0, h)], axis=0),
                  qaug[h], preferred_element_type=F32) for h in heads]
    p_first = []
    for h in heads:
        cp = jnp.where(qi > 0, fb_ref[b, hg[h], qi] - fb_ref[b, hg[h], jp], NEG_INF)
        st_p = st[h][0:BLK]
        st_d = jnp.where(causal, st[h][BLK:2 * BLK], NEG_INF)
        m = jnp.maximum(jnp.max(st_d, axis=0, keepdims=True),
                        jnp.max(st_p, axis=0, keepdims=True) + cp)
        m_ref[h] = m
        p_first.append(jnp.concatenate([jnp.exp2(st_p - (m - cp)), jnp.exp2(st_d - m)],
                                       axis=0).astype(BF16))
    for h in heads:
        vt2 = jnp.concatenate([vt_ref[0, h, jp], vt_ref[0, h, qi]], axis=1)
        acc_ref[h] = jnp.dot(vt2, p_first[h], preferred_element_type=F32)

    def block(j):
        k0 = pl.multiple_of(j * BLK, BLK)
        st = [jnp.dot(kpair(k0, h), qaug[h], preferred_element_type=F32)
              for h in heads]
        p, alpha = [], []
        for h in heads:
            c = fb_ref[b, hg[h], qi] - fb_ref[b, hg[h], j]
            m_old = m_ref[h]
            m_new = jnp.maximum(m_old, jnp.max(st[h], axis=0, keepdims=True) + c)
            p.append(jnp.exp2(st[h] - (m_new - c)).astype(BF16))
            alpha.append(jnp.exp2(m_old - m_new))
            m_ref[h] = m_new
        for h in heads:
            acc_ref[h] = alpha[h] * acc_ref[h] + jnp.dot(vt_ref[0, h, j], p[h],
                                                         preferred_element_type=F32)

    def margin(j):
        jn = jnp.clip(j + 1, 0, qi)
        worst = None
        for h in heads:
            kmax = jnp.sqrt(jnp.sum(jnp.where(lane == hg[h], kmax_sq, 0.0), axis=-1, keepdims=True))
            gap = fb_ref[b, hg[h], jn] - fb_ref[b, hg[h], qi]
            mh = (qn_ref[0, pl.ds(hg[h], 1), :] * kmax + gt_ref[0, pl.ds(hg[h], 1), :]
                  - m_ref[h] - gap)
            worst = mh if worst is None else jnp.maximum(worst, mh)
        return jnp.max(worst) + SKIP_LOG2

    def body(carry):
        j, _ = carry
        block(j)
        return j - 1, margin(j - 1)

    lax.while_loop(lambda c: (c[0] >= 0) & (c[1] >= 0.0), body, (qi - 2, margin(qi - 2)))

    out_t = jnp.concatenate(
        [acc_ref[h, 0:HEAD_DIM, :] / acc_ref[h, HEAD_DIM:HEAD_DIM + 1, :] for h in heads], axis=0)
    o_ref[0] = out_t.T.astype(o_ref.dtype)


def _fox(fb, qt, feat, kf, vt, gt, qn, kmsq):
    B, _, S = qt.shape
    n_h = FOX_HEADS
    n_grp = qt.shape[1] // (n_h * HEAD_DIM)
    nb = S // BLK
    return pl.pallas_call(
        _fox_kernel,
        grid_spec=pltpu.PrefetchScalarGridSpec(
            num_scalar_prefetch=1,
            grid=(B, n_grp, nb),
            in_specs=[
                pl.BlockSpec((1, n_h * HEAD_DIM, BLK), lambda b, h, i, s: (b, h, i)),
                pl.BlockSpec((1, n_h * FEAT_ROWS, BLK), lambda b, h, i, s: (b, h, i)),
                pl.BlockSpec((1, S, n_h * LANES), lambda b, h, i, s: (b, 0, h),
                             pipeline_mode=pl.Buffered(1)),
                pl.BlockSpec((1, n_h, nb, V_ROWS, BLK), lambda b, h, i, s: (b, h, 0, 0, 0),
                             pipeline_mode=pl.Buffered(1)),
                pl.BlockSpec((1, 8, BLK), lambda b, h, i, s: (b, 0, i)),
                pl.BlockSpec((1, 8, BLK), lambda b, h, i, s: (b, 0, i)),
                pl.BlockSpec((1, nb, 1, LANES), lambda b, h, i, s: (b, 0, 0, 0)),
            ],
            out_specs=pl.BlockSpec((1, BLK, n_h * HEAD_DIM), lambda b, h, i, s: (b, i, h)),
            scratch_shapes=[
                pltpu.VMEM((n_h, 1, BLK), F32),
                pltpu.VMEM((n_h, V_ROWS, BLK), F32),
            ],
        ),
        out_shape=jax.ShapeDtypeStruct((B, S, n_grp * n_h * HEAD_DIM), BF16),
        compiler_params=_params(("arbitrary", "arbitrary", "arbitrary")),
        name="fox",
    )(fb, qt, feat, kf, vt, gt, qn, kmsq)


def _pre1_kernel(x_ref, g_ref, wt_ref, pos_ref, inv_ref, qt_ref, k_ref, vt_ref, *, n_q, n_kv):
    tm = x_ref.shape[1]
    half = HEAD_DIM // 2
    hb = _rms(x_ref[0], g_ref[...]).astype(BF16)
    ang = inv_ref[...] * pos_ref[0].astype(F32)
    cos = jnp.cos(ang)
    sin = jnp.sin(ang)

    def proj_t(r0, rows):
        return lax.dot_general(wt_ref[r0:r0 + rows, :], hb, _NT, preferred_element_type=F32)

    def rope_t(x):
        x1, x2 = x[0:half], x[half:HEAD_DIM]
        return jnp.concatenate([x1 * cos - x2 * sin, x2 * cos + x1 * sin], axis=0)

    qw = n_q * HEAD_DIM
    for c in range(n_q // 4):
        qt = proj_t(c * 4 * HEAD_DIM, 4 * HEAD_DIM)
        for a in range(4):
            hq = c * 4 + a
            qt_ref[0, hq * HEAD_DIM:(hq + 1) * HEAD_DIM, :] = rope_t(
                qt[a * HEAD_DIM:(a + 1) * HEAD_DIM]).astype(BF16)
    kt = proj_t(qw, n_kv * HEAD_DIM)
    zpad = jnp.zeros((LANES - HEAD_DIM, tm), F32)
    for g in range(n_kv):
        kg = jnp.concatenate([rope_t(kt[g * HEAD_DIM:(g + 1) * HEAD_DIM]), zpad], axis=0)
        k_ref[0, :, g * LANES:(g + 1) * LANES] = kg.T.astype(BF16)
    vt = proj_t(qw + n_kv * HEAD_DIM, n_kv * HEAD_DIM).astype(BF16)
    pad_row = lax.broadcasted_iota(jnp.int32, (V_ROWS - HEAD_DIM, tm), 0)
    ones_pad = jnp.where(pad_row == 0, 1.0, 0.0).astype(BF16)
    for g in range(n_kv):
        vt_ref[0, g * V_ROWS:g * V_ROWS + HEAD_DIM, :] = vt[g * HEAD_DIM:(g + 1) * HEAD_DIM]
        vt_ref[0, g * V_ROWS + HEAD_DIM:(g + 1) * V_ROWS, :] = ones_pad


def _pre1(h, g, wt, pos, inv, tm, n_q, n_kv):
    B, S, D = h.shape
    return pl.pallas_call(
        functools.partial(_pre1_kernel, n_q=n_q, n_kv=n_kv),
        grid=(B, S // tm),
        in_specs=[
            pl.BlockSpec((1, tm, D), lambda b, i: (b, i, 0)),
            pl.BlockSpec((1, D), lambda b, i: (0, 0)),
            pl.BlockSpec(wt.shape, lambda b, i: (0, 0)),
            pl.BlockSpec((1, 1, tm), lambda b, i: (b, 0, i)),
            pl.BlockSpec(inv.shape, lambda b, i: (0, 0)),
        ],
        out_specs=[
            pl.BlockSpec((1, n_q * HEAD_DIM, tm), lambda b, i: (b, 0, i)),
            pl.BlockSpec((1, tm, n_kv * LANES), lambda b, i: (b, i, 0)),
            pl.BlockSpec((1, n_kv * V_ROWS, tm), lambda b, i: (b, 0, i)),
        ],
        out_shape=[
            jax.ShapeDtypeStruct((B, n_q * HEAD_DIM, S), BF16),
            jax.ShapeDtypeStruct((B, S, n_kv * LANES), BF16),
            jax.ShapeDtypeStruct((B, n_kv * V_ROWS, S), BF16),
        ],
        compiler_params=_params(("arbitrary", "arbitrary")),
        name="pre1",
    )(h, g, wt, pos, inv)


def _swa_kernel(sink_ref, qt_ref, kp_ref, ko_ref, vp_ref, vo_ref, o_ref, *, n_kv, group):
    i = pl.program_id(1)
    W = WINDOW
    n_sub = qt_ref.shape[2] // W
    r = lax.broadcasted_iota(jnp.int32, (2 * W, W), 0)
    c = lax.broadcasted_iota(jnp.int32, (2 * W, W), 1)
    rel = c + W - r
    band = (rel >= 0) & (rel < W)
    valid = [jnp.concatenate([band & ((r >= W) | (i > 0)) if u == 0 else band] * group, axis=1)
             for u in range(n_sub)]
    seg = lax.broadcasted_iota(jnp.int32, (1, group * W), 1) // W
    zpad = jnp.zeros((LANES - HEAD_DIM, group * W), BF16)
    chains = [(u, g) for u in range(n_sub) for g in range(n_kv)]

    def keys(u, g):
        ls = slice(g * LANES, (g + 1) * LANES)
        if u == 0:
            return jnp.concatenate([kp_ref[0, :, ls], ko_ref[0, 0:W, ls]], axis=0)
        return ko_ref[0, (u - 1) * W:(u + 1) * W, ls]

    def values(u, g):
        rs = slice(g * V_ROWS, (g + 1) * V_ROWS)
        if u == 0:
            return jnp.concatenate([vp_ref[0, rs, :], vo_ref[0, rs, 0:W]], axis=1)
        return vo_ref[0, rs, (u - 1) * W:(u + 1) * W]

    st = []
    for u, g in chains:
        qg = jnp.concatenate(
            [qt_ref[0, (g * group + a) * HEAD_DIM:(g * group + a + 1) * HEAD_DIM, u * W:(u + 1) * W]
             for a in range(group)], axis=1)
        st.append(jnp.dot(keys(u, g), jnp.concatenate([qg, zpad], axis=0),
                          preferred_element_type=F32))
    p, sink_term = [], []
    for n, (u, g) in enumerate(chains):
        sg = jnp.where(valid[u], st[n], NEG_INF)
        sink = jnp.zeros((1, group * W), F32)
        for a in range(group):
            sink = jnp.where(seg == a, sink_ref[g * group + a] * LOG2E, sink)
        m = jnp.maximum(jnp.max(sg, axis=0, keepdims=True), sink)
        p.append(jnp.exp2(sg - m).astype(BF16))
        sink_term.append(jnp.exp2(sink - m))
    acc = [jnp.dot(values(u, g), p[n], preferred_element_type=F32)
           for n, (u, g) in enumerate(chains)]
    for n, (u, g) in enumerate(chains):
        o = acc[n][0:HEAD_DIM] / (acc[n][HEAD_DIM:HEAD_DIM + 1] + sink_term[n])
        for a in range(0, group, 2):
            pair = jnp.concatenate([o[:, a * W:(a + 1) * W], o[:, (a + 1) * W:(a + 2) * W]], axis=0)
            l0 = (g * group + a) * HEAD_DIM
            o_ref[0, u * W:(u + 1) * W, l0:l0 + 2 * HEAD_DIM] = pair.T.astype(o_ref.dtype)


def _swa(qt, kpad, vt, sinks, n_q, n_kv):
    B, _, S = qt.shape
    W = WINDOW
    n_sub = SWA_SUB
    prev = lambda i: jnp.maximum(n_sub * i - 1, 0)
    return pl.pallas_call(
        functools.partial(_swa_kernel, n_kv=n_kv, group=n_q // n_kv),
        grid_spec=pltpu.PrefetchScalarGridSpec(
            num_scalar_prefetch=1,
            grid=(B, S // (n_sub * W)),
            in_specs=[
                pl.BlockSpec((1, n_q * HEAD_DIM, n_sub * W), lambda b, i, s: (b, 0, i)),
                pl.BlockSpec((1, W, n_kv * LANES), lambda b, i, s: (b, prev(i), 0)),
                pl.BlockSpec((1, n_sub * W, n_kv * LANES), lambda b, i, s: (b, i, 0)),
                pl.BlockSpec((1, n_kv * V_ROWS, W), lambda b, i, s: (b, 0, prev(i))),
                pl.BlockSpec((1, n_kv * V_ROWS, n_sub * W), lambda b, i, s: (b, 0, i)),
            ],
            out_specs=pl.BlockSpec((1, n_sub * W, n_q * HEAD_DIM), lambda b, i, s: (b, i, 0)),
        ),
        out_shape=jax.ShapeDtypeStruct((B, S, n_q * HEAD_DIM), BF16),
        compiler_params=_params(("arbitrary", "arbitrary")),
        name="swa",
    )(sinks, qt, kpad, kpad, vt, vt)


def _post_kernel(*refs, n_mix, final_norm):
    h_ref = refs[0]
    mix_refs = refs[1:1 + n_mix]
    (p_ref, wo_ref, gf_ref, wg_ref, wu_ref, wd_ref, gp_ref, wpg_ref, wpp_ref, gfin_ref,
     o_ref) = refs[1 + n_mix:]
    h = h_ref[...]
    off = 0
    for m_ref in mix_refs:
        w = m_ref.shape[1]
        h = h + jnp.dot(m_ref[...], wo_ref[off:off + w, :], preferred_element_type=F32)
        off += w
    pb = p_ref[...].astype(BF16)
    half = wpp_ref.shape[1] // 2
    pp_lo = jnp.dot(pb, wpp_ref[:, :half], preferred_element_type=F32)
    hb = _rms(h, gf_ref[...]).astype(BF16)
    g = jnp.dot(hb, wg_ref[...], preferred_element_type=F32)
    u = jnp.dot(hb, wu_ref[...], preferred_element_type=F32)
    act = (g * jax.nn.sigmoid(g) * u).astype(BF16)
    h = h + jnp.dot(act, wd_ref[...], preferred_element_type=F32)
    pp_hi = jnp.dot(pb, wpp_ref[:, half:], preferred_element_type=F32)
    gate = jax.nn.sigmoid(jnp.dot(_rms(h, gp_ref[...]).astype(BF16), wpg_ref[...],
                                  preferred_element_type=F32))
    h = h + gate * jnp.concatenate([pp_lo, pp_hi], axis=1)
    if final_norm:
        h = _rms(h, gfin_ref[...])
    o_ref[...] = h


def _post(h, mixes, p_all, layer, wo, gf, wg, wu, wd, gp, wpg, wpp, gfin, tm, final_norm):
    T, D = h.shape
    row = lambda w: pl.BlockSpec((tm, w), lambda i: (i, 0))
    full = lambda a: pl.BlockSpec(a.shape, lambda i: (0, 0))
    lay = lambda a: pl.BlockSpec((None,) + a.shape[1:], lambda i: (layer, 0, 0),
                                 pipeline_mode=pl.Buffered(1))
    return pl.pallas_call(
        functools.partial(_post_kernel, n_mix=len(mixes), final_norm=final_norm),
        grid=(T // tm,),
        in_specs=[row(D)] + [row(m.shape[1]) for m in mixes]
        + [pl.BlockSpec((None, tm, p_all.shape[2]), lambda i: (layer, i, 0))]
        + [full(wo)] + [lay(a) for a in (gf, wg, wu, wd, gp, wpg, wpp)] + [full(gfin)],
        out_specs=row(D),
        out_shape=jax.ShapeDtypeStruct((T, D), F32),
        compiler_params=_params(("arbitrary",)),
        name="post",
    )(h, *mixes, p_all, wo, gf, wg, wu, wd, gp, wpg, wpp, gfin)


def _layer0_weights(w_in, b_f, n_fox):
    fw = n_fox * HEAD_DIM
    scale = HEAD_DIM ** -0.5 * LOG2E
    D = w_in.shape[0]
    qa, ka, va, qs, ks, vs = (w_in[:, i * fw:(i + 1) * fw] for i in range(6))
    wt = jnp.concatenate([qa * scale, va, qs * scale, vs], axis=1).T.astype(BF16)
    wk = jnp.concatenate([ka, ks], axis=1).astype(BF16)
    gate_pad = ((0, 0), (0, LANES - 3 * n_fox))
    wf = jnp.pad(jnp.tile(w_in[:, 6 * fw:], (1, 3)), gate_pad).astype(BF16)
    bf = jnp.pad(jnp.tile(b_f.reshape(1, n_fox), (1, 3)), gate_pad)
    heads = np.arange(n_fox)
    pk = np.zeros((LANES, n_fox // 2 * LANES), np.float32)
    aq = np.zeros((n_fox * FEAT_ROWS, LANES), np.float32)
    for piece in range(3):
        pk[piece * n_fox + heads, heads // 2 * LANES + heads % 2 * FEAT_ROWS + 3 + piece] = -1.0
        aq[heads * FEAT_ROWS + piece, piece * n_fox + heads] = 1.0
    return wt, wk, wf, bf, jnp.asarray(pk, BF16), jnp.asarray(aq, BF16)


def kernel(x, p, positions, norm_mix, norm_ffn, norm_ple, norm_final, ev_w_in, ev_b_f, ev_w_out,
           od_w_in, od_sinks, od_w_out, ffn_w_gate, ffn_w_up, ffn_w_down, ple_w_proj, ple_w_gate):
    B, S, D = x.shape
    T = B * S
    n_heads = D // HEAD_DIM
    n_fox = n_heads // 2
    fox_w = n_fox * HEAD_DIM
    n_q, n_kv = n_heads, 4
    assert S % BLK == 0 and n_fox == 8
    row = lambda a: a.reshape(1, -1)

    wt, wk, wf, bf, pk, aq = _layer0_weights(ev_w_in[0], ev_b_f[0], n_fox)
    tri = jnp.asarray(np.arange(BLK)[None, :] > np.arange(BLK)[:, None], BF16)
    (qtf, vtf, kf, feat, gt, qn, fb, kmsq, o_sb) = _pre0(
        x, row(norm_mix[0]), wt, wk, wf, bf, pk, aq, tri, n_fox)
    fb_heads = fb[:, :, 0, :n_fox].transpose(0, 2, 1)
    o_fox = _fox(fb_heads, qtf, feat, kf, vtf, gt, qn, kmsq)

    tm = min(512, T)
    depth = norm_ffn.shape[0]
    p_all = p.reshape(depth, T, -1)
    stacked = (norm_ffn.reshape(depth, 1, D), ffn_w_gate.astype(BF16), ffn_w_up.astype(BF16),
               ffn_w_down.astype(BF16), norm_ple.reshape(depth, 1, D), ple_w_gate.astype(BF16),
               ple_w_proj.astype(BF16))
    h = _post(x.reshape(T, D), [o_fox.reshape(T, fox_w), o_sb.reshape(T, fox_w)], p_all, 0,
              ev_w_out[0].astype(BF16), *stacked, row(norm_final), tm, final_norm=False)

    qw = n_q * HEAD_DIM
    kw = n_kv * HEAD_DIM
    col_scale1 = jnp.ones((qw + 2 * kw,), F32).at[:qw].set(HEAD_DIM ** -0.5 * LOG2E)
    w1t = (od_w_in[0] * col_scale1).T.astype(BF16)
    half = HEAD_DIM // 2
    inv = (ROPE_THETA ** (-jnp.arange(half, dtype=F32) / half)).reshape(half, 1)
    qt1, k1, vt1 = _pre1(h.reshape(B, S, D), row(norm_mix[1]), w1t, positions.reshape(B, 1, S),
                         inv, min(4 * tm, S), n_q, n_kv)
    o_swa = _swa(qt1, k1, vt1, od_sinks[0], n_q, n_kv)
    out = _post(h, [o_swa.reshape(T, qw)], p_all, 1, od_w_out[0].astype(BF16), *stacked,
                row(norm_final), tm, final_norm=True)
    return out.reshape(B, S, D)
```
